```python
import math
import jax, jax.numpy as jnp
from jax import lax
import numpy as np

D_MODEL = 1024
BATCH = 2
SEQ = 8192
DEPTH = 4
DEC_BATCH = 8
DEC_SEQ = 16
PAST_LEN = 4096

CHUNK = 64
N_META = 16
CONV_K = 4
EPS = 1e-6
N_SSD = (DEPTH + 1) // 2
N_GDN = DEPTH // 2
SSD_INNER = 2 * D_MODEL
SSD_HEADDIM = 64
SSD_HEADS = SSD_INNER // SSD_HEADDIM
SSD_GROUPS = 4
SSD_STATE = 128
SSD_CONV_DIM = SSD_INNER + 2 * SSD_GROUPS * SSD_STATE
SSD_IN_DIM = SSD_INNER + SSD_CONV_DIM + SSD_HEADS
GDN_DK = 128
GDN_DV = 256
GDN_HEADS = D_MODEL // GDN_DK
GDN_KEY_DIM = GDN_HEADS * GDN_DK
GDN_VAL_DIM = GDN_HEADS * GDN_DV
GDN_CONV_DIM = 2 * GDN_KEY_DIM + GDN_VAL_DIM
GDN_IN_DIM = GDN_CONV_DIM + GDN_VAL_DIM + 2 * GDN_HEADS

kernel_name = 'hybrid_ssd_gdn_stream'


def rmsnorm(x, w):
    x32 = x.astype(jnp.float32)
    y = x32 * lax.rsqrt(jnp.mean(x32 * x32, axis=-1, keepdims=True) + EPS)
    return (y * w.astype(jnp.float32)).astype(x.dtype)


def causal_conv(u, buf, w, b):
    T = u.shape[1]
    xp = jnp.concatenate([buf.astype(u.dtype), u], axis=1)
    out = xp[:, 0:T] * w[0]
    for k in range(1, CONV_K):
        out = out + xp[:, k:k + T] * w[k]
    if b is not None:
        out = out + b
    return out, xp[:, -(CONV_K - 1):]


def _segments(T, with_meta):
    if with_meta:
        return ((0, N_META, N_META), (N_META, T, CHUNK))
    return ((0, T, min(T, CHUNK)),)


def ssd_scan(xs, dt, a, bm, cm, h0, L):
    Bz, T, H, P = xs.shape
    G, N = bm.shape[2], bm.shape[3]
    R = H // G
    C = T // L
    xc = xs.reshape(Bz, C, L, G, R, P)
    dtc = dt.reshape(Bz, C, L, G, R)
    bc = bm.reshape(Bz, C, L, G, N)
    cc = cm.reshape(Bz, C, L, G, N)
    cs = jnp.cumsum(dtc * a.reshape(G, R), axis=2)
    incl = jnp.tril(jnp.ones((L, L), dtype=bool))[:, :, None, None]
    seg = cs[:, :, :, None] - cs[:, :, None, :]
    decay = jnp.where(incl, jnp.exp(jnp.where(incl, seg, 0.0)), 0.0)
    xdt = xc * dtc[..., None]
    cb = jnp.einsum('bclgn,bcsgn->bclsg', cc, bc)
    y_diag = jnp.einsum('bclsg,bclsgr,bcsgrp->bclgrp', cb, decay, xdt)
    cs_last = cs[:, :, -1]
    chunk_states = jnp.einsum('bcsgn,bcsgr,bcsgrp->bcgrpn', bc, jnp.exp(cs_last[:, :, None] - cs), xdt)

    def step(h, inp):
        dec, add = inp
        return h * dec[..., None, None] + add, h

    h_fin, h_prev = lax.scan(step, h0.reshape(Bz, G, R, P, N),
                             (jnp.moveaxis(jnp.exp(cs_last), 1, 0), jnp.moveaxis(chunk_states, 1, 0)))
    h_prev = jnp.moveaxis(h_prev, 0, 1)
    y_off = jnp.einsum('bclgn,bcgrpn,bclgr->bclgrp', cc, h_prev, jnp.exp(cs))
    y = (y_diag + y_off).reshape(Bz, T, H, P)
    return y, h_fin.reshape(Bz, H, P, N)


def gdn_scan(q, k, v, g, beta, h0, L):
    Bz, T, H, K = q.shape
    V = v.shape[-1]
    C = T // L
    to_blk = lambda t: jnp.swapaxes(t.reshape(Bz, C, L, H, t.shape[-1]), 2, 3)
    qc, kc, vc = to_blk(q), to_blk(k), to_blk(v)
    gc = jnp.cumsum(jnp.swapaxes(g.reshape(Bz, C, L, H), 2, 3), axis=-1)
    bc = jnp.swapaxes(beta.reshape(Bz, C, L, H), 2, 3)[..., None]
    incl = jnp.tril(jnp.ones((L, L), dtype=bool))
    strict = jnp.tril(jnp.ones((L, L), dtype=bool), -1)
    gdiff = gc[..., :, None] - gc[..., None, :]
    dec = jnp.where(incl, jnp.exp(jnp.where(incl, gdiff, 0.0)), 0.0)
    kb = kc * bc
    a_mat = jnp.where(strict, jnp.einsum('bchlk,bchsk->bchls', kb, kc) * dec, 0.0) + jnp.eye(L, dtype=q.dtype)
    u = lax.linalg.triangular_solve(a_mat, vc * bc, left_side=True, lower=True, unit_diagonal=True)
    w = lax.linalg.triangular_solve(a_mat, kb * jnp.exp(gc)[..., None], left_side=True, lower=True, unit_diagonal=True)
    qk = jnp.einsum('bchlk,bchsk->bchls', qc, kc) * dec
    q_g = qc * jnp.exp(gc)[..., None]
    g_last = gc[..., -1]
    k_g = kc * jnp.exp(g_last[..., None] - gc)[..., None]

    def step(S, inp):
        u_c, w_c, qg_c, qk_c, kg_c, gl_c = inp
        v_new = u_c - jnp.einsum('bhlk,bhkv->bhlv', w_c, S)
        o = jnp.einsum('bhlk,bhkv->bhlv', qg_c, S) + jnp.einsum('bhls,bhsv->bhlv', qk_c, v_new)
        S = S * jnp.exp(gl_c)[..., None, None] + jnp.einsum('bhlk,bhlv->bhkv', kg_c, v_new)
        return S, o

    mv = lambda t: jnp.moveaxis(t, 1, 0)
    S_fin, o = lax.scan(step, h0, (mv(u), mv(w), mv(q_g), mv(qk), mv(k_g), mv(g_last)))
    o = jnp.swapaxes(jnp.moveaxis(o, 0, 1), 2, 3).reshape(Bz, T, H, V)
    return o, S_fin


def ssd_layer(h, conv_buf, ssm_state, with_meta, norm_w, w_in, conv_w, conv_b, dt_bias, a_log, d_skip, gnorm_w, w_out):
    Bz, T, _ = h.shape
    proj = rmsnorm(h, norm_w) @ w_in
    z = proj[..., :SSD_INNER]
    xbc = proj[..., SSD_INNER:SSD_INNER + SSD_CONV_DIM]
    dt_raw = proj[..., SSD_INNER + SSD_CONV_DIM:]
    xbc, new_buf = causal_conv(xbc, conv_buf, conv_w, conv_b)
    xbc = jax.nn.silu(xbc).astype(jnp.float32)
    xs = xbc[..., :SSD_INNER].reshape(Bz, T, SSD_HEADS, SSD_HEADDIM)
    bm = xbc[..., SSD_INNER:SSD_INNER + SSD_GROUPS * SSD_STATE].reshape(Bz, T, SSD_GROUPS, SSD_STATE)
    cm = xbc[..., SSD_INNER + SSD_GROUPS * SSD_STATE:].reshape(Bz, T, SSD_GROUPS, SSD_STATE)
    dt = jax.nn.softplus(dt_raw.astype(jnp.float32) + dt_bias.astype(jnp.float32))
    a = -jnp.exp(a_log.astype(jnp.float32))
    st = ssm_state.astype(jnp.float32)
    ys = []
    for (s, e, L) in _segments(T, with_meta):
        y_s, st = ssd_scan(xs[:, s:e], dt[:, s:e], a, bm[:, s:e], cm[:, s:e], st, L)
        ys.append(y_s)
    y = jnp.concatenate(ys, axis=1) + xs * d_skip.astype(jnp.float32)[:, None]
    yg = y.reshape(Bz, T, SSD_INNER) * jax.nn.silu(z.astype(jnp.float32))
    yg = yg.reshape(Bz, T, SSD_GROUPS, SSD_INNER // SSD_GROUPS)
    yg = yg * lax.rsqrt(jnp.mean(yg * yg, axis=-1, keepdims=True) + EPS)
    yg = (yg.reshape(Bz, T, SSD_INNER) * gnorm_w.astype(jnp.float32)).astype(h.dtype)
    return h + yg @ w_out, new_buf, st


def _l2norm(t):
    return t * lax.rsqrt(jnp.sum(t * t, axis=-1, keepdims=True) + EPS)


def gdn_layer(h, conv_buf, S0, with_meta, norm_w, w_in, conv_w, dt_bias, a_log, onorm_w, w_out):
    Bz, T, _ = h.shape
    proj = rmsnorm(h, norm_w) @ w_in
    qkv = proj[..., :GDN_CONV_DIM]
    z = proj[..., GDN_CONV_DIM:GDN_CONV_DIM + GDN_VAL_DIM]
    a_raw = proj[..., GDN_CONV_DIM + GDN_VAL_DIM:GDN_CONV_DIM + GDN_VAL_DIM + GDN_HEADS]
    b_raw = proj[..., GDN_CONV_DIM + GDN_VAL_DIM + GDN_HEADS:]
    qkv, new_buf = causal_conv(qkv, conv_buf, conv_w, None)
    qkv = jax.nn.silu(qkv).astype(jnp.float32)
    q = _l2norm(qkv[..., :GDN_KEY_DIM].reshape(Bz, T, GDN_HEADS, GDN_DK)) * (GDN_DK ** -0.5)
    k = _l2norm(qkv[..., GDN_KEY_DIM:2 * GDN_KEY_DIM].reshape(Bz, T, GDN_HEADS, GDN_DK))
    v = qkv[..., 2 * GDN_KEY_DIM:].reshape(Bz, T, GDN_HEADS, GDN_DV)
    g = -jnp.exp(a_log.astype(jnp.float32)) * jax.nn.softplus(a_raw.astype(jnp.float32) + dt_bias.astype(jnp.float32))
    beta = jax.nn.sigmoid(b_raw.astype(jnp.float32))
    st = S0.astype(jnp.float32)
    os_ = []
    for (s, e, L) in _segments(T, with_meta):
        o_s, st = gdn_scan(q[:, s:e], k[:, s:e], v[:, s:e], g[:, s:e], beta[:, s:e], st, L)
        os_.append(o_s)
    o = jnp.concatenate(os_, axis=1)
    o = o * lax.rsqrt(jnp.mean(o * o, axis=-1, keepdims=True) + EPS) * onorm_w.astype(jnp.float32)
    o = o * jax.nn.silu(z.astype(jnp.float32)).reshape(Bz, T, GDN_HEADS, GDN_DV)
    o = o.reshape(Bz, T, GDN_VAL_DIM).astype(h.dtype)
    return h + o @ w_out, new_buf, st


def setup_inputs(seed: int = 0) -> dict:
    key = jax.random.key(seed)
    ks = jax.random.split(key, 32)

    def nrm(k, shape, scale):
        return jax.random.normal(k, shape, jnp.float32) * scale

    def dt_bias_init(k, shape):
        u = jax.random.uniform(k, shape, jnp.float32)
        dt = jnp.exp(math.log(1e-3) + u * (math.log(1e-1) - math.log(1e-3)))
        return dt + jnp.log(-jnp.expm1(-dt))

    return {
        'x_prompt': nrm(ks[0], (BATCH, SEQ, D_MODEL), 1.0),
        'x_sample': nrm(ks[1], (DEC_BATCH, DEC_SEQ, D_MODEL), 1.0),
        'state_ssd': nrm(ks[2], (N_SSD, DEC_BATCH, SSD_HEADS, SSD_HEADDIM, SSD_STATE), 0.1),
        'state_ssd_conv': nrm(ks[3], (N_SSD, DEC_BATCH, CONV_K - 1, SSD_CONV_DIM), 1.0),
        'state_gdn': nrm(ks[4], (N_GDN, DEC_BATCH, GDN_HEADS, GDN_DK, GDN_DV), 0.1),
        'state_gdn_conv': nrm(ks[5], (N_GDN, DEC_BATCH, CONV_K - 1, GDN_CONV_DIM), 1.0),
        'meta_tokens': nrm(ks[6], (N_META, D_MODEL), 1.0),
        'ssd_norm_w': 1.0 + nrm(ks[7], (N_SSD, D_MODEL), 0.02),
        'ssd_w_in': nrm(ks[8], (N_SSD, D_MODEL, SSD_IN_DIM), D_MODEL ** -0.5),
        'ssd_conv_w': nrm(ks[9], (N_SSD, CONV_K, SSD_CONV_DIM), CONV_K ** -0.5),
        'ssd_conv_b': nrm(ks[10], (N_SSD, SSD_CONV_DIM), 0.02),
        'ssd_dt_bias': dt_bias_init(ks[11], (N_SSD, SSD_HEADS)),
        'ssd_a_log': jnp.log(jax.random.uniform(ks[12], (N_SSD, SSD_HEADS), jnp.float32, 1.0, 16.0)),
        'ssd_d': 1.0 + nrm(ks[13], (N_SSD, SSD_HEADS), 0.02),
        'ssd_gnorm_w': 1.0 + nrm(ks[14], (N_SSD, SSD_INNER), 0.02),
        'ssd_w_out': nrm(ks[15], (N_SSD, SSD_INNER, D_MODEL), SSD_INNER ** -0.5),
        'gdn_norm_w': 1.0 + nrm(ks[16], (N_GDN, D_MODEL), 0.02),
        'gdn_w_in': nrm(ks[17], (N_GDN, D_MODEL, GDN_IN_DIM), D_MODEL ** -0.5),
        'gdn_conv_w': nrm(ks[18], (N_GDN, CONV_K, GDN_CONV_DIM), CONV_K ** -0.5),
        'gdn_dt_bias': dt_bias_init(ks[19], (N_GDN, GDN_HEADS)),
        'gdn_a_log': jnp.log(jax.random.uniform(ks[20], (N_GDN, GDN_HEADS), jnp.float32, 1.0, 16.0)),
        'gdn_onorm_w': 1.0 + nrm(ks[21], (N_GDN, GDN_DV), 0.02),
        'gdn_w_out': nrm(ks[22], (N_GDN, GDN_VAL_DIM, D_MODEL), GDN_VAL_DIM ** -0.5),
        'final_norm_w': 1.0 + nrm(ks[23], (D_MODEL,), 0.02),
    }


def reference(x_prompt, x_sample, state_ssd, state_ssd_conv, state_gdn, state_gdn_conv, meta_tokens,
              ssd_norm_w, ssd_w_in, ssd_conv_w, ssd_conv_b, ssd_dt_bias, ssd_a_log, ssd_d, ssd_gnorm_w, ssd_w_out,
              gdn_norm_w, gdn_w_in, gdn_conv_w, gdn_dt_bias, gdn_a_log, gdn_onorm_w, gdn_w_out, final_norm_w):
    Bp = x_prompt.shape[0]
    meta = jnp.broadcast_to(meta_tokens.astype(x_prompt.dtype)[None], (Bp, N_META, D_MODEL))
    hp = jnp.concatenate([meta, x_prompt], axis=1)
    hs = x_sample
    p_ssd, p_ssd_conv, p_gdn, p_gdn_conv = [], [], [], []
    s_ssd, s_ssd_conv, s_gdn, s_gdn_conv = [], [], [], []
    for i in range(DEPTH):
        j = i // 2
        if i % 2 == 0:
            w = (ssd_norm_w[j], ssd_w_in[j], ssd_conv_w[j], ssd_conv_b[j], ssd_dt_bias[j], ssd_a_log[j],
                 ssd_d[j], ssd_gnorm_w[j], ssd_w_out[j])
            zb = jnp.zeros((Bp, CONV_K - 1, SSD_CONV_DIM), hp.dtype)
            zs = jnp.zeros((Bp, SSD_HEADS, SSD_HEADDIM, SSD_STATE), jnp.float32)
            hp, cb, st = ssd_layer(hp, zb, zs, True, *w)
            p_ssd.append(st.astype(state_ssd.dtype))
            p_ssd_conv.append(cb.astype(state_ssd_conv.dtype))
            hs, cb, st = ssd_layer(hs, state_ssd_conv[j], state_ssd[j], False, *w)
            s_ssd.append(st.astype(state_ssd.dtype))
            s_ssd_conv.append(cb.astype(state_ssd_conv.dtype))
        else:
            w = (gdn_norm_w[j], gdn_w_in[j], gdn_conv_w[j], gdn_dt_bias[j], gdn_a_log[j], gdn_onorm_w[j], gdn_w_out[j])
            zb = jnp.zeros((Bp, CONV_K - 1, GDN_CONV_DIM), hp.dtype)
            zs = jnp.zeros((Bp, GDN_HEADS, GDN_DK, GDN_DV), jnp.float32)
            hp, cb, st = gdn_layer(hp, zb, zs, True, *w)
            p_gdn.append(st.astype(state_gdn.dtype))
            p_gdn_conv.append(cb.astype(state_gdn_conv.dtype))
            hs, cb, st = gdn_layer(hs, state_gdn_conv[j], state_gdn[j], False, *w)
            s_gdn.append(st.astype(state_gdn.dtype))
            s_gdn_conv.append(cb.astype(state_gdn_conv.dtype))
    y_prompt = rmsnorm(hp, final_norm_w)[:, N_META:]
    y_sample = rmsnorm(hs, final_norm_w)
    return (y_prompt, y_sample,
            jnp.stack(p_ssd), jnp.stack(p_ssd_conv), jnp.stack(p_gdn), jnp.stack(p_gdn_conv),
            jnp.stack(s_ssd), jnp.stack(s_ssd_conv), jnp.stack(s_gdn), jnp.stack(s_gdn_conv))
```

```python
import functools

import jax
import jax.numpy as jnp
from jax import lax
from jax.experimental import pallas as pl
from jax.experimental.pallas import tpu as pltpu

F32 = jnp.float32
BF16 = jnp.bfloat16
HIGHEST = lax.Precision.HIGHEST

D_MODEL = 1024
N_META = 16
CONV_K = 4
EPS = 1e-6
CHUNK = 64
TAIL = 8
LANE = 128
SSD_INNER = 2048
SSD_P = 64
SSD_H = 32
SSD_G = 4
SSD_N = 128
SSD_GW = SSD_INNER // SSD_G
SSD_CONV = SSD_INNER + 2 * SSD_G * SSD_N
GDN_DK = 128
GDN_DV = 256
GDN_H = 8
GDN_KEY = GDN_H * GDN_DK
GDN_VAL = GDN_H * GDN_DV
GDN_CONV = 2 * GDN_KEY + GDN_VAL
REP = 32
NEG_BIG = -1e30

VMEM_LIMIT = 48 * 1024 * 1024


def _sigmoid(x):
    return 1.0 / (1.0 + jnp.exp(-x))


def _silu(x):
    return x * _sigmoid(x)


def _softplus(x):
    return jnp.maximum(x, 0.0) + jnp.log1p(jnp.exp(-jnp.abs(x)))


def _dot(a, b, precision=None):
    return jnp.dot(a, b, preferred_element_type=F32, precision=precision)


def _dot_nt(a, b, precision=None):
    return lax.dot_general(a, b, (((1,), (1,)), ((), ())), preferred_element_type=F32, precision=precision)


def _dot_tn(a, b):
    return lax.dot_general(a, b, (((0,), (0,)), ((), ())), preferred_element_type=F32)


def _iota(shape, dim):
    return lax.broadcasted_iota(jnp.int32, shape, dim)


def _split3_merge(x):
    hi = x.astype(BF16)
    r1 = x - hi.astype(F32)
    mid = r1.astype(BF16)
    lo = (r1 - mid.astype(F32)).astype(BF16)
    lane = _iota(x.shape, 1)
    zero = jnp.zeros_like(hi)
    return jnp.where(lane < REP, hi, jnp.where(lane < 2 * REP, mid, jnp.where(lane < 3 * REP, lo, zero)))


def _causal_conv(xp_scr, x_new, w_ref, n_rows):
    xp_scr[TAIL:TAIL + n_rows, :] = x_new
    acc = xp_scr[pl.ds(TAIL - 3, n_rows), :] * w_ref[0:1, :]
    for k in range(1, CONV_K):
        acc = acc + xp_scr[pl.ds(TAIL - 3 + k, n_rows), :] * w_ref[k:k + 1, :]
    xp_scr[0:TAIL, :] = xp_scr[n_rows:n_rows + TAIL, :]
    return acc


def _inproj_kernel(splits, x_ref, nw_ref, w_ref, *out_refs):
    x = x_ref[...]
    ms = jnp.mean(x * x, axis=-1, keepdims=True)
    xn = (x * lax.rsqrt(ms + EPS) * nw_ref[...]).astype(BF16)
    for (a, b), o_ref in zip(splits, out_refs):
        o_ref[...] = _dot(xn, w_ref[:, a:b])


def _inproj(x2d, norm_w, w_bf16, widths, tm):
    m = x2d.shape[0]
    n_tot = w_bf16.shape[1]
    splits, a = [], 0
    for wd in widths:
        splits.append((a, a + wd))
        a += wd
    assert a == n_tot and m % tm == 0
    return pl.pallas_call(
        functools.partial(_inproj_kernel, tuple(splits)),
        grid=(m // tm,),
        in_specs=[
            pl.BlockSpec((tm, D_MODEL), lambda i: (i, 0)),
            pl.BlockSpec((1, D_MODEL), lambda i: (0, 0)),
            pl.BlockSpec((D_MODEL, n_tot), lambda i: (0, 0), pipeline_mode=pl.Buffered(1)),
        ],
        out_specs=[pl.BlockSpec((tm, wd), lambda i: (i, 0)) for wd in widths],
        out_shape=[jax.ShapeDtypeStruct((m, wd), F32) for wd in widths],
        compiler_params=pltpu.CompilerParams(
            dimension_semantics=("parallel",), vmem_limit_bytes=VMEM_LIMIT),
        name="inproj",
    )(x2d, norm_w.reshape(1, D_MODEL), w_bf16)


def _outproj_kernel(final, y_ref, w_ref, h_ref, *rest):
    if final:
        fw_ref, o_ref = rest
    else:
        (o_ref,) = rest
    h = h_ref[...] + _dot(y_ref[...], w_ref[...])
    if final:
        ms = jnp.mean(h * h, axis=-1, keepdims=True)
        h = h * lax.rsqrt(ms + EPS) * fw_ref[...]
    o_ref[...] = h


def _outproj(y2d, w_bf16, h2d, tm, final_w=None):
    m, k = y2d.shape
    assert m % tm == 0
    final = final_w is not None
    in_specs = [
        pl.BlockSpec((tm, k), lambda i: (i, 0)),
        pl.BlockSpec((k, D_MODEL), lambda i: (0, 0), pipeline_mode=pl.Buffered(1)),
        pl.BlockSpec((tm, D_MODEL), lambda i: (i, 0)),
    ]
    args = [y2d, w_bf16, h2d]
    if final:
        in_specs.append(pl.BlockSpec((1, D_MODEL), lambda i: (0, 0)))
        args.append(final_w.reshape(1, D_MODEL))
    return pl.pallas_call(
        functools.partial(_outproj_kernel, final),
        grid=(m // tm,),
        in_specs=in_specs,
        out_specs=pl.BlockSpec((tm, D_MODEL), lambda i: (i, 0)),
        out_shape=jax.ShapeDtypeStruct((m, D_MODEL), F32),
        compiler_params=pltpu.CompilerParams(
            dimension_semantics=("parallel",), vmem_limit_bytes=VMEM_LIMIT),
        name="outproj_final" if final else "outproj",
    )(*args)


def _ssd_kernel(n_valid, n_chunks,
                z_ref, xbc_ref, dt_ref, tail_ref, s0_ref, cw_ref, cb_ref, dtb_ref, alog_ref, dsk_ref,
                gw_ref, e3_ref, tri_ref, sel_ref,
                y_ref, sfin_ref, xp_scr, st_scr):
    L = CHUNK
    c = pl.program_id(1)

    @pl.when(c == 0)
    def _():
        xp_scr[0:TAIL, :] = tail_ref[0]
        st_scr[...] = s0_ref[0]

    u = _silu(_causal_conv(xp_scr, xbc_ref[0], cw_ref, L) + cb_ref[...])
    xs = u[:, :SSD_INNER]
    bm = u[:, SSD_INNER:SSD_INNER + SSD_G * SSD_N]
    cm = u[:, SSD_INNER + SSD_G * SSD_N:]

    dt4 = _softplus(dt_ref[0] + dtb_ref[...])
    if n_valid < L:
        dt4 = jnp.where(_iota(dt4.shape, 0) < n_valid, dt4, 0.0)
    da4 = dt4 * (-jnp.exp(alog_ref[...]))
    cs4 = _dot(tri_ref[...], da4, precision=HIGHEST)
    cs_last = cs4[L - 1:L, :]
    stack = jnp.concatenate([cs4, dt4, jnp.exp(cs_last - cs4), jnp.exp(cs4)], axis=0)
    ex = _dot(_split3_merge(stack), e3_ref[...])
    cs_x, dt_x, wq_x, ecs_x = ex[0:L], ex[L:2 * L], ex[2 * L:3 * L], ex[3 * L:4 * L]

    lane4 = _iota(cs4.shape, 1)
    r_mat = jnp.concatenate([jnp.where(lane4 < REP, cs4, 0.0),
                             jnp.where((lane4 >= REP) & (lane4 < 2 * REP), cs4, 0.0)], axis=0)
    cs_t2 = _dot_nt(sel_ref[...], r_mat, precision=HIGHEST)
    row_x = jnp.concatenate(
        [jnp.broadcast_to(cs_t2[j:j + 1, :], (L, LANE)) for j in range(SSD_H // 2)], axis=1)

    tok = _iota((L, SSD_INNER), 0)
    src = _iota((L, SSD_INNER), 1) & (L - 1)
    decay = jnp.exp(jnp.where(tok >= src, cs_x - row_x, NEG_BIG))

    cm_b = cm.astype(BF16)
    bm_b = bm.astype(BF16)
    cb_parts = []
    for g in range(SSD_G):
        cg = cm_b[:, g * SSD_N:(g + 1) * SSD_N]
        bg = bm_b[:, g * SSD_N:(g + 1) * SSD_N]
        cb2 = _dot_nt(cg, jnp.concatenate([bg, bg], axis=0))
        cb_parts.extend([cb2] * (SSD_GW // LANE))
    m_x = (jnp.concatenate(cb_parts, axis=1) * decay).astype(BF16)

    xdt = xs * dt_x
    xdt_b = xdt.astype(BF16)
    blk = 4 * L
    bmask = (_iota((blk, blk), 0) >> 6) == (_iota((blk, blk), 1) >> 6)
    y_parts = []
    for j in range(SSD_INNER // blk):
        xj = xdt_b[:, j * blk:(j + 1) * blk]
        rhs = jnp.where(bmask, jnp.concatenate([xj] * 4, axis=0), jnp.zeros((blk, blk), BF16))
        y_parts.append(_dot(m_x[:, j * blk:(j + 1) * blk], rhs))
    y = jnp.concatenate(y_parts, axis=1)

    xw_b = (xdt * wq_x).astype(BF16)
    e_last = ecs_x[L - 1:L, :]
    off_parts = []
    for g in range(SSD_G):
        sl = slice(g * SSD_GW, (g + 1) * SSD_GW)
        st_g = st_scr[:, sl]
        off_parts.append(_dot(cm_b[:, g * SSD_N:(g + 1) * SSD_N], st_g.astype(BF16)))
        st_scr[:, sl] = st_g * e_last[:, sl] + _dot_tn(bm_b[:, g * SSD_N:(g + 1) * SSD_N], xw_b[:, sl])
    y = y + jnp.concatenate(off_parts, axis=1) * ecs_x + xs * dsk_ref[...]

    yg = y * _silu(z_ref[0])
    for g in range(SSD_G):
        sl = slice(g * SSD_GW, (g + 1) * SSD_GW)
        blk_g = yg[:, sl]
        ms = jnp.mean(blk_g * blk_g, axis=-1, keepdims=True)
        y_ref[0, :, sl] = (blk_g * lax.rsqrt(ms + EPS) * gw_ref[:, sl]).astype(BF16)

    @pl.when(c == n_chunks - 1)
    def _():
        sfin_ref[0] = st_scr[...]


def _ssd_consts():
    k = jnp.arange(LANE)[:, None]
    col = jnp.arange(SSD_INNER)[None, :]
    e3 = ((k % REP == col // SSD_P) & (k < 3 * REP)).astype(BF16)
    t = jnp.arange(CHUNK)
    tri = (t[:, None] >= t[None, :]).astype(F32)
    j = jnp.arange(SSD_H // 2)[:, None]
    kk = jnp.arange(LANE)[None, :]
    sel = ((kk == 2 * j) | (kk == REP + 2 * j + 1)).astype(F32)
    return e3, tri, sel


def _ssd_mixer(z, xbc, dt, tail0, s0, p, n_valid):
    n_seq, t_len, _ = z.shape
    n_chunks = t_len // CHUNK
    shared = tail0.shape[0] == 1
    init_map = (lambda s, c: (0, 0, 0)) if shared else (lambda s, c: (s, 0, 0))
    tok_map = lambda s, c: (s, c, 0)
    const2 = lambda s, c: (0, 0)
    e3, tri, sel = _ssd_consts()
    return pl.pallas_call(
        functools.partial(_ssd_kernel, n_valid, n_chunks),
        grid=(n_seq, n_chunks),
        in_specs=[
            pl.BlockSpec((1, CHUNK, SSD_INNER), tok_map),
            pl.BlockSpec((1, CHUNK, SSD_CONV), tok_map),
            pl.BlockSpec((1, CHUNK, LANE), tok_map),
            pl.BlockSpec((1, TAIL, SSD_CONV), init_map),
            pl.BlockSpec((1, SSD_N, SSD_INNER), init_map),
            pl.BlockSpec((CONV_K, SSD_CONV), const2),
            pl.BlockSpec((1, SSD_CONV), const2),
            pl.BlockSpec((1, LANE), const2),
            pl.BlockSpec((1, LANE), const2),
            pl.BlockSpec((1, SSD_INNER), const2),
            pl.BlockSpec((1, SSD_INNER), const2),
            pl.BlockSpec((LANE, SSD_INNER), const2),
            pl.BlockSpec((CHUNK, CHUNK), const2),
            pl.BlockSpec((SSD_H // 2, LANE), const2),
        ],
        out_specs=[
            pl.BlockSpec((1, CHUNK, SSD_INNER), tok_map),
            pl.BlockSpec((1, SSD_N, SSD_INNER), lambda s, c: (s, 0, 0)),
        ],
        out_shape=[
            jax.ShapeDtypeStruct((n_seq, t_len, SSD_INNER), BF16),
            jax.ShapeDtypeStruct((n_seq, SSD_N, SSD_INNER), F32),
        ],
        scratch_shapes=[
            pltpu.VMEM((CHUNK + TAIL, SSD_CONV), F32),
            pltpu.VMEM((SSD_N, SSD_INNER), F32),
        ],
        compiler_params=pltpu.CompilerParams(
            dimension_semantics=("parallel", "arbitrary"), vmem_limit_bytes=VMEM_LIMIT),
        name="ssd_mixer",
    )(z, xbc, dt, tail0, s0, p["conv_w"], p["conv_b"], p["dt_bias4"], p["a_log4"], p["d_x"], p["gnorm_w"],
      e3, tri, sel)


def _unit_lower_inverse(n_strict, eye):
    L = CHUNK
    r = _iota((L, L), 0)
    cidx = _iota((L, L), 1)
    same16 = (r >> 4) == (cidx >> 4)
    same32 = (r >> 5) == (cidx >> 5)

    def mm(a, b):
        return _dot(a.astype(BF16), b.astype(BF16))

    nd = jnp.where(same16, n_strict, 0.0)
    t = eye - nd
    pw = mm(nd, nd)
    t = t + mm(t, pw)
    pw = mm(pw, pw)
    t = t + mm(t, pw)
    pw = mm(pw, pw)
    t = t + mm(t, pw)
    n1 = jnp.where(same32 & (~same16), n_strict, 0.0)
    t = t - mm(t, mm(n1, t))
    n2 = jnp.where(same32, 0.0, n_strict)
    t = t - mm(t, mm(n2, t))
    return t


def _gdn_kernel(n_valid, n_chunks,
                qkv_ref, z_ref, a_ref, b_ref, tail_ref, s0_ref, cw_ref, dtb_ref, alog_ref, ow_ref,
                e3_ref, tri_ref, sel_ref,
                y_ref, sfin_ref, xp_scr, st_scr):
    L = CHUNK
    c = pl.program_id(1)

    @pl.when(c == 0)
    def _():
        xp_scr[0:TAIL, :] = tail_ref[0]
        st_scr[...] = s0_ref[0]

    u = _silu(_causal_conv(xp_scr, qkv_ref[0], cw_ref, L))

    lane = _iota((1, LANE), 1)
    head_lane = ((lane & (REP - 1)) < GDN_H) & (lane < 3 * REP)
    coef = jnp.where(head_lane, -jnp.exp(alog_ref[...]), 0.0)
    g = coef * _softplus(a_ref[0] + dtb_ref[...])
    beta = _sigmoid(b_ref[0])
    if n_valid < L:
        live = _iota((L, LANE), 0) < n_valid
        g = jnp.where(live, g, 0.0)
        beta = jnp.where(live, beta, 0.0)
    gc = _dot(tri_ref[...], g, precision=HIGHEST)
    g_last = gc[L - 1:L, :]
    stack = jnp.concatenate([gc, beta, jnp.exp(gc), jnp.exp(g_last - gc)], axis=0)
    ex = _dot(_split3_merge(stack), e3_ref[...])
    gc_x, beta_x, egc_x, egl_x = ex[0:L], ex[L:2 * L], ex[2 * L:3 * L], ex[3 * L:4 * L]
    gc_t = _dot_nt(sel_ref[...], gc, precision=HIGHEST)

    r = _iota((L, L), 0)
    cidx = _iota((L, L), 1)
    incl = r >= cidx
    strict = r > cidx
    eye = (r == cidx).astype(F32)

    for h in range(GDN_H):
        ks = slice(h * GDN_DK, (h + 1) * GDN_DK)
        vs = slice(h * GDN_DV, (h + 1) * GDN_DV)
        q_h = u[:, ks]
        k_h = u[:, GDN_KEY + h * GDN_DK:GDN_KEY + (h + 1) * GDN_DK]
        v_h = u[:, 2 * GDN_KEY + h * GDN_DV:2 * GDN_KEY + (h + 1) * GDN_DV]
        q_h = q_h * lax.rsqrt(jnp.sum(q_h * q_h, axis=-1, keepdims=True) + EPS) * (GDN_DK ** -0.5)
        k_h = k_h * lax.rsqrt(jnp.sum(k_h * k_h, axis=-1, keepdims=True) + EPS)
        beta_h = beta_x[:, ks]
        egc_h = egc_x[:, ks]
        kb = k_h * beta_h
        dec = jnp.exp(jnp.where(incl, gc_x[:, h * GDN_DK:h * GDN_DK + L]
                                - jnp.broadcast_to(gc_t[h:h + 1, :], (L, L)), NEG_BIG))
        kq = _dot_nt(jnp.concatenate([kb, q_h], axis=0).astype(BF16), k_h.astype(BF16))
        n_strict = jnp.where(strict, kq[0:L] * dec, 0.0)
        t_inv = _unit_lower_inverse(n_strict, eye)
        rhs = jnp.concatenate([v_h * jnp.concatenate([beta_h, beta_h], axis=1), kb * egc_h], axis=1)
        uw = _dot(t_inv.astype(BF16), rhs.astype(BF16))
        s_h = st_scr[h]
        wq = _dot(jnp.concatenate([uw[:, GDN_DV:], q_h * egc_h], axis=0).astype(BF16), s_h.astype(BF16))
        v_new = uw[:, :GDN_DV] - wq[0:L]
        v_new_b = v_new.astype(BF16)
        o = wq[L:2 * L] + _dot((kq[L:2 * L] * dec).astype(BF16), v_new_b)
        e_last = egc_x[L - 1:L, ks]
        st_scr[h] = (s_h * jnp.concatenate([e_last, e_last], axis=1)
                     + _dot_tn((k_h * egl_x[:, ks]).astype(BF16), v_new_b))
        o = o * lax.rsqrt(jnp.mean(o * o, axis=-1, keepdims=True) + EPS) * ow_ref[...]
        y_ref[0, :, vs] = (o * _silu(z_ref[0, :, vs])).astype(BF16)

    @pl.when(c == n_chunks - 1)
    def _():
        sfin_ref[0] = st_scr[...]


def _gdn_consts():
    k = jnp.arange(LANE)[:, None]
    col = jnp.arange(GDN_KEY)[None, :]
    e3 = ((k % REP == col // GDN_DK) & (k < 3 * REP)).astype(BF16)
    t = jnp.arange(CHUNK)
    tri = (t[:, None] >= t[None, :]).astype(F32)
    sel = (jnp.arange(LANE)[None, :] == jnp.arange(GDN_H)[:, None]).astype(F32)
    return e3, tri, sel


def _gdn_mixer(qkv, z, a, b, tail0, s0, p, n_valid):
    n_seq, t_len, _ = z.shape
    n_chunks = t_len // CHUNK
    shared = tail0.shape[0] == 1
    init3 = (lambda s, c: (0, 0, 0)) if shared else (lambda s, c: (s, 0, 0))
    init4 = (lambda s, c: (0, 0, 0, 0)) if shared else (lambda s, c: (s, 0, 0, 0))
    tok_map = lambda s, c: (s, c, 0)
    const2 = lambda s, c: (0, 0)
    e3, tri, sel = _gdn_consts()
    return pl.pallas_call(
        functools.partial(_gdn_kernel, n_valid, n_chunks),
        grid=(n_seq, n_chunks),
        in_specs=[
            pl.BlockSpec((1, CHUNK, GDN_CONV), tok_map),
            pl.BlockSpec((1, CHUNK, GDN_VAL), tok_map),
            pl.BlockSpec((1, CHUNK, LANE), tok_map),
            pl.BlockSpec((1, CHUNK, LANE), tok_map),
            pl.BlockSpec((1, TAIL, GDN_CONV), init3),
            pl.BlockSpec((1, GDN_H, GDN_DK, GDN_DV), init4),
            pl.BlockSpec((CONV_K, GDN_CONV), const2),
            pl.BlockSpec((1, LANE), const2),
            pl.BlockSpec((1, LANE), const2),
            pl.BlockSpec((1, GDN_DV), const2),
            pl.BlockSpec((LANE, GDN_KEY), const2),
            pl.BlockSpec((CHUNK, CHUNK), const2),
            pl.BlockSpec((GDN_H, LANE), const2),
        ],
        out_specs=[
            pl.BlockSpec((1, CHUNK, GDN_VAL), tok_map),
            pl.BlockSpec((1, GDN_H, GDN_DK, GDN_DV), lambda s, c: (s, 0, 0, 0)),
        ],
        out_shape=[
            jax.ShapeDtypeStruct((n_seq, t_len, GDN_VAL), BF16),
            jax.ShapeDtypeStruct((n_seq, GDN_H, GDN_DK, GDN_DV), F32),
        ],
        scratch_shapes=[
            pltpu.VMEM((CHUNK + TAIL, GDN_CONV), F32),
            pltpu.VMEM((GDN_H, GDN_DK, GDN_DV), F32),
        ],
        compiler_params=pltpu.CompilerParams(
            dimension_semantics=("parallel", "arbitrary"), vmem_limit_bytes=VMEM_LIMIT),
        name="gdn_mixer",
    )(qkv, z, a, b, tail0, s0, p["conv_w"], p["dt_bias"], p["a_log"], p["onorm_w"], e3, tri, sel)


def _rep_lanes(v, n_heads):
    out = jnp.zeros((LANE,), v.dtype)
    for r in range(LANE // REP):
        if n_heads < REP and r == 3:
            break
        out = out.at[r * REP:r * REP + n_heads].set(v)
    return out.reshape(1, LANE)


def _rep_cols(w, n_heads):
    out = jnp.zeros((w.shape[0], LANE), w.dtype)
    for r in range(LANE // REP):
        if n_heads < REP and r == 3:
            break
        out = out.at[:, r * REP:r * REP + n_heads].set(w)
    return out


def _ssd_params(norm_w, w_in, conv_w, conv_b, dt_bias, a_log, d_skip, gnorm_w, w_out):
    w_main = w_in[:, :SSD_INNER + SSD_CONV]
    w_dt = w_in[:, SSD_INNER + SSD_CONV:]
    return dict(
        norm_w=norm_w,
        w_in=jnp.concatenate([w_main, _rep_cols(w_dt, SSD_H)], axis=1).astype(BF16),
        conv_w=conv_w, conv_b=conv_b.reshape(1, SSD_CONV),
        dt_bias4=_rep_lanes(dt_bias, SSD_H), a_log4=_rep_lanes(a_log, SSD_H),
        d_x=jnp.repeat(d_skip, SSD_P).reshape(1, SSD_INNER),
        gnorm_w=gnorm_w.reshape(1, SSD_INNER),
        w_out=w_out.astype(BF16),
    )


def _gdn_params(norm_w, w_in, conv_w, dt_bias, a_log, onorm_w, w_out):
    w_main = w_in[:, :GDN_CONV + GDN_VAL]
    w_a = w_in[:, GDN_CONV + GDN_VAL:GDN_CONV + GDN_VAL + GDN_H]
    w_b = w_in[:, GDN_CONV + GDN_VAL + GDN_H:]
    return dict(
        norm_w=norm_w,
        w_in=jnp.concatenate([w_main, _rep_cols(w_a, GDN_H), _rep_cols(w_b, GDN_H)], axis=1).astype(BF16),
        conv_w=conv_w,
        dt_bias=_rep_lanes(dt_bias, GDN_H), a_log=_rep_lanes(a_log, GDN_H),
        onorm_w=onorm_w.reshape(1, GDN_DV),
        w_out=w_out.astype(BF16),
    )


def _ssd_state_in(s):
    return jnp.swapaxes(s.reshape(s.shape[0], SSD_INNER, SSD_N), 1, 2)


def _ssd_state_out(s):
    return jnp.swapaxes(s, 1, 2).reshape(s.shape[0], SSD_H, SSD_P, SSD_N)


def _tail_from_rows(rows3):
    return jnp.pad(rows3, ((0, 0), (TAIL - (CONV_K - 1), 0), (0, 0)))


def _ssd_layer(h, tail0, s0, p, n_valid, tm_in, tm_out, final_w=None):
    n_seq, t_len, _ = h.shape
    h2 = h.reshape(n_seq * t_len, D_MODEL)
    z, xbc, dt = _inproj(h2, p["norm_w"], p["w_in"], (SSD_INNER, SSD_CONV, LANE), tm_in)
    xbc = xbc.reshape(n_seq, t_len, SSD_CONV)
    y, s_fin = _ssd_mixer(z.reshape(n_seq, t_len, SSD_INNER), xbc, dt.reshape(n_seq, t_len, LANE),
                          tail0, s0, p, n_valid)
    h_new = _outproj(y.reshape(n_seq * t_len, SSD_INNER), p["w_out"], h2, tm_out, final_w)
    return h_new.reshape(n_seq, t_len, D_MODEL), xbc, s_fin


def _gdn_layer(h, tail0, s0, p, n_valid, tm_in, tm_out, final_w=None):
    n_seq, t_len, _ = h.shape
    h2 = h.reshape(n_seq * t_len, D_MODEL)
    qkv, z, a, b = _inproj(h2, p["norm_w"], p["w_in"], (GDN_CONV, GDN_VAL, LANE, LANE), tm_in)
    qkv = qkv.reshape(n_seq, t_len, GDN_CONV)
    y, s_fin = _gdn_mixer(qkv, z.reshape(n_seq, t_len, GDN_VAL), a.reshape(n_seq, t_len, LANE),
                          b.reshape(n_seq, t_len, LANE), tail0, s0, p, n_valid)
    h_new = _outproj(y.reshape(n_seq * t_len, GDN_VAL), p["w_out"], h2, tm_out, final_w)
    return h_new.reshape(n_seq, t_len, D_MODEL), qkv, s_fin


def kernel(x_prompt, x_sample, state_ssd, state_ssd_conv, state_gdn, state_gdn_conv, meta_tokens,
           ssd_norm_w, ssd_w_in, ssd_conv_w, ssd_conv_b, ssd_dt_bias, ssd_a_log, ssd_d, ssd_gnorm_w, ssd_w_out,
           gdn_norm_w, gdn_w_in, gdn_conv_w, gdn_dt_bias, gdn_a_log, gdn_onorm_w, gdn_w_out, final_norm_w):
    depth = ssd_norm_w.shape[0] + gdn_norm_w.shape[0]
    n_dec, dec_t, _ = x_sample.shape
    assert dec_t == N_META and N_META <= CHUNK and x_prompt.shape[1] % CHUNK == 0
    hs = jnp.concatenate([x_sample, meta_tokens.astype(x_sample.dtype)[None]], axis=0)
    hs = jnp.pad(hs, ((0, 0), (0, CHUNK - N_META), (0, 0)))
    hp = x_prompt
    n_small = n_dec + 1
    tm_small = n_small * CHUNK
    outs = {k: [] for k in ("p_ssd", "p_ssd_conv", "p_gdn", "p_gdn_conv", "s_ssd", "s_ssd_conv", "s_gdn", "s_gdn_conv")}
    for i in range(depth):
        j = i // 2
        fw = final_norm_w if i == depth - 1 else None
        if i % 2 == 0:
            p = _ssd_params(ssd_norm_w[j], ssd_w_in[j], ssd_conv_w[j], ssd_conv_b[j], ssd_dt_bias[j],
                            ssd_a_log[j], ssd_d[j], ssd_gnorm_w[j], ssd_w_out[j])
            tail_s = _tail_from_rows(jnp.concatenate(
                [state_ssd_conv[j], jnp.zeros((1, CONV_K - 1, SSD_CONV), F32)], axis=0))
            s0_s = _ssd_state_in(jnp.concatenate(
                [state_ssd[j], jnp.zeros((1,) + state_ssd.shape[2:], F32)], axis=0))
            hs, xbc_s, sfin_s = _ssd_layer(hs, tail_s, s0_s, p, N_META, tm_small, tm_small, fw)
            outs["s_ssd"].append(_ssd_state_out(sfin_s[:n_dec]))
            outs["s_ssd_conv"].append(xbc_s[:n_dec, N_META - (CONV_K - 1):N_META])
            hp, xbc_p, sfin_p = _ssd_layer(hp, xbc_s[n_dec:, N_META - TAIL:N_META], sfin_s[n_dec:], p,
                                           CHUNK, 256, 512, fw)
            outs["p_ssd"].append(_ssd_state_out(sfin_p))
            outs["p_ssd_conv"].append(xbc_p[:, -(CONV_K - 1):])
        else:
            p = _gdn_params(gdn_norm_w[j], gdn_w_in[j], gdn_conv_w[j], gdn_dt_bias[j], gdn_a_log[j],
                            gdn_onorm_w[j], gdn_w_out[j])
            tail_s = _tail_from_rows(jnp.concatenate(
                [state_gdn_conv[j], jnp.zeros((1, CONV_K - 1, GDN_CONV), F32)], axis=0))
            s0_s = jnp.concatenate([state_gdn[j], jnp.zeros((1,) + state_gdn.shape[2:], F32)], axis=0)
            hs, qkv_s, sfin_s = _gdn_layer(hs, tail_s, s0_s, p, N_META, tm_small, tm_small, fw)
            outs["s_gdn"].append(sfin_s[:n_dec])
            outs["s_gdn_conv"].append(qkv_s[:n_dec, N_META - (CONV_K - 1):N_META])
            hp, qkv_p, sfin_p = _gdn_layer(hp, qkv_s[n_dec:, N_META - TAIL:N_META], sfin_s[n_dec:], p,
                                           CHUNK, 256, 512, fw)
            outs["p_gdn"].append(sfin_p)
            outs["p_gdn_conv"].append(qkv_p[:, -(CONV_K - 1):])
    y_prompt = hp
    y_sample = hs[:n_dec, :N_META]
    st = lambda k: jnp.stack(outs[k])
    return (y_prompt, y_sample, st("p_ssd"), st("p_ssd_conv"), st("p_gdn"), st("p_gdn_conv"),
            st("s_ssd"), st("s_ssd_conv"), st("s_gdn"), st("s_gdn_conv"))
```

```python
import functools

import jax
import jax.numpy as jnp
from jax import lax
from jax.experimental import pallas as pl
from jax.experimental.pallas import tpu as pltpu

F32 = jnp.float32
BF16 = jnp.bfloat16
HIGHEST = lax.Precision.HIGHEST

D_MODEL = 1024
N_META = 16
CONV_K = 4
EPS = 1e-6
CHUNK = 64
TAIL = 8
LANE = 128
SSD_INNER = 2048
SSD_P = 64
SSD_H = 32
SSD_G = 4
SSD_N = 128
SSD_GW = SSD_INNER // SSD_G
SSD_CONV = SSD_INNER + 2 * SSD_G * SSD_N
GDN_DK = 128
GDN_DV = 256
GDN_H = 8
GDN_KEY = GDN_H * GDN_DK
GDN_VAL = GDN_H * GDN_DV
GDN_CONV = 2 * GDN_KEY + GDN_VAL
REP = 32
NEG_BIG = -1e30

VMEM_LIMIT = 48 * 1024 * 1024


def _sigmoid(x):
    return 1.0 / (1.0 + jnp.exp(-x))


def _silu(x):
    return x * _sigmoid(x)


def _softplus(x):
    return jnp.maximum(x, 0.0) + jnp.log1p(jnp.exp(-jnp.abs(x)))


def _dot(a, b, precision=None):
    return jnp.dot(a, b, preferred_element_type=F32, precision=precision)


def _dot_nt(a, b, precision=None):
    return lax.dot_general(a, b, (((1,), (1,)), ((), ())), preferred_element_type=F32, precision=precision)


def _dot_tn(a, b):
    return lax.dot_general(a, b, (((0,), (0,)), ((), ())), preferred_element_type=F32)


def _iota(shape, dim):
    return lax.broadcasted_iota(jnp.int32, shape, dim)


def _block_id(idx, size):
    assert size & (size - 1) == 0
    return idx >> (size.bit_length() - 1)


def _split3_merge(x):
    hi = x.astype(BF16)
    r1 = x - hi.astype(F32)
    mid = r1.astype(BF16)
    lo = (r1 - mid.astype(F32)).astype(BF16)
    lane = _iota(x.shape, 1)
    zero = jnp.zeros_like(hi)
    return jnp.where(lane < REP, hi, jnp.where(lane < 2 * REP, mid, jnp.where(lane < 3 * REP, lo, zero)))


def _causal_conv(xp_scr, x_new, w_ref, n_rows):
    xp_scr[TAIL:TAIL + n_rows, :] = x_new
    acc = xp_scr[pl.ds(TAIL - 3, n_rows), :] * w_ref[0:1, :]
    for k in range(1, CONV_K):
        acc = acc + xp_scr[pl.ds(TAIL - 3 + k, n_rows), :] * w_ref[k:k + 1, :]
    xp_scr[0:TAIL, :] = xp_scr[n_rows:n_rows + TAIL, :]
    return acc


def _inproj_kernel(splits, x_ref, nw_ref, w_ref, *out_refs):
    x = x_ref[...]
    ms = jnp.mean(x * x, axis=-1, keepdims=True)
    xn = (x * lax.rsqrt(ms + EPS) * nw_ref[...]).astype(BF16)
    for (a, b), o_ref in zip(splits, out_refs):
        o_ref[...] = _dot(xn, w_ref[:, a:b])


def _inproj(x2d, norm_w, w_bf16, widths, tm):
    m = x2d.shape[0]
    n_tot = w_bf16.shape[1]
    splits, a = [], 0
    for wd in widths:
        splits.append((a, a + wd))
        a += wd
    assert a == n_tot and m % tm == 0
    return pl.pallas_call(
        functools.partial(_inproj_kernel, tuple(splits)),
        grid=(m // tm,),
        in_specs=[
            pl.BlockSpec((tm, D_MODEL), lambda i: (i, 0)),
            pl.BlockSpec((1, D_MODEL), lambda i: (0, 0)),
            pl.BlockSpec((D_MODEL, n_tot), lambda i: (0, 0), pipeline_mode=pl.Buffered(1)),
        ],
        out_specs=[pl.BlockSpec((tm, wd), lambda i: (i, 0)) for wd in widths],
        out_shape=[jax.ShapeDtypeStruct((m, wd), F32) for wd in widths],
        compiler_params=pltpu.CompilerParams(
            dimension_semantics=("parallel",), vmem_limit_bytes=VMEM_LIMIT),
        name="inproj",
    )(x2d, norm_w.reshape(1, D_MODEL), w_bf16)


def _outproj_kernel(final, y_ref, w_ref, h_ref, *rest):
    if final:
        fw_ref, o_ref = rest
    else:
        (o_ref,) = rest
    h = h_ref[...] + _dot(y_ref[...], w_ref[...])
    if final:
        ms = jnp.mean(h * h, axis=-1, keepdims=True)
        h = h * lax.rsqrt(ms + EPS) * fw_ref[...]
    o_ref[...] = h


def _outproj(y2d, w_bf16, h2d, tm, final_w=None):
    m, k = y2d.shape
    assert m % tm == 0
    final = final_w is not None
    in_specs = [
        pl.BlockSpec((tm, k), lambda i: (i, 0)),
        pl.BlockSpec((k, D_MODEL), lambda i: (0, 0), pipeline_mode=pl.Buffered(1)),
        pl.BlockSpec((tm, D_MODEL), lambda i: (i, 0)),
    ]
    args = [y2d, w_bf16, h2d]
    if final:
        in_specs.append(pl.BlockSpec((1, D_MODEL), lambda i: (0, 0)))
        args.append(final_w.reshape(1, D_MODEL))
    return pl.pallas_call(
        functools.partial(_outproj_kernel, final),
        grid=(m // tm,),
        in_specs=in_specs,
        out_specs=pl.BlockSpec((tm, D_MODEL), lambda i: (i, 0)),
        out_shape=jax.ShapeDtypeStruct((m, D_MODEL), F32),
        compiler_params=pltpu.CompilerParams(
            dimension_semantics=("parallel",), vmem_limit_bytes=VMEM_LIMIT),
        name="outproj_final" if final else "outproj",
    )(*args)


def _ssd_kernel(n_valid, n_chunks,
                z_ref, xbc_ref, dt_ref, tail_ref, s0_ref, cw_ref, cb_ref, dtb_ref, alog_ref, dsk_ref,
                gw_ref, e3_ref, tri_ref, sel_ref,
                y_ref, sfin_ref, xp_scr, st_scr):
    L = CHUNK
    c = pl.program_id(1)

    @pl.when(c == 0)
    def _():
        xp_scr[0:TAIL, :] = tail_ref[0]
        st_scr[...] = s0_ref[0]

    u = _silu(_causal_conv(xp_scr, xbc_ref[0], cw_ref, L) + cb_ref[...])
    xs = u[:, :SSD_INNER]
    bm = u[:, SSD_INNER:SSD_INNER + SSD_G * SSD_N]
    cm = u[:, SSD_INNER + SSD_G * SSD_N:]

    dt4 = _softplus(dt_ref[0] + dtb_ref[...])
    if n_valid < L:
        dt4 = jnp.where(_iota(dt4.shape, 0) < n_valid, dt4, 0.0)
    da4 = dt4 * (-jnp.exp(alog_ref[...]))
    cs4 = _dot(tri_ref[...], da4, precision=HIGHEST)
    cs_last = cs4[L - 1:L, :]
    stack = jnp.concatenate([cs4, dt4, jnp.exp(cs_last - cs4), jnp.exp(cs4)], axis=0)
    ex = _dot(_split3_merge(stack), e3_ref[...])
    cs_x, dt_x, wq_x, ecs_x = ex[0:L], ex[L:2 * L], ex[2 * L:3 * L], ex[3 * L:4 * L]

    lane4 = _iota(cs4.shape, 1)
    r_mat = jnp.concatenate([jnp.where(lane4 < REP, cs4, 0.0),
                             jnp.where((lane4 >= REP) & (lane4 < 2 * REP), cs4, 0.0)], axis=0)
    cs_t2 = _dot_nt(sel_ref[...], r_mat, precision=HIGHEST)
    row_x = jnp.concatenate(
        [jnp.broadcast_to(cs_t2[j:j + 1, :], (L, LANE)) for j in range(SSD_H // 2)], axis=1)

    tok = _iota((L, SSD_INNER), 0)
    src = _iota((L, SSD_INNER), 1) & (L - 1)
    decay = jnp.exp(jnp.where(tok >= src, cs_x - row_x, NEG_BIG))

    cm_b = cm.astype(BF16)
    bm_b = bm.astype(BF16)
    cb_parts = []
    for g in range(SSD_G):
        cg = cm_b[:, g * SSD_N:(g + 1) * SSD_N]
        bg = bm_b[:, g * SSD_N:(g + 1) * SSD_N]
        cb2 = _dot_nt(cg, jnp.concatenate([bg, bg], axis=0))
        cb_parts.extend([cb2] * (SSD_GW // LANE))
    m_x = (jnp.concatenate(cb_parts, axis=1) * decay).astype(BF16)

    xdt = xs * dt_x
    xdt_b = xdt.astype(BF16)
    blk = 4 * L
    bmask = _block_id(_iota((blk, blk), 0), L) == _block_id(_iota((blk, blk), 1), L)
    y_parts = []
    for j in range(SSD_INNER // blk):
        xj = xdt_b[:, j * blk:(j + 1) * blk]
        rhs = jnp.where(bmask, jnp.concatenate([xj] * 4, axis=0), jnp.zeros((blk, blk), BF16))
        y_parts.append(_dot(m_x[:, j * blk:(j + 1) * blk], rhs))
    y = jnp.concatenate(y_parts, axis=1)

    xw_b = (xdt * wq_x).astype(BF16)
    e_last = ecs_x[L - 1:L, :]
    off_parts = []
    for g in range(SSD_G):
        sl = slice(g * SSD_GW, (g + 1) * SSD_GW)
        st_g = st_scr[:, sl]
        off_parts.append(_dot(cm_b[:, g * SSD_N:(g + 1) * SSD_N], st_g.astype(BF16)))
        st_scr[:, sl] = st_g * e_last[:, sl] + _dot_tn(bm_b[:, g * SSD_N:(g + 1) * SSD_N], xw_b[:, sl])
    y = y + jnp.concatenate(off_parts, axis=1) * ecs_x + xs * dsk_ref[...]

    yg = y * _silu(z_ref[0])
    for g in range(SSD_G):
        sl = slice(g * SSD_GW, (g + 1) * SSD_GW)
        blk_g = yg[:, sl]
        ms = jnp.mean(blk_g * blk_g, axis=-1, keepdims=True)
        y_ref[0, :, sl] = (blk_g * lax.rsqrt(ms + EPS) * gw_ref[:, sl]).astype(BF16)

    @pl.when(c == n_chunks - 1)
    def _():
        sfin_ref[0] = st_scr[...]


def _ssd_consts():
    k = jnp.arange(LANE)[:, None]
    col = jnp.arange(SSD_INNER)[None, :]
    e3 = ((k % REP == col // SSD_P) & (k < 3 * REP)).astype(BF16)
    t = jnp.arange(CHUNK)
    tri = (t[:, None] >= t[None, :]).astype(F32)
    j = jnp.arange(SSD_H // 2)[:, None]
    kk = jnp.arange(LANE)[None, :]
    sel = ((kk == 2 * j) | (kk == REP + 2 * j + 1)).astype(F32)
    return e3, tri, sel


def _ssd_mixer(z, xbc, dt, tail0, s0, p, n_valid):
    n_seq, t_len, _ = z.shape
    n_chunks = t_len // CHUNK
    shared = tail0.shape[0] == 1
    init_map = (lambda s, c: (0, 0, 0)) if shared else (lambda s, c: (s, 0, 0))
    tok_map = lambda s, c: (s, c, 0)
    const2 = lambda s, c: (0, 0)
    e3, tri, sel = _ssd_consts()
    return pl.pallas_call(
        functools.partial(_ssd_kernel, n_valid, n_chunks),
        grid=(n_seq, n_chunks),
        in_specs=[
            pl.BlockSpec((1, CHUNK, SSD_INNER), tok_map),
            pl.BlockSpec((1, CHUNK, SSD_CONV), tok_map),
            pl.BlockSpec((1, CHUNK, LANE), tok_map),
            pl.BlockSpec((1, TAIL, SSD_CONV), init_map),
            pl.BlockSpec((1, SSD_N, SSD_INNER), init_map),
            pl.BlockSpec((CONV_K, SSD_CONV), const2),
            pl.BlockSpec((1, SSD_CONV), const2),
            pl.BlockSpec((1, LANE), const2),
            pl.BlockSpec((1, LANE), const2),
            pl.BlockSpec((1, SSD_INNER), const2),
            pl.BlockSpec((1, SSD_INNER), const2),
            pl.BlockSpec((LANE, SSD_INNER), const2),
            pl.BlockSpec((CHUNK, CHUNK), const2),
            pl.BlockSpec((SSD_H // 2, LANE), const2),
        ],
        out_specs=[
            pl.BlockSpec((1, CHUNK, SSD_INNER), tok_map),
            pl.BlockSpec((1, SSD_N, SSD_INNER), lambda s, c: (s, 0, 0)),
        ],
        out_shape=[
            jax.ShapeDtypeStruct((n_seq, t_len, SSD_INNER), BF16),
            jax.ShapeDtypeStruct((n_seq, SSD_N, SSD_INNER), F32),
        ],
        scratch_shapes=[
            pltpu.VMEM((CHUNK + TAIL, SSD_CONV), F32),
            pltpu.VMEM((SSD_N, SSD_INNER), F32),
        ],
        compiler_params=pltpu.CompilerParams(
            dimension_semantics=("parallel", "arbitrary"), vmem_limit_bytes=VMEM_LIMIT),
        name="ssd_mixer",
    )(z, xbc, dt, tail0, s0, p["conv_w"], p["conv_b"], p["dt_bias4"], p["a_log4"], p["d_x"], p["gnorm_w"],
      e3, tri, sel)


def _unit_lower_inverses(n_list, eye):
    L = CHUNK
    r = _iota((L, L), 0)
    cidx = _iota((L, L), 1)
    same16 = _block_id(r, L // 4) == _block_id(cidx, L // 4)
    same32 = _block_id(r, L // 2) == _block_id(cidx, L // 2)

    def mm(a_list, b_list):
        return [_dot(a.astype(BF16), b.astype(BF16)) for a, b in zip(a_list, b_list)]

    def axpy(t_list, d_list, sign):
        return [t + sign * d for t, d in zip(t_list, d_list)]

    nd = [jnp.where(same16, n, 0.0) for n in n_list]
    t = [eye - x for x in nd]
    pw = mm(nd, nd)
    for step in range(3):
        t = axpy(t, mm(t, pw), 1.0)
        if step < 2:
            pw = mm(pw, pw)
    n1 = [jnp.where(same32 & (~same16), n, 0.0) for n in n_list]
    t = axpy(t, mm(t, mm(n1, t)), -1.0)
    n2 = [jnp.where(same32, 0.0, n) for n in n_list]
    t = axpy(t, mm(t, mm(n2, t)), -1.0)
    return t


def _gdn_kernel(n_valid, n_chunks,
                qkv_ref, z_ref, a_ref, b_ref, tail_ref, s0_ref, cw_ref, dtb_ref, alog_ref, ow_ref,
                e3_ref, tri_ref, sel_ref,
                y_ref, sfin_ref, xp_scr, st_scr):
    L = CHUNK
    c = pl.program_id(1)

    @pl.when(c == 0)
    def _():
        xp_scr[0:TAIL, :] = tail_ref[0]
        st_scr[...] = s0_ref[0]

    u = _silu(_causal_conv(xp_scr, qkv_ref[0], cw_ref, L))

    lane = _iota((1, LANE), 1)
    head_lane = ((lane & (REP - 1)) < GDN_H) & (lane < 3 * REP)
    coef = jnp.where(head_lane, -jnp.exp(alog_ref[...]), 0.0)
    g = coef * _softplus(a_ref[0] + dtb_ref[...])
    beta = _sigmoid(b_ref[0])
    if n_valid < L:
        live = _iota((L, LANE), 0) < n_valid
        g = jnp.where(live, g, 0.0)
        beta = jnp.where(live, beta, 0.0)
    gc = _dot(tri_ref[...], g, precision=HIGHEST)
    g_last = gc[L - 1:L, :]
    stack = jnp.concatenate([gc, beta, jnp.exp(gc), jnp.exp(g_last - gc)], axis=0)
    ex = _dot(_split3_merge(stack), e3_ref[...])
    gc_x, beta_x, egc_x, egl_x = ex[0:L], ex[L:2 * L], ex[2 * L:3 * L], ex[3 * L:4 * L]
    gc_t = _dot_nt(sel_ref[...], gc, precision=HIGHEST)

    r = _iota((L, L), 0)
    cidx = _iota((L, L), 1)
    incl = r >= cidx
    strict = r > cidx
    eye = (r == cidx).astype(F32)

    heads = range(GDN_H)
    ks = [slice(h * GDN_DK, (h + 1) * GDN_DK) for h in heads]
    vs = [slice(h * GDN_DV, (h + 1) * GDN_DV) for h in heads]
    q, k, kb, dec = [], [], [], []
    for h in heads:
        q_h = u[:, ks[h]]
        k_h = u[:, GDN_KEY + h * GDN_DK:GDN_KEY + (h + 1) * GDN_DK]
        q.append(q_h * lax.rsqrt(jnp.sum(q_h * q_h, axis=-1, keepdims=True) + EPS) * (GDN_DK ** -0.5))
        k.append(k_h * lax.rsqrt(jnp.sum(k_h * k_h, axis=-1, keepdims=True) + EPS))
        kb.append(k[h] * beta_x[:, ks[h]])
        dec.append(jnp.exp(jnp.where(incl, gc_x[:, h * GDN_DK:h * GDN_DK + L]
                                     - jnp.broadcast_to(gc_t[h:h + 1, :], (L, L)), NEG_BIG)))
    kq = [_dot_nt(jnp.concatenate([kb[h], q[h]], axis=0).astype(BF16), k[h].astype(BF16))
          for h in heads]
    t_inv = _unit_lower_inverses([jnp.where(strict, kq[h][0:L] * dec[h], 0.0) for h in heads], eye)
    uw = []
    for h in heads:
        beta_h = beta_x[:, ks[h]]
        v_h = u[:, 2 * GDN_KEY + h * GDN_DV:2 * GDN_KEY + (h + 1) * GDN_DV]
        rhs = jnp.concatenate([v_h * jnp.concatenate([beta_h, beta_h], axis=1), kb[h] * egc_x[:, ks[h]]], axis=1)
        uw.append(_dot(t_inv[h].astype(BF16), rhs.astype(BF16)))
    s_old = [st_scr[h] for h in heads]
    wq = [_dot(jnp.concatenate([uw[h][:, GDN_DV:], q[h] * egc_x[:, ks[h]]], axis=0).astype(BF16),
               s_old[h].astype(BF16)) for h in heads]
    v_new = [(uw[h][:, :GDN_DV] - wq[h][0:L]).astype(BF16) for h in heads]
    o = [wq[h][L:2 * L] + _dot((kq[h][L:2 * L] * dec[h]).astype(BF16), v_new[h]) for h in heads]
    for h in heads:
        e_last = egc_x[L - 1:L, ks[h]]
        st_scr[h] = (s_old[h] * jnp.concatenate([e_last, e_last], axis=1)
                     + _dot_tn((k[h] * egl_x[:, ks[h]]).astype(BF16), v_new[h]))
    for h in heads:
        o_h = o[h] * lax.rsqrt(jnp.mean(o[h] * o[h], axis=-1, keepdims=True) + EPS) * ow_ref[...]
        y_ref[0, :, vs[h]] = (o_h * _silu(z_ref[0, :, vs[h]])).astype(BF16)

    @pl.when(c == n_chunks - 1)
    def _():
        sfin_ref[0] = st_scr[...]


def _gdn_consts():
    k = jnp.arange(LANE)[:, None]
    col = jnp.arange(GDN_KEY)[None, :]
    e3 = ((k % REP == col // GDN_DK) & (k < 3 * REP)).astype(BF16)
    t = jnp.arange(CHUNK)
    tri = (t[:, None] >= t[None, :]).astype(F32)
    sel = (jnp.arange(LANE)[None, :] == jnp.arange(GDN_H)[:, None]).astype(F32)
    return e3, tri, sel


def _gdn_mixer(qkv, z, a, b, tail0, s0, p, n_valid):
    n_seq, t_len, _ = z.shape
    n_chunks = t_len // CHUNK
    shared = tail0.shape[0] == 1
    init3 = (lambda s, c: (0, 0, 0)) if shared else (lambda s, c: (s, 0, 0))
    init4 = (lambda s, c: (0, 0, 0, 0)) if shared else (lambda s, c: (s, 0, 0, 0))
    tok_map = lambda s, c: (s, c, 0)
    const2 = lambda s, c: (0, 0)
    e3, tri, sel = _gdn_consts()
    return pl.pallas_call(
        functools.partial(_gdn_kernel, n_valid, n_chunks),
        grid=(n_seq, n_chunks),
        in_specs=[
            pl.BlockSpec((1, CHUNK, GDN_CONV), tok_map),
            pl.BlockSpec((1, CHUNK, GDN_VAL), tok_map),
            pl.BlockSpec((1, CHUNK, LANE), tok_map),
            pl.BlockSpec((1, CHUNK, LANE), tok_map),
            pl.BlockSpec((1, TAIL, GDN_CONV), init3),
            pl.BlockSpec((1, GDN_H, GDN_DK, GDN_DV), init4),
            pl.BlockSpec((CONV_K, GDN_CONV), const2),
            pl.BlockSpec((1, LANE), const2),
            pl.BlockSpec((1, LANE), const2),
            pl.BlockSpec((1, GDN_DV), const2),
            pl.BlockSpec((LANE, GDN_KEY), const2),
            pl.BlockSpec((CHUNK, CHUNK), const2),
            pl.BlockSpec((GDN_H, LANE), const2),
        ],
        out_specs=[
            pl.BlockSpec((1, CHUNK, GDN_VAL), tok_map),
            pl.BlockSpec((1, GDN_H, GDN_DK, GDN_DV), lambda s, c: (s, 0, 0, 0)),
        ],
        out_shape=[
            jax.ShapeDtypeStruct((n_seq, t_len, GDN_VAL), BF16),
            jax.ShapeDtypeStruct((n_seq, GDN_H, GDN_DK, GDN_DV), F32),
        ],
        scratch_shapes=[
            pltpu.VMEM((CHUNK + TAIL, GDN_CONV), F32),
            pltpu.VMEM((GDN_H, GDN_DK, GDN_DV), F32),
        ],
        compiler_params=pltpu.CompilerParams(
            dimension_semantics=("parallel", "arbitrary"), vmem_limit_bytes=VMEM_LIMIT),
        name="gdn_mixer",
    )(qkv, z, a, b, tail0, s0, p["conv_w"], p["dt_bias"], p["a_log"], p["onorm_w"], e3, tri, sel)


def _rep_lanes(v, n_heads):
    out = jnp.zeros((LANE,), v.dtype)
    for r in range(LANE // REP):
        if n_heads < REP and r == 3:
            break
        out = out.at[r * REP:r * REP + n_heads].set(v)
    return out.reshape(1, LANE)


def _rep_cols(w, n_heads):
    out = jnp.zeros((w.shape[0], LANE), w.dtype)
    for r in range(LANE // REP):
        if n_heads < REP and r == 3:
            break
        out = out.at[:, r * REP:r * REP + n_heads].set(w)
    return out


def _ssd_params(norm_w, w_in, conv_w, conv_b, dt_bias, a_log, d_skip, gnorm_w, w_out):
    w_main = w_in[:, :SSD_INNER + SSD_CONV]
    w_dt = w_in[:, SSD_INNER + SSD_CONV:]
    return dict(
        norm_w=norm_w,
        w_in=jnp.concatenate([w_main, _rep_cols(w_dt, SSD_H)], axis=1).astype(BF16),
        conv_w=conv_w, conv_b=conv_b.reshape(1, SSD_CONV),
        dt_bias4=_rep_lanes(dt_bias, SSD_H), a_log4=_rep_lanes(a_log, SSD_H),
        d_x=jnp.repeat(d_skip, SSD_P).reshape(1, SSD_INNER),
        gnorm_w=gnorm_w.reshape(1, SSD_INNER),
        w_out=w_out.astype(BF16),
    )


def _gdn_params(norm_w, w_in, conv_w, dt_bias, a_log, onorm_w, w_out):
    w_main = w_in[:, :GDN_CONV + GDN_VAL]
    w_a = w_in[:, GDN_CONV + GDN_VAL:GDN_CONV + GDN_VAL + GDN_H]
    w_b = w_in[:, GDN_CONV + GDN_VAL + GDN_H:]
    return dict(
        norm_w=norm_w,
        w_in=jnp.concatenate([w_main, _rep_cols(w_a, GDN_H), _rep_cols(w_b, GDN_H)], axis=1).astype(BF16),
        conv_w=conv_w,
        dt_bias=_rep_lanes(dt_bias, GDN_H), a_log=_rep_lanes(a_log, GDN_H),
        onorm_w=onorm_w.reshape(1, GDN_DV),
        w_out=w_out.astype(BF16),
    )


def _ssd_state_in(s):
    return jnp.swapaxes(s.reshape(s.shape[0], SSD_INNER, SSD_N), 1, 2)


def _ssd_state_out(s):
    return jnp.swapaxes(s, 1, 2).reshape(s.shape[0], SSD_H, SSD_P, SSD_N)


def _tail_from_rows(rows3):
    return jnp.pad(rows3, ((0, 0), (TAIL - (CONV_K - 1), 0), (0, 0)))


def _ssd_layer(h, tail0, s0, p, n_valid, tm_in, tm_out, final_w=None):
    n_seq, t_len, _ = h.shape
    h2 = h.reshape(n_seq * t_len, D_MODEL)
    z, xbc, dt = _inproj(h2, p["norm_w"], p["w_in"], (SSD_INNER, SSD_CONV, LANE), tm_in)
    xbc = xbc.reshape(n_seq, t_len, SSD_CONV)
    y, s_fin = _ssd_mixer(z.reshape(n_seq, t_len, SSD_INNER), xbc, dt.reshape(n_seq, t_len, LANE),
                          tail0, s0, p, n_valid)
    h_new = _outproj(y.reshape(n_seq * t_len, SSD_INNER), p["w_out"], h2, tm_out, final_w)
    return h_new.reshape(n_seq, t_len, D_MODEL), xbc, s_fin


def _gdn_layer(h, tail0, s0, p, n_valid, tm_in, tm_out, final_w=None):
    n_seq, t_len, _ = h.shape
    h2 = h.reshape(n_seq * t_len, D_MODEL)
    qkv, z, a, b = _inproj(h2, p["norm_w"], p["w_in"], (GDN_CONV, GDN_VAL, LANE, LANE), tm_in)
    qkv = qkv.reshape(n_seq, t_len, GDN_CONV)
    y, s_fin = _gdn_mixer(qkv, z.reshape(n_seq, t_len, GDN_VAL), a.reshape(n_seq, t_len, LANE),
                          b.reshape(n_seq, t_len, LANE), tail0, s0, p, n_valid)
    h_new = _outproj(y.reshape(n_seq * t_len, GDN_VAL), p["w_out"], h2, tm_out, final_w)
    return h_new.reshape(n_seq, t_len, D_MODEL), qkv, s_fin


def kernel(x_prompt, x_sample, state_ssd, state_ssd_conv, state_gdn, state_gdn_conv, meta_tokens,
           ssd_norm_w, ssd_w_in, ssd_conv_w, ssd_conv_b, ssd_dt_bias, ssd_a_log, ssd_d, ssd_gnorm_w, ssd_w_out,
           gdn_norm_w, gdn_w_in, gdn_conv_w, gdn_dt_bias, gdn_a_log, gdn_onorm_w, gdn_w_out, final_norm_w):
    depth = ssd_norm_w.shape[0] + gdn_norm_w.shape[0]
    n_dec, dec_t, _ = x_sample.shape
    assert dec_t == N_META and N_META <= CHUNK and x_prompt.shape[1] % CHUNK == 0
    hs = jnp.concatenate([x_sample, meta_tokens.astype(x_sample.dtype)[None]], axis=0)
    hs = jnp.pad(hs, ((0, 0), (0, CHUNK - N_META), (0, 0)))
    hp = x_prompt
    n_small = n_dec + 1
    tm_small = n_small * CHUNK
    outs = {k: [] for k in ("p_ssd", "p_ssd_conv", "p_gdn", "p_gdn_conv", "s_ssd", "s_ssd_conv", "s_gdn", "s_gdn_conv")}
    for i in range(depth):
        j = i // 2
        fw = final_norm_w if i == depth - 1 else None
        if i % 2 == 0:
            p = _ssd_params(ssd_norm_w[j], ssd_w_in[j], ssd_conv_w[j], ssd_conv_b[j], ssd_dt_bias[j],
                            ssd_a_log[j], ssd_d[j], ssd_gnorm_w[j], ssd_w_out[j])
            tail_s = _tail_from_rows(jnp.concatenate(
                [state_ssd_conv[j], jnp.zeros((1, CONV_K - 1, SSD_CONV), F32)], axis=0))
            s0_s = _ssd_state_in(jnp.concatenate(
                [state_ssd[j], jnp.zeros((1,) + state_ssd.shape[2:], F32)], axis=0))
            hs, xbc_s, sfin_s = _ssd_layer(hs, tail_s, s0_s, p, N_META, tm_small, tm_small, fw)
            outs["s_ssd"].append(_ssd_state_out(sfin_s[:n_dec]))
            outs["s_ssd_conv"].append(xbc_s[:n_dec, N_META - (CONV_K - 1):N_META])
            hp, xbc_p, sfin_p = _ssd_layer(hp, xbc_s[n_dec:, N_META - TAIL:N_META], sfin_s[n_dec:], p,
                                           CHUNK, 256, 512, fw)
            outs["p_ssd"].append(_ssd_state_out(sfin_p))
            outs["p_ssd_conv"].append(xbc_p[:, -(CONV_K - 1):])
        else:
            p = _gdn_params(gdn_norm_w[j], gdn_w_in[j], gdn_conv_w[j], gdn_dt_bias[j], gdn_a_log[j],
                            gdn_onorm_w[j], gdn_w_out[j])
            tail_s = _tail_from_rows(jnp.concatenate(
                [state_gdn_conv[j], jnp.zeros((1, CONV_K - 1, GDN_CONV), F32)], axis=0))
            s0_s = jnp.concatenate([state_gdn[j], jnp.zeros((1,) + state_gdn.shape[2:], F32)], axis=0)
            hs, qkv_s, sfin_s = _gdn_layer(hs, tail_s, s0_s, p, N_META, tm_small, tm_small, fw)
            outs["s_gdn"].append(sfin_s[:n_dec])
            outs["s_gdn_conv"].append(qkv_s[:n_dec, N_META - (CONV_K - 1):N_META])
            hp, qkv_p, sfin_p = _gdn_layer(hp, qkv_s[n_dec:, N_META - TAIL:N_META], sfin_s[n_dec:], p,
                                           CHUNK, 256, 512, fw)
            outs["p_gdn"].append(sfin_p)
            outs["p_gdn_conv"].append(qkv_p[:, -(CONV_K - 1):])
    y_prompt = hp
    y_sample = hs[:n_dec, :N_META]
    st = lambda k: jnp.stack(outs[k])
    return (y_prompt, y_sample, st("p_ssd"), st("p_ssd_conv"), st("p_gdn"), st("p_gdn_conv"),
            st("s_ssd"), st("s_ssd_conv"), st("s_gdn"), st("s_gdn_conv"))
```

```python
import functools

import jax
import jax.numpy as jnp
from jax import lax
from jax.experimental import pallas as pl
from jax.experimental.pallas import tpu as pltpu

F32 = jnp.float32
BF16 = jnp.bfloat16
HIGHEST = lax.Precision.HIGHEST

D_MODEL = 1024
N_META = 16
CONV_K = 4
EPS = 1e-6
CHUNK = 64
TAIL = 8
LANE = 128
COL_BLK = 512
SSD_INNER = 2048
SSD_P = 64
SSD_H = 32
SSD_G = 4
SSD_N = 128
SSD_GW = SSD_INNER // SSD_G
SSD_CONV = SSD_INNER + 2 * SSD_G * SSD_N
GDN_DK = 128
GDN_DV = 256
GDN_H = 8
GDN_KEY = GDN_H * GDN_DK
GDN_VAL = GDN_H * GDN_DV
GDN_CONV = 2 * GDN_KEY + GDN_VAL
REP = 32
NEG_BIG = -1e30

VMEM_LIMIT = 48 * 1024 * 1024
TM_PROMPT_IN = 256
TM_PROMPT_OUT = 512


def _sigmoid(x):
    return 1.0 / (1.0 + jnp.exp(-x))


def _silu(x):
    return x * _sigmoid(x)


def _softplus(x):
    return jnp.maximum(x, 0.0) + jnp.log1p(jnp.exp(-jnp.abs(x)))


def _dot(a, b, precision=None):
    return jnp.dot(a, b, preferred_element_type=F32, precision=precision)


def _dot_nt(a, b, precision=None):
    return lax.dot_general(a, b, (((1,), (1,)), ((), ())), preferred_element_type=F32, precision=precision)


def _dot_tn(a, b):
    return lax.dot_general(a, b, (((0,), (0,)), ((), ())), preferred_element_type=F32)


def _iota(shape, dim):
    return lax.broadcasted_iota(jnp.int32, shape, dim)


def _split3_merge(x):
    hi = x.astype(BF16)
    r1 = x - hi.astype(F32)
    mid = r1.astype(BF16)
    lo = (r1 - mid.astype(F32)).astype(BF16)
    lane = _iota(x.shape, 1)
    zero = jnp.zeros_like(hi)
    return jnp.where(lane < REP, hi, jnp.where(lane < 2 * REP, mid, jnp.where(lane < 3 * REP, lo, zero)))


def _rmsnorm_bf16(x, w):
    ms = jnp.mean(x * x, axis=-1, keepdims=True)
    return (x * lax.rsqrt(ms + EPS) * w).astype(BF16)


def _conv_block(xp_scr, base, seg, cols, raw, cw_ref, bias):
    xp_scr[base + TAIL:base + TAIL + seg, cols] = raw
    acc = xp_scr[pl.ds(base + TAIL - (CONV_K - 1), seg), cols] * cw_ref[0:1, cols]
    for k in range(1, CONV_K):
        acc = acc + xp_scr[pl.ds(base + TAIL - (CONV_K - 1) + k, seg), cols] * cw_ref[k:k + 1, cols]
    if bias is not None:
        acc = acc + bias
    return acc


def _roll_tail(xp_scr, base, n_valid, tailo_ref, s):
    last = xp_scr[base + n_valid:base + n_valid + TAIL, :]
    tailo_ref[s] = last
    xp_scr[base:base + TAIL, :] = last


def _ssd_inproj_kernel(spt, seg, n_valid, x_ref, nw_ref, w_ref, wdt_ref, cw_ref, cb_ref, dtb_ref, tail_ref,
                       zs_ref, u_ref, dt_ref, tailo_ref, xp_scr):
    pitch = seg + TAIL

    @pl.when(pl.program_id(1) == 0)
    def _():
        for s in range(spt):
            xp_scr[s * pitch:s * pitch + TAIL, :] = tail_ref[s]

    xn = _rmsnorm_bf16(x_ref[0], nw_ref[...])
    dt_ref[0] = _softplus(_dot(xn, wdt_ref[...]) + dtb_ref[...])
    for j in range(SSD_INNER // COL_BLK):
        cols = slice(j * COL_BLK, (j + 1) * COL_BLK)
        zs_ref[0, :, cols] = _silu(_dot(xn, w_ref[:, cols]))
    for j in range(SSD_CONV // COL_BLK):
        cols = slice(j * COL_BLK, (j + 1) * COL_BLK)
        raw = _dot(xn, w_ref[:, SSD_INNER + j * COL_BLK:SSD_INNER + (j + 1) * COL_BLK])
        for s in range(spt):
            acc = _conv_block(xp_scr, s * pitch, seg, cols, raw[s * seg:(s + 1) * seg], cw_ref, cb_ref[:, cols])
            u_ref[0, s * seg:(s + 1) * seg, cols] = _silu(acc)
    for s in range(spt):
        _roll_tail(xp_scr, s * pitch, n_valid, tailo_ref, s)


def _gdn_inproj_kernel(spt, seg, n_valid, x_ref, nw_ref, w_ref, wab_ref, cw_ref, dtb_ref, alog_ref, tail_ref,
                       q_ref, k_ref, v_ref, zs_ref, g_ref, beta_ref, tailo_ref, xp_scr):
    pitch = seg + TAIL

    @pl.when(pl.program_id(1) == 0)
    def _():
        for s in range(spt):
            xp_scr[s * pitch:s * pitch + TAIL, :] = tail_ref[s]

    xn = _rmsnorm_bf16(x_ref[0], nw_ref[...])
    ab = _dot(xn, wab_ref[...])
    lane = _iota((1, LANE), 1)
    head_lane = ((lane & (REP - 1)) < GDN_H) & (lane < 3 * REP)
    coef = jnp.where(head_lane, -jnp.exp(alog_ref[...]), 0.0)
    g_ref[0] = coef * _softplus(ab[:, :LANE] + dtb_ref[...])
    beta_ref[0] = _sigmoid(ab[:, LANE:])
    for j in range(GDN_VAL // COL_BLK):
        cols = slice(j * COL_BLK, (j + 1) * COL_BLK)
        zs_ref[0, :, cols] = _silu(_dot(xn, w_ref[:, GDN_CONV + j * COL_BLK:GDN_CONV + (j + 1) * COL_BLK]))
    for j in range(GDN_CONV // COL_BLK):
        cols = slice(j * COL_BLK, (j + 1) * COL_BLK)
        raw = _dot(xn, w_ref[:, cols])
        for s in range(spt):
            rows = slice(s * seg, (s + 1) * seg)
            u = _silu(_conv_block(xp_scr, s * pitch, seg, cols, raw[rows], cw_ref, None))
            if j * COL_BLK < 2 * GDN_KEY:
                is_q = j * COL_BLK < GDN_KEY
                dst = q_ref if is_q else k_ref
                off = j * COL_BLK - (0 if is_q else GDN_KEY)
                for i in range(COL_BLK // GDN_DK):
                    t = u[:, i * GDN_DK:(i + 1) * GDN_DK]
                    t = t * lax.rsqrt(jnp.sum(t * t, axis=-1, keepdims=True) + EPS)
                    if is_q:
                        t = t * (GDN_DK ** -0.5)
                    dst[0, rows, off + i * GDN_DK:off + (i + 1) * GDN_DK] = t
            else:
                off = j * COL_BLK - 2 * GDN_KEY
                v_ref[0, rows, off:off + COL_BLK] = u
    for s in range(spt):
        _roll_tail(xp_scr, s * pitch, n_valid, tailo_ref, s)


def _inproj_call(body, x, tail0, spt, tm, n_valid, weights, consts, out_widths, conv_dim, name):
    n_grp, rows, _ = x.shape
    assert rows % tm == 0 and tm % spt == 0
    seg = tm // spt
    n_tiles = rows // tm
    assert (spt == 1 and n_valid == seg) or n_tiles == 1
    assert n_valid % TAIL == 0 and TAIL <= n_valid <= seg
    shared = tail0.shape[0] == 1
    tok_map = lambda g, t: (g, t, 0)
    tail_map = (lambda g, t: (0, 0, 0)) if shared else (lambda g, t: (g, 0, 0))
    in_specs = [pl.BlockSpec((1, tm, D_MODEL), tok_map), pl.BlockSpec((1, D_MODEL), lambda g, t: (0, 0))]
    args = [x, consts[0]]
    for w, layer in weights:
        if layer is None:
            in_specs.append(pl.BlockSpec(w.shape, lambda g, t: (0, 0), pipeline_mode=pl.Buffered(1)))
        else:
            in_specs.append(pl.BlockSpec((None,) + w.shape[1:], lambda g, t, layer=layer: (layer, 0, 0),
                                         pipeline_mode=pl.Buffered(1)))
        args.append(w)
    for cst in consts[1:]:
        in_specs.append(pl.BlockSpec(cst.shape, lambda g, t: (0, 0)))
        args.append(cst)
    in_specs.append(pl.BlockSpec((spt, TAIL, conv_dim), tail_map))
    args.append(tail0)
    out_specs = [pl.BlockSpec((1, tm, wd), tok_map) for wd in out_widths]
    out_shape = [jax.ShapeDtypeStruct((n_grp, rows, wd), F32) for wd in out_widths]
    out_specs.append(pl.BlockSpec((spt, TAIL, conv_dim), lambda g, t: (g, 0, 0)))
    out_shape.append(jax.ShapeDtypeStruct((n_grp * spt, TAIL, conv_dim), F32))
    return pl.pallas_call(
        functools.partial(body, spt, seg, n_valid),
        grid=(n_grp, n_tiles),
        in_specs=in_specs, out_specs=out_specs, out_shape=out_shape,
        scratch_shapes=[pltpu.VMEM((spt * (seg + TAIL), conv_dim), F32)],
        compiler_params=pltpu.CompilerParams(
            dimension_semantics=("parallel", "arbitrary"), vmem_limit_bytes=VMEM_LIMIT),
        name=name,
    )(*args)


def _outproj_kernel(final, y_ref, w_ref, h_ref, *rest):
    if final:
        fw_ref, o_ref = rest
    else:
        (o_ref,) = rest
    h = h_ref[...] + _dot(y_ref[...], w_ref[...])
    if final:
        ms = jnp.mean(h * h, axis=-1, keepdims=True)
        h = h * lax.rsqrt(ms + EPS) * fw_ref[...]
    o_ref[...] = h


def _outproj(y2d, w_stack, layer, h2d, tm, final_w=None):
    m, k = y2d.shape
    assert m % tm == 0
    final = final_w is not None
    in_specs = [
        pl.BlockSpec((tm, k), lambda i: (i, 0)),
        pl.BlockSpec((None, k, D_MODEL), lambda i: (layer, 0, 0), pipeline_mode=pl.Buffered(1)),
        pl.BlockSpec((tm, D_MODEL), lambda i: (i, 0)),
    ]
    args = [y2d, w_stack, h2d]
    if final:
        in_specs.append(pl.BlockSpec((1, D_MODEL), lambda i: (0, 0)))
        args.append(final_w.reshape(1, D_MODEL))
    return pl.pallas_call(
        functools.partial(_outproj_kernel, final),
        grid=(m // tm,),
        in_specs=in_specs,
        out_specs=pl.BlockSpec((tm, D_MODEL), lambda i: (i, 0)),
        out_shape=jax.ShapeDtypeStruct((m, D_MODEL), F32),
        compiler_params=pltpu.CompilerParams(
            dimension_semantics=("parallel",), vmem_limit_bytes=VMEM_LIMIT),
        name="outproj_final" if final else "outproj",
    )(*args)


def _ssd_kernel(n_valid, n_chunks,
                zs_ref, u_ref, dt_ref, s0_ref, alog_ref, dsk_ref, gw_ref, e3_ref, tri_ref, sel_ref,
                cmask_ref, bmask_ref,
                y_ref, sfin_ref, st_scr):
    L = CHUNK
    c = pl.program_id(1)

    @pl.when(c == 0)
    def _():
        st_scr[...] = s0_ref[0]

    xs = u_ref[0, :, 0:SSD_INNER]
    bm_b = u_ref[0, :, SSD_INNER:SSD_INNER + SSD_G * SSD_N].astype(BF16)
    cm_b = u_ref[0, :, SSD_INNER + SSD_G * SSD_N:SSD_CONV].astype(BF16)

    dt4 = dt_ref[0]
    if n_valid < L:
        dt4 = jnp.where(_iota(dt4.shape, 0) < n_valid, dt4, 0.0)
    da4 = dt4 * (-jnp.exp(alog_ref[...]))
    cs4 = _dot(tri_ref[...], da4, precision=HIGHEST)
    cs_last = cs4[L - 1:L, :]
    stack = jnp.concatenate([cs4, dt4, jnp.exp(cs_last - cs4), jnp.exp(cs4)], axis=0)
    ex = _dot(_split3_merge(stack), e3_ref[...])
    cs_x, dt_x, wq_x, ecs_x = ex[0:L], ex[L:2 * L], ex[2 * L:3 * L], ex[3 * L:4 * L]

    lane4 = _iota(cs4.shape, 1)
    r_mat = jnp.concatenate([jnp.where(lane4 < REP, cs4, 0.0),
                             jnp.where((lane4 >= REP) & (lane4 < 2 * REP), cs4, 0.0)], axis=0)
    cs_t2 = _dot_nt(sel_ref[...], r_mat, precision=HIGHEST)
    row_x = jnp.concatenate(
        [jnp.broadcast_to(cs_t2[j:j + 1, :], (L, LANE)) for j in range(SSD_H // 2)], axis=1)
    decay = jnp.exp(cs_x - row_x + cmask_ref[...])

    cb_parts = []
    for g in range(SSD_G):
        cg = cm_b[:, g * SSD_N:(g + 1) * SSD_N]
        bg = bm_b[:, g * SSD_N:(g + 1) * SSD_N]
        cb2 = _dot_nt(cg, jnp.concatenate([bg, bg], axis=0))
        cb_parts.extend([cb2] * (SSD_GW // LANE))
    m_x = (jnp.concatenate(cb_parts, axis=1) * decay).astype(BF16)

    xdt = xs * dt_x
    xdt_b = xdt.astype(BF16)
    blk = 4 * L
    y_parts = []
    for j in range(SSD_INNER // blk):
        xj = xdt_b[:, j * blk:(j + 1) * blk]
        rhs = jnp.concatenate([xj] * 4, axis=0) * bmask_ref[...]
        y_parts.append(_dot(m_x[:, j * blk:(j + 1) * blk], rhs))
    y = jnp.concatenate(y_parts, axis=1)

    xw_b = (xdt * wq_x).astype(BF16)
    e_last = ecs_x[L - 1:L, :]
    off_parts = []
    for g in range(SSD_G):
        sl = slice(g * SSD_GW, (g + 1) * SSD_GW)
        st_g = st_scr[:, sl]
        off_parts.append(_dot(cm_b[:, g * SSD_N:(g + 1) * SSD_N], st_g.astype(BF16)))
        st_scr[:, sl] = st_g * e_last[:, sl] + _dot_tn(bm_b[:, g * SSD_N:(g + 1) * SSD_N], xw_b[:, sl])
    y = y + jnp.concatenate(off_parts, axis=1) * ecs_x + xs * dsk_ref[...]

    yg = y * zs_ref[0]
    for g in range(SSD_G):
        sl = slice(g * SSD_GW, (g + 1) * SSD_GW)
        blk_g = yg[:, sl]
        ms = jnp.mean(blk_g * blk_g, axis=-1, keepdims=True)
        y_ref[0, :, sl] = (blk_g * lax.rsqrt(ms + EPS) * gw_ref[:, sl]).astype(BF16)

    @pl.when(c == n_chunks - 1)
    def _():
        sfin_ref[0] = st_scr[...]


def _ssd_consts():
    k = jnp.arange(LANE)[:, None]
    col = jnp.arange(SSD_INNER)[None, :]
    e3 = ((k % REP == col // SSD_P) & (k < 3 * REP)).astype(BF16)
    t = jnp.arange(CHUNK)
    tri = (t[:, None] >= t[None, :]).astype(F32)
    j = jnp.arange(SSD_H // 2)[:, None]
    kk = jnp.arange(LANE)[None, :]
    sel = ((kk == 2 * j) | (kk == REP + 2 * j + 1)).astype(F32)
    cmask = jnp.where(t[:, None] >= (col % CHUNK), 0.0, NEG_BIG).astype(F32)
    b = jnp.arange(4 * CHUNK)
    bmask = ((b[:, None] // CHUNK) == (b[None, :] // CHUNK)).astype(BF16)
    return e3, tri, sel, cmask, bmask


def _ssd_mixer(zs, u, dt, s0, p, n_valid):
    n_seq, t_len, _ = zs.shape
    n_chunks = t_len // CHUNK
    shared = s0.shape[0] == 1
    init_map = (lambda s, c: (0, 0, 0)) if shared else (lambda s, c: (s, 0, 0))
    tok_map = lambda s, c: (s, c, 0)
    const2 = lambda s, c: (0, 0)
    consts = (p["a_log4"], p["d_x"], p["gnorm_w"]) + _ssd_consts()
    return pl.pallas_call(
        functools.partial(_ssd_kernel, n_valid, n_chunks),
        grid=(n_seq, n_chunks),
        in_specs=[
            pl.BlockSpec((1, CHUNK, SSD_INNER), tok_map),
            pl.BlockSpec((1, CHUNK, SSD_CONV), tok_map),
            pl.BlockSpec((1, CHUNK, LANE), tok_map),
            pl.BlockSpec((1, SSD_N, SSD_INNER), init_map),
        ] + [pl.BlockSpec(cst.shape, const2) for cst in consts],
        out_specs=[
            pl.BlockSpec((1, CHUNK, SSD_INNER), tok_map),
            pl.BlockSpec((1, SSD_N, SSD_INNER), lambda s, c: (s, 0, 0)),
        ],
        out_shape=[
            jax.ShapeDtypeStruct((n_seq, t_len, SSD_INNER), BF16),
            jax.ShapeDtypeStruct((n_seq, SSD_N, SSD_INNER), F32),
        ],
        scratch_shapes=[pltpu.VMEM((SSD_N, SSD_INNER), F32)],
        compiler_params=pltpu.CompilerParams(
            dimension_semantics=("parallel", "arbitrary"), vmem_limit_bytes=VMEM_LIMIT),
        name="ssd_mixer",
    )(zs, u, dt, s0, *consts)


def _unit_lower_inverses(n_list, eye, same16, same32):
    def mm(a_list, b_list):
        return [_dot(a.astype(BF16), b.astype(BF16)) for a, b in zip(a_list, b_list)]

    def axpy(t_list, d_list, sign):
        return [t + sign * d for t, d in zip(t_list, d_list)]

    nd = [n * same16 for n in n_list]
    t = [eye - x for x in nd]
    pw = mm(nd, nd)
    for step in range(3):
        t = axpy(t, mm(t, pw), 1.0)
        if step < 2:
            pw = mm(pw, pw)
    n1 = [n * (same32 - same16) for n in n_list]
    t = axpy(t, mm(t, mm(n1, t)), -1.0)
    n2 = [n * (1.0 - same32) for n in n_list]
    t = axpy(t, mm(t, mm(n2, t)), -1.0)
    return t


def _gdn_kernel(n_valid, n_chunks,
                q_ref, k_ref, v_ref, zs_ref, g_ref, beta_ref, s0_ref, ow_ref, e3_ref, tri_ref, sel_ref,
                masks_ref,
                y_ref, sfin_ref, st_scr):
    L = CHUNK
    c = pl.program_id(1)

    @pl.when(c == 0)
    def _():
        st_scr[...] = s0_ref[0]

    g = g_ref[0]
    beta = beta_ref[0]
    if n_valid < L:
        live = _iota((L, LANE), 0) < n_valid
        g = jnp.where(live, g, 0.0)
        beta = jnp.where(live, beta, 0.0)
    gc = _dot(tri_ref[...], g, precision=HIGHEST)
    g_last = gc[L - 1:L, :]
    stack = jnp.concatenate([gc, beta, jnp.exp(gc), jnp.exp(g_last - gc)], axis=0)
    ex = _dot(_split3_merge(stack), e3_ref[...])
    gc_x, beta_x, egc_x, egl_x = ex[0:L], ex[L:2 * L], ex[2 * L:3 * L], ex[3 * L:4 * L]
    gc_t = _dot_nt(sel_ref[...], gc, precision=HIGHEST)

    incl_add = masks_ref[0]
    strict = masks_ref[1]
    eye = masks_ref[2]
    same16 = masks_ref[3]
    same32 = masks_ref[4]

    heads = range(GDN_H)
    ks = [slice(h * GDN_DK, (h + 1) * GDN_DK) for h in heads]
    vs = [slice(h * GDN_DV, (h + 1) * GDN_DV) for h in heads]
    q = [q_ref[0, :, ks[h]] for h in heads]
    k = [k_ref[0, :, ks[h]] for h in heads]
    kb = [k[h] * beta_x[:, ks[h]] for h in heads]
    dec = [jnp.exp(gc_x[:, h * GDN_DK:h * GDN_DK + L] - jnp.broadcast_to(gc_t[h:h + 1, :], (L, L)) + incl_add)
           for h in heads]
    kq = [_dot_nt(jnp.concatenate([kb[h], q[h]], axis=0).astype(BF16), k[h].astype(BF16))
          for h in heads]
    t_inv = _unit_lower_inverses([kq[h][0:L] * dec[h] * strict for h in heads], eye, same16, same32)
    uw = []
    for h in heads:
        beta_h = beta_x[:, ks[h]]
        rhs = jnp.concatenate([v_ref[0, :, vs[h]] * jnp.concatenate([beta_h, beta_h], axis=1),
                               kb[h] * egc_x[:, ks[h]]], axis=1)
        uw.append(_dot(t_inv[h].astype(BF16), rhs.astype(BF16)))
    s_old = [st_scr[h] for h in heads]
    wq = [_dot(jnp.concatenate([uw[h][:, GDN_DV:], q[h] * egc_x[:, ks[h]]], axis=0).astype(BF16),
               s_old[h].astype(BF16)) for h in heads]
    v_new = [(uw[h][:, :GDN_DV] - wq[h][0:L]).astype(BF16) for h in heads]
    o = [wq[h][L:2 * L] + _dot((kq[h][L:2 * L] * dec[h]).astype(BF16), v_new[h]) for h in heads]
    for h in heads:
        e_last = egc_x[L - 1:L, ks[h]]
        st_scr[h] = (s_old[h] * jnp.concatenate([e_last, e_last], axis=1)
                     + _dot_tn((k[h] * egl_x[:, ks[h]]).astype(BF16), v_new[h]))
    for h in heads:
        o_h = o[h] * lax.rsqrt(jnp.mean(o[h] * o[h], axis=-1, keepdims=True) + EPS) * ow_ref[...]
        y_ref[0, :, vs[h]] = (o_h * zs_ref[0, :, vs[h]]).astype(BF16)

    @pl.when(c == n_chunks - 1)
    def _():
        sfin_ref[0] = st_scr[...]


def _gdn_consts():
    k = jnp.arange(LANE)[:, None]
    col = jnp.arange(GDN_KEY)[None, :]
    e3 = ((k % REP == col // GDN_DK) & (k < 3 * REP)).astype(BF16)
    t = jnp.arange(CHUNK)
    r, cc = t[:, None], t[None, :]
    tri = (r >= cc).astype(F32)
    sel = (jnp.arange(LANE)[None, :] == jnp.arange(GDN_H)[:, None]).astype(F32)
    masks = jnp.stack([
        jnp.where(r >= cc, 0.0, NEG_BIG),
        (r > cc).astype(F32),
        (r == cc).astype(F32),
        ((r // (CHUNK // 4)) == (cc // (CHUNK // 4))).astype(F32),
        ((r // (CHUNK // 2)) == (cc // (CHUNK // 2))).astype(F32),
    ]).astype(F32)
    return e3, tri, sel, masks


def _gdn_mixer(q, k, v, zs, g, beta, s0, p, n_valid):
    n_seq, t_len, _ = zs.shape
    n_chunks = t_len // CHUNK
    shared = s0.shape[0] == 1
    init4 = (lambda s, c: (0, 0, 0, 0)) if shared else (lambda s, c: (s, 0, 0, 0))
    tok_map = lambda s, c: (s, c, 0)
    e3, tri, sel, masks = _gdn_consts()
    consts2 = (p["onorm_w"], e3, tri, sel)
    return pl.pallas_call(
        functools.partial(_gdn_kernel, n_valid, n_chunks),
        grid=(n_seq, n_chunks),
        in_specs=[
            pl.BlockSpec((1, CHUNK, GDN_KEY), tok_map),
            pl.BlockSpec((1, CHUNK, GDN_KEY), tok_map),
            pl.BlockSpec((1, CHUNK, GDN_VAL), tok_map),
            pl.BlockSpec((1, CHUNK, GDN_VAL), tok_map),
            pl.BlockSpec((1, CHUNK, LANE), tok_map),
            pl.BlockSpec((1, CHUNK, LANE), tok_map),
            pl.BlockSpec((1, GDN_H, GDN_DK, GDN_DV), init4),
        ] + [pl.BlockSpec(cst.shape, lambda s, c: (0, 0)) for cst in consts2]
          + [pl.BlockSpec(masks.shape, lambda s, c: (0, 0, 0))],
        out_specs=[
            pl.BlockSpec((1, CHUNK, GDN_VAL), tok_map),
            pl.BlockSpec((1, GDN_H, GDN_DK, GDN_DV), lambda s, c: (s, 0, 0, 0)),
        ],
        out_shape=[
            jax.ShapeDtypeStruct((n_seq, t_len, GDN_VAL), BF16),
            jax.ShapeDtypeStruct((n_seq, GDN_H, GDN_DK, GDN_DV), F32),
        ],
        scratch_shapes=[pltpu.VMEM((GDN_H, GDN_DK, GDN_DV), F32)],
        compiler_params=pltpu.CompilerParams(
            dimension_semantics=("parallel", "arbitrary"), vmem_limit_bytes=VMEM_LIMIT),
        name="gdn_mixer",
    )(q, k, v, zs, g, beta, s0, *consts2, masks)


def _rep_lanes(v, n_heads):
    n_rep = LANE // REP if n_heads == REP else 3
    row = jnp.pad(v, (0, REP - n_heads))
    return jnp.pad(jnp.tile(row, n_rep), (0, LANE - n_rep * REP)).reshape(1, LANE)


def _rep_cols(w, n_heads):
    n_rep = LANE // REP if n_heads == REP else 3
    blk = jnp.pad(w, ((0, 0), (0, REP - n_heads)))
    return jnp.pad(jnp.tile(blk, (1, n_rep)), ((0, 0), (0, LANE - n_rep * REP)))


def _ssd_state_in(s):
    return jnp.swapaxes(s.reshape(s.shape[0], SSD_INNER, SSD_N), 1, 2)


def _ssd_state_out(s):
    return jnp.swapaxes(s, 1, 2).reshape(s.shape[0], SSD_H, SSD_P, SSD_N)


def _tail_from_rows(rows3):
    return jnp.pad(rows3, ((0, 0), (TAIL - (CONV_K - 1), 0), (0, 0)))


def _segment_valid(n_valid, tm_in, spt):
    return tm_in // spt if n_valid == CHUNK else n_valid


def _ssd_layer(h, tail0, s0, p, layer, spt, tm_in, tm_out, n_valid, final_w=None):
    n_grp, rows, _ = h.shape
    weights = [(p["w_in"], layer), (p["w_dt4"][layer], None)]
    consts = [p["norm_w"][layer].reshape(1, D_MODEL), p["conv_w"][layer], p["conv_b"][layer].reshape(1, SSD_CONV),
              p["dt_bias4"][layer]]
    zs, u, dt, tails = _inproj_call(_ssd_inproj_kernel, h, tail0, spt, tm_in, _segment_valid(n_valid, tm_in, spt),
                                    weights, consts, (SSD_INNER, SSD_CONV, LANE), SSD_CONV, "ssd_inproj")
    seq = lambda a: a.reshape(n_grp * spt, rows // spt, a.shape[-1])
    mp = dict(a_log4=p["a_log4"][layer], d_x=p["d_x"][layer], gnorm_w=p["gnorm_w"][layer])
    y, s_fin = _ssd_mixer(seq(zs), seq(u), seq(dt), s0, mp, n_valid)
    h_new = _outproj(y.reshape(n_grp * rows, SSD_INNER), p["w_out"], layer, h.reshape(n_grp * rows, D_MODEL),
                     tm_out, final_w)
    return h_new.reshape(h.shape), tails, s_fin


def _gdn_layer(h, tail0, s0, p, layer, spt, tm_in, tm_out, n_valid, final_w=None):
    n_grp, rows, _ = h.shape
    weights = [(p["w_in"], layer), (p["w_ab"][layer], None)]
    consts = [p["norm_w"][layer].reshape(1, D_MODEL), p["conv_w"][layer], p["dt_bias"][layer], p["a_log"][layer]]
    q, k, v, zs, g, beta, tails = _inproj_call(
        _gdn_inproj_kernel, h, tail0, spt, tm_in, _segment_valid(n_valid, tm_in, spt), weights, consts,
        (GDN_KEY, GDN_KEY, GDN_VAL, GDN_VAL, LANE, LANE), GDN_CONV, "gdn_inproj")
    seq = lambda a: a.reshape(n_grp * spt, rows // spt, a.shape[-1])
    mp = dict(onorm_w=p["onorm_w"][layer].reshape(1, GDN_DV))
    y, s_fin = _gdn_mixer(seq(q), seq(k), seq(v), seq(zs), seq(g), seq(beta), s0, mp, n_valid)
    h_new = _outproj(y.reshape(n_grp * rows, GDN_VAL), p["w_out"], layer, h.reshape(n_grp * rows, D_MODEL),
                     tm_out, final_w)
    return h_new.reshape(h.shape), tails, s_fin


def kernel(x_prompt, x_sample, state_ssd, state_ssd_conv, state_gdn, state_gdn_conv, meta_tokens,
           ssd_norm_w, ssd_w_in, ssd_conv_w, ssd_conv_b, ssd_dt_bias, ssd_a_log, ssd_d, ssd_gnorm_w, ssd_w_out,
           gdn_norm_w, gdn_w_in, gdn_conv_w, gdn_dt_bias, gdn_a_log, gdn_onorm_w, gdn_w_out, final_norm_w):
    n_ssd, n_gdn = ssd_norm_w.shape[0], gdn_norm_w.shape[0]
    depth = n_ssd + n_gdn
    n_dec, dec_t, _ = x_sample.shape
    assert dec_t == N_META and N_META <= CHUNK and x_prompt.shape[1] % TM_PROMPT_OUT == 0
    keep = CONV_K - 1

    ssd_p = dict(
        norm_w=ssd_norm_w, w_in=ssd_w_in.astype(BF16), w_out=ssd_w_out.astype(BF16),
        w_dt4=[_rep_cols(ssd_w_in[j][:, SSD_INNER + SSD_CONV:], SSD_H).astype(BF16) for j in range(n_ssd)],
        conv_w=ssd_conv_w, conv_b=ssd_conv_b,
        dt_bias4=[_rep_lanes(ssd_dt_bias[j], SSD_H) for j in range(n_ssd)],
        a_log4=[_rep_lanes(ssd_a_log[j], SSD_H) for j in range(n_ssd)],
        d_x=[jnp.repeat(ssd_d[j], SSD_P).reshape(1, SSD_INNER) for j in range(n_ssd)],
        gnorm_w=[ssd_gnorm_w[j].reshape(1, SSD_INNER) for j in range(n_ssd)],
    )
    ab0 = GDN_CONV + GDN_VAL
    gdn_p = dict(
        norm_w=gdn_norm_w, w_in=gdn_w_in.astype(BF16), w_out=gdn_w_out.astype(BF16),
        w_ab=[jnp.concatenate([_rep_cols(gdn_w_in[j][:, ab0:ab0 + GDN_H], GDN_H),
                               _rep_cols(gdn_w_in[j][:, ab0 + GDN_H:], GDN_H)], axis=1).astype(BF16)
              for j in range(n_gdn)],
        conv_w=gdn_conv_w,
        dt_bias=[_rep_lanes(gdn_dt_bias[j], GDN_H) for j in range(n_gdn)],
        a_log=[_rep_lanes(gdn_a_log[j], GDN_H) for j in range(n_gdn)],
        onorm_w=gdn_onorm_w,
    )

    n_small = n_dec + 1
    hs = jnp.concatenate([x_sample, meta_tokens.astype(x_sample.dtype)[None]], axis=0)
    hs = jnp.pad(hs, ((0, 0), (0, CHUNK - N_META), (0, 0))).reshape(1, n_small * CHUNK, D_MODEL)
    hp = x_prompt
    tm_small = n_small * CHUNK
    last_rows = lambda tails: tails[:, TAIL - keep:]
    outs = {k: [] for k in ("p_ssd", "p_ssd_conv", "p_gdn", "p_gdn_conv", "s_ssd", "s_ssd_conv", "s_gdn", "s_gdn_conv")}
    for i in range(depth):
        j = i // 2
        fw = final_norm_w if i == depth - 1 else None
        if i % 2 == 0:
            tail_s = _tail_from_rows(jnp.concatenate([state_ssd_conv[j], jnp.zeros((1, keep, SSD_CONV), F32)], axis=0))
            s0_s = _ssd_state_in(jnp.concatenate([state_ssd[j], jnp.zeros((1,) + state_ssd.shape[2:], F32)], axis=0))
            hs, tails_s, sfin_s = _ssd_layer(hs, tail_s, s0_s, ssd_p, j, n_small, tm_small, tm_small, N_META, fw)
            hp, tails_p, sfin_p = _ssd_layer(hp, tails_s[n_dec:], sfin_s[n_dec:], ssd_p, j, 1, TM_PROMPT_IN,
                                             TM_PROMPT_OUT, CHUNK, fw)
            outs["s_ssd"].append(_ssd_state_out(sfin_s[:n_dec]))
            outs["p_ssd"].append(_ssd_state_out(sfin_p))
            outs["s_ssd_conv"].append(last_rows(tails_s[:n_dec]))
            outs["p_ssd_conv"].append(last_rows(tails_p))
        else:
            tail_s = _tail_from_rows(jnp.concatenate([state_gdn_conv[j], jnp.zeros((1, keep, GDN_CONV), F32)], axis=0))
            s0_s = jnp.concatenate([state_gdn[j], jnp.zeros((1,) + state_gdn.shape[2:], F32)], axis=0)
            hs, tails_s, sfin_s = _gdn_layer(hs, tail_s, s0_s, gdn_p, j, n_small, tm_small, tm_small, N_META, fw)
            hp, tails_p, sfin_p = _gdn_layer(hp, tails_s[n_dec:], sfin_s[n_dec:], gdn_p, j, 1, TM_PROMPT_IN,
                                             TM_PROMPT_OUT, CHUNK, fw)
            outs["s_gdn"].append(sfin_s[:n_dec])
            outs["p_gdn"].append(sfin_p)
            outs["s_gdn_conv"].append(last_rows(tails_s[:n_dec]))
            outs["p_gdn_conv"].append(last_rows(tails_p))
    y_sample = hs.reshape(n_small, CHUNK, D_MODEL)[:n_dec, :N_META]
    st = lambda key: jnp.stack(outs[key])
    return (hp, y_sample, st("p_ssd"), st("p_ssd_conv"), st("p_gdn"), st("p_gdn_conv"),
            st("s_ssd"), st("s_ssd_conv"), st("s_gdn"), st("s_gdn_conv"))
```

```python
import functools

import jax
import jax.numpy as jnp
from jax import lax
from jax.experimental import pallas as pl
from jax.experimental.pallas import tpu as pltpu

F32 = jnp.float32
BF16 = jnp.bfloat16
HIGHEST = lax.Precision.HIGHEST

D_MODEL = 1024
N_META = 16
CONV_K = 4
EPS = 1e-6
CHUNK = 64
TAIL = 8
LANE = 128
COL_BLK = 512
SSD_INNER = 2048
SSD_P = 64
SSD_H = 32
SSD_G = 4
SSD_N = 128
SSD_GW = SSD_INNER // SSD_G
SSD_CONV = SSD_INNER + 2 * SSD_G * SSD_N
GDN_DK = 128
GDN_DV = 256
GDN_H = 8
GDN_KEY = GDN_H * GDN_DK
GDN_VAL = GDN_H * GDN_DV
GDN_CONV = 2 * GDN_KEY + GDN_VAL
REP = 32
NEG_BIG = -1e30

VMEM_LIMIT = 48 * 1024 * 1024
TM_PROMPT_IN = 256
TM_PROMPT_OUT = 512


def _sigmoid(x):
    return 1.0 / (1.0 + jnp.exp(-x))


def _silu(x):
    return x * _sigmoid(x)


def _softplus(x):
    return jnp.maximum(x, 0.0) + jnp.log1p(jnp.exp(-jnp.abs(x)))


def _dot(a, b, precision=None):
    return jnp.dot(a, b, preferred_element_type=F32, precision=precision)


def _dot_nt(a, b, precision=None):
    return lax.dot_general(a, b, (((1,), (1,)), ((), ())), preferred_element_type=F32, precision=precision)


def _dot_tn(a, b):
    return lax.dot_general(a, b, (((0,), (0,)), ((), ())), preferred_element_type=F32)


def _iota(shape, dim):
    return lax.broadcasted_iota(jnp.int32, shape, dim)


def _split3_merge(x):
    hi = x.astype(BF16)
    r1 = x - hi.astype(F32)
    mid = r1.astype(BF16)
    lo = (r1 - mid.astype(F32)).astype(BF16)
    lane = _iota(x.shape, 1)
    zero = jnp.zeros_like(hi)
    return jnp.where(lane < REP, hi, jnp.where(lane < 2 * REP, mid, jnp.where(lane < 3 * REP, lo, zero)))


def _rmsnorm_bf16(x, w):
    ms = jnp.mean(x * x, axis=-1, keepdims=True)
    return (x * lax.rsqrt(ms + EPS) * w).astype(BF16)


def _conv_block(xp_scr, base, seg, cols, raw, cw_ref, bias):
    xp_scr[base + TAIL:base + TAIL + seg, cols] = raw
    x = xp_scr[base:base + TAIL + seg, cols]
    acc = x * cw_ref[0:1, cols]
    for k in range(1, CONV_K):
        acc = x * cw_ref[k:k + 1, cols] + pltpu.roll(acc, 1, 0)
    acc = acc[TAIL:]
    if bias is not None:
        acc = acc + bias
    return acc


def _roll_tail(xp_scr, base, n_valid, tailo_ref, s):
    last = xp_scr[base + n_valid:base + n_valid + TAIL, :]
    tailo_ref[s] = last
    xp_scr[base:base + TAIL, :] = last


def _ssd_inproj_kernel(spt, seg, n_valid, x_ref, nw_ref, w_ref, wdt_ref, cw_ref, cb_ref, dtb_ref, tail_ref,
                       zs_ref, u_ref, dt_ref, tailo_ref, xp_scr):
    pitch = seg + TAIL

    @pl.when(pl.program_id(1) == 0)
    def _():
        for s in range(spt):
            xp_scr[s * pitch:s * pitch + TAIL, :] = tail_ref[s]

    xn = _rmsnorm_bf16(x_ref[0], nw_ref[...])
    dt_ref[0] = _softplus(_dot(xn, wdt_ref[...]) + dtb_ref[...])

    def conv_dot(j):
        return _dot(xn, w_ref[:, SSD_INNER + j * COL_BLK:SSD_INNER + (j + 1) * COL_BLK])

    n_conv, n_gate = SSD_CONV // COL_BLK, SSD_INNER // COL_BLK
    raw = conv_dot(0)
    for j in range(n_conv):
        cols = slice(j * COL_BLK, (j + 1) * COL_BLK)
        raw_next = conv_dot(j + 1) if j + 1 < n_conv else None
        gate = _dot(xn, w_ref[:, cols]) if j < n_gate else None
        for s in range(spt):
            acc = _conv_block(xp_scr, s * pitch, seg, cols, raw[s * seg:(s + 1) * seg], cw_ref, cb_ref[:, cols])
            u_ref[0, s * seg:(s + 1) * seg, cols] = _silu(acc)
        if gate is not None:
            zs_ref[0, :, cols] = _silu(gate)
        raw = raw_next
    for s in range(spt):
        _roll_tail(xp_scr, s * pitch, n_valid, tailo_ref, s)


def _gdn_inproj_kernel(spt, seg, n_valid, x_ref, nw_ref, w_ref, wab_ref, cw_ref, dtb_ref, alog_ref, tail_ref,
                       q_ref, k_ref, v_ref, zs_ref, g_ref, beta_ref, tailo_ref, xp_scr):
    pitch = seg + TAIL

    @pl.when(pl.program_id(1) == 0)
    def _():
        for s in range(spt):
            xp_scr[s * pitch:s * pitch + TAIL, :] = tail_ref[s]

    xn = _rmsnorm_bf16(x_ref[0], nw_ref[...])
    ab = _dot(xn, wab_ref[...])
    lane = _iota((1, LANE), 1)
    head_lane = ((lane & (REP - 1)) < GDN_H) & (lane < 3 * REP)
    coef = jnp.where(head_lane, -jnp.exp(alog_ref[...]), 0.0)
    g_ref[0] = coef * _softplus(ab[:, :LANE] + dtb_ref[...])
    beta_ref[0] = _sigmoid(ab[:, LANE:])
    def conv_dot(j):
        return _dot(xn, w_ref[:, j * COL_BLK:(j + 1) * COL_BLK])

    n_conv, n_gate = GDN_CONV // COL_BLK, GDN_VAL // COL_BLK
    raw_next = conv_dot(0)
    for j in range(n_conv):
        cols = slice(j * COL_BLK, (j + 1) * COL_BLK)
        raw = raw_next
        raw_next = conv_dot(j + 1) if j + 1 < n_conv else None
        if j < n_gate:
            gate = _dot(xn, w_ref[:, GDN_CONV + j * COL_BLK:GDN_CONV + (j + 1) * COL_BLK])
            zs_ref[0, :, cols] = _silu(gate)
        for s in range(spt):
            rows = slice(s * seg, (s + 1) * seg)
            u = _silu(_conv_block(xp_scr, s * pitch, seg, cols, raw[rows], cw_ref, None))
            if j * COL_BLK < 2 * GDN_KEY:
                is_q = j * COL_BLK < GDN_KEY
                dst = q_ref if is_q else k_ref
                off = j * COL_BLK - (0 if is_q else GDN_KEY)
                for i in range(COL_BLK // GDN_DK):
                    t = u[:, i * GDN_DK:(i + 1) * GDN_DK]
                    t = t * lax.rsqrt(jnp.sum(t * t, axis=-1, keepdims=True) + EPS)
                    if is_q:
                        t = t * (GDN_DK ** -0.5)
                    dst[0, rows, off + i * GDN_DK:off + (i + 1) * GDN_DK] = t
            else:
                off = j * COL_BLK - 2 * GDN_KEY
                v_ref[0, rows, off:off + COL_BLK] = u
    for s in range(spt):
        _roll_tail(xp_scr, s * pitch, n_valid, tailo_ref, s)


def _inproj_call(body, x, tail0, spt, tm, n_valid, weights, consts, out_widths, conv_dim, name):
    n_grp, rows, _ = x.shape
    assert rows % tm == 0 and tm % spt == 0
    seg = tm // spt
    n_tiles = rows // tm
    assert (spt == 1 and n_valid == seg) or n_tiles == 1
    assert n_valid % TAIL == 0 and TAIL <= n_valid <= seg
    shared = tail0.shape[0] == 1
    tok_map = lambda g, t: (g, t, 0)
    tail_map = (lambda g, t: (0, 0, 0)) if shared else (lambda g, t: (g, 0, 0))
    in_specs = [pl.BlockSpec((1, tm, D_MODEL), tok_map), pl.BlockSpec((1, D_MODEL), lambda g, t: (0, 0))]
    args = [x, consts[0]]
    for w, layer in weights:
        if layer is None:
            in_specs.append(pl.BlockSpec(w.shape, lambda g, t: (0, 0), pipeline_mode=pl.Buffered(1)))
        else:
            in_specs.append(pl.BlockSpec((None,) + w.shape[1:], lambda g, t, layer=layer: (layer, 0, 0),
                                         pipeline_mode=pl.Buffered(1)))
        args.append(w)
    for cst in consts[1:]:
        in_specs.append(pl.BlockSpec(cst.shape, lambda g, t: (0, 0)))
        args.append(cst)
    in_specs.append(pl.BlockSpec((spt, TAIL, conv_dim), tail_map))
    args.append(tail0)
    out_specs = [pl.BlockSpec((1, tm, wd), tok_map) for wd in out_widths]
    out_shape = [jax.ShapeDtypeStruct((n_grp, rows, wd), F32) for wd in out_widths]
    out_specs.append(pl.BlockSpec((spt, TAIL, conv_dim), lambda g, t: (g, 0, 0)))
    out_shape.append(jax.ShapeDtypeStruct((n_grp * spt, TAIL, conv_dim), F32))
    return pl.pallas_call(
        functools.partial(body, spt, seg, n_valid),
        grid=(n_grp, n_tiles),
        in_specs=in_specs, out_specs=out_specs, out_shape=out_shape,
        scratch_shapes=[pltpu.VMEM((spt * (seg + TAIL), conv_dim), F32)],
        compiler_params=pltpu.CompilerParams(
            dimension_semantics=("parallel", "arbitrary"), vmem_limit_bytes=VMEM_LIMIT),
        name=name,
    )(*args)


def _outproj_kernel(final, y_ref, w_ref, h_ref, *rest):
    if final:
        fw_ref, o_ref = rest
    else:
        (o_ref,) = rest
    h = h_ref[...] + _dot(y_ref[...], w_ref[...])
    if final:
        ms = jnp.mean(h * h, axis=-1, keepdims=True)
        h = h * lax.rsqrt(ms + EPS) * fw_ref[...]
    o_ref[...] = h


def _outproj(y2d, w_stack, layer, h2d, tm, final_w=None):
    m, k = y2d.shape
    assert m % tm == 0
    final = final_w is not None
    in_specs = [
        pl.BlockSpec((tm, k), lambda i: (i, 0)),
        pl.BlockSpec((None, k, D_MODEL), lambda i: (layer, 0, 0), pipeline_mode=pl.Buffered(1)),
        pl.BlockSpec((tm, D_MODEL), lambda i: (i, 0)),
    ]
    args = [y2d, w_stack, h2d]
    if final:
        in_specs.append(pl.BlockSpec((1, D_MODEL), lambda i: (0, 0)))
        args.append(final_w.reshape(1, D_MODEL))
    return pl.pallas_call(
        functools.partial(_outproj_kernel, final),
        grid=(m // tm,),
        in_specs=in_specs,
        out_specs=pl.BlockSpec((tm, D_MODEL), lambda i: (i, 0)),
        out_shape=jax.ShapeDtypeStruct((m, D_MODEL), F32),
        compiler_params=pltpu.CompilerParams(
            dimension_semantics=("parallel",), vmem_limit_bytes=VMEM_LIMIT),
        name="outproj_final" if final else "outproj",
    )(*args)


def _ssd_kernel(nb, shared, n_valid, n_chunks,
                zs_ref, u_ref, dt_ref, s0_ref, alog_ref, dsk_ref, gw_ref, e3_ref, tri_ref, sel_ref,
                cmask_ref, bmask_ref,
                y_ref, sfin_ref, st_scr):
    L = CHUNK
    c = pl.program_id(1)
    seqs = range(nb)
    groups = range(SSD_G)
    gn = [slice(g * SSD_N, (g + 1) * SSD_N) for g in groups]
    gw = [slice(g * SSD_GW, (g + 1) * SSD_GW) for g in groups]

    @pl.when(c == 0)
    def _():
        for b in seqs:
            st_scr[b] = s0_ref[0 if shared else b].T

    xs = [u_ref[b, :, 0:SSD_INNER] for b in seqs]
    bm_b = [u_ref[b, :, SSD_INNER:SSD_INNER + SSD_G * SSD_N].astype(BF16) for b in seqs]
    cm_b = [u_ref[b, :, SSD_INNER + SSD_G * SSD_N:SSD_CONV].astype(BF16) for b in seqs]

    dt4 = [dt_ref[b] for b in seqs]
    if n_valid < L:
        live = _iota((L, LANE), 0) < n_valid
        dt4 = [jnp.where(live, d, 0.0) for d in dt4]
    neg_a = -jnp.exp(alog_ref[...])
    cs4 = [_dot(tri_ref[...], dt4[b] * neg_a, precision=HIGHEST) for b in seqs]
    stack = []
    for b in seqs:
        stack += [cs4[b], dt4[b], jnp.exp(cs4[b][L - 1:L, :] - cs4[b]), jnp.exp(cs4[b])]
    ex = _dot(_split3_merge(jnp.concatenate(stack, axis=0)), e3_ref[...])
    cs_x = [ex[(4 * b) * L:(4 * b + 1) * L] for b in seqs]
    dt_x = [ex[(4 * b + 1) * L:(4 * b + 2) * L] for b in seqs]
    wq_x = [ex[(4 * b + 2) * L:(4 * b + 3) * L] for b in seqs]
    ecs_x = [ex[(4 * b + 3) * L:(4 * b + 4) * L] for b in seqs]

    lane4 = _iota((L, LANE), 1)
    lo_lanes = lane4 < REP
    hi_lanes = (lane4 >= REP) & (lane4 < 2 * REP)
    cs_t2 = [_dot_nt(sel_ref[...], jnp.concatenate([jnp.where(lo_lanes, cs4[b], 0.0),
                                                    jnp.where(hi_lanes, cs4[b], 0.0)], axis=0),
                     precision=HIGHEST) for b in seqs]
    decay = []
    for b in seqs:
        row_x = jnp.concatenate(
            [jnp.broadcast_to(cs_t2[b][j:j + 1, :], (L, LANE)) for j in range(SSD_H // 2)], axis=1)
        decay.append(jnp.exp(cs_x[b] - row_x + cmask_ref[...]))

    cb2 = [[_dot_nt(cm_b[b][:, gn[g]], jnp.concatenate([bm_b[b][:, gn[g]]] * 2, axis=0)) for b in seqs]
           for g in groups]
    m_x = [(jnp.concatenate([cb2[g][b] for g in groups for _ in range(SSD_GW // LANE)], axis=1)
            * decay[b]).astype(BF16) for b in seqs]

    xdt = [xs[b] * dt_x[b] for b in seqs]
    xdt_b = [x.astype(BF16) for x in xdt]
    blk = 4 * L
    y_parts = [[] for _ in seqs]
    for j in range(SSD_INNER // blk):
        for b in seqs:
            xj = xdt_b[b][:, j * blk:(j + 1) * blk]
            rhs = jnp.concatenate([xj] * 4, axis=0) * bmask_ref[...]
            y_parts[b].append(_dot(m_x[b][:, j * blk:(j + 1) * blk], rhs))

    xw_b = [(xdt[b] * wq_x[b]).astype(BF16) for b in seqs]
    off_parts = [[] for _ in seqs]
    for g in groups:
        for b in seqs:
            st_g = st_scr[b, :, gw[g]]
            off_parts[b].append(_dot(cm_b[b][:, gn[g]], st_g.astype(BF16)))
            st_scr[b, :, gw[g]] = (st_g * ecs_x[b][L - 1:L, gw[g]]
                                   + _dot_tn(bm_b[b][:, gn[g]], xw_b[b][:, gw[g]]))
    for b in seqs:
        y = (jnp.concatenate(y_parts[b], axis=1) + jnp.concatenate(off_parts[b], axis=1) * ecs_x[b]
             + xs[b] * dsk_ref[...])
        yg = y * zs_ref[b]
        for g in groups:
            blk_g = yg[:, gw[g]]
            ms = jnp.mean(blk_g * blk_g, axis=-1, keepdims=True)
            y_ref[b, :, gw[g]] = (blk_g * lax.rsqrt(ms + EPS) * gw_ref[:, gw[g]]).astype(BF16)

    @pl.when(c == n_chunks - 1)
    def _():
        for b in seqs:
            sfin_ref[b] = st_scr[b].T


def _ssd_consts():
    k = jnp.arange(LANE)[:, None]
    col = jnp.arange(SSD_INNER)[None, :]
    e3 = ((k % REP == col // SSD_P) & (k < 3 * REP)).astype(BF16)
    t = jnp.arange(CHUNK)
    tri = (t[:, None] >= t[None, :]).astype(F32)
    j = jnp.arange(SSD_H // 2)[:, None]
    kk = jnp.arange(LANE)[None, :]
    sel = ((kk == 2 * j) | (kk == REP + 2 * j + 1)).astype(F32)
    cmask = jnp.where(t[:, None] >= (col % CHUNK), 0.0, NEG_BIG).astype(F32)
    b = jnp.arange(4 * CHUNK)
    bmask = ((b[:, None] // CHUNK) == (b[None, :] // CHUNK)).astype(BF16)
    return e3, tri, sel, cmask, bmask


def _seqs_per_step(n_seq):
    return 2 if n_seq % 2 == 0 else 1


def _ssd_mixer(zs, u, dt, s0, p, n_valid):
    n_seq, t_len, _ = zs.shape
    n_chunks = t_len // CHUNK
    nb = _seqs_per_step(n_seq)
    shared = s0.shape[0] == 1
    init_spec = (pl.BlockSpec((1, SSD_INNER, SSD_N), lambda s, c: (0, 0, 0)) if shared
                 else pl.BlockSpec((nb, SSD_INNER, SSD_N), lambda s, c: (s, 0, 0)))
    tok_map = lambda s, c: (s, c, 0)
    const2 = lambda s, c: (0, 0)
    consts = (p["a_log4"], p["d_x"], p["gnorm_w"]) + _ssd_consts()
    return pl.pallas_call(
        functools.partial(_ssd_kernel, nb, shared, n_valid, n_chunks),
        grid=(n_seq // nb, n_chunks),
        in_specs=[
            pl.BlockSpec((nb, CHUNK, SSD_INNER), tok_map),
            pl.BlockSpec((nb, CHUNK, SSD_CONV), tok_map),
            pl.BlockSpec((nb, CHUNK, LANE), tok_map),
            init_spec,
        ] + [pl.BlockSpec(cst.shape, const2) for cst in consts],
        out_specs=[
            pl.BlockSpec((nb, CHUNK, SSD_INNER), tok_map),
            pl.BlockSpec((nb, SSD_INNER, SSD_N), lambda s, c: (s, 0, 0)),
        ],
        out_shape=[
            jax.ShapeDtypeStruct((n_seq, t_len, SSD_INNER), BF16),
            jax.ShapeDtypeStruct((n_seq, SSD_INNER, SSD_N), F32),
        ],
        scratch_shapes=[pltpu.VMEM((nb, SSD_N, SSD_INNER), F32)],
        compiler_params=pltpu.CompilerParams(
            dimension_semantics=("parallel", "arbitrary"), vmem_limit_bytes=VMEM_LIMIT),
        name="ssd_mixer",
    )(zs, u, dt, s0, *consts)


def _unit_lower_inverses(n_list, eye, same16, same32):
    def mm(a_list, b_list):
        return [_dot(a.astype(BF16), b.astype(BF16)) for a, b in zip(a_list, b_list)]

    def axpy(t_list, d_list, sign):
        return [t + sign * d for t, d in zip(t_list, d_list)]

    nd = [n * same16 for n in n_list]
    t = [eye - x for x in nd]
    pw = mm(nd, nd)
    for step in range(3):
        t = axpy(t, mm(t, pw), 1.0)
        if step < 2:
            pw = mm(pw, pw)
    n1 = [n * (same32 - same16) for n in n_list]
    t = axpy(t, mm(t, mm(n1, t)), -1.0)
    n2 = [n * (1.0 - same32) for n in n_list]
    t = axpy(t, mm(t, mm(n2, t)), -1.0)
    return t


def _gdn_kernel(nb, shared, n_valid, n_chunks,
                q_ref, k_ref, v_ref, zs_ref, g_ref, beta_ref, s0_ref, ow_ref, e3_ref, tri_ref, sel_ref,
                masks_ref,
                y_ref, sfin_ref, st_scr):
    L = CHUNK
    c = pl.program_id(1)
    seqs = range(nb)

    @pl.when(c == 0)
    def _():
        for b in seqs:
            st_scr[b] = s0_ref[0 if shared else b]

    g = [g_ref[b] for b in seqs]
    beta = [beta_ref[b] for b in seqs]
    if n_valid < L:
        live = _iota((L, LANE), 0) < n_valid
        g = [jnp.where(live, x, 0.0) for x in g]
        beta = [jnp.where(live, x, 0.0) for x in beta]
    gc = [_dot(tri_ref[...], g[b], precision=HIGHEST) for b in seqs]
    stack = []
    for b in seqs:
        stack += [gc[b], beta[b], jnp.exp(gc[b]), jnp.exp(gc[b][L - 1:L, :] - gc[b])]
    ex = _dot(_split3_merge(jnp.concatenate(stack, axis=0)), e3_ref[...])
    gc_x = [ex[(4 * b) * L:(4 * b + 1) * L] for b in seqs]
    beta_x = [ex[(4 * b + 1) * L:(4 * b + 2) * L] for b in seqs]
    egc_x = [ex[(4 * b + 2) * L:(4 * b + 3) * L] for b in seqs]
    egl_x = [ex[(4 * b + 3) * L:(4 * b + 4) * L] for b in seqs]
    gc_t = [_dot_nt(sel_ref[...], gc[b], precision=HIGHEST) for b in seqs]

    incl_add = masks_ref[0]
    strict = masks_ref[1]
    eye = masks_ref[2]
    same16 = masks_ref[3]
    same32 = masks_ref[4]

    chains = [(b, h) for h in range(GDN_H) for b in seqs]
    ks = [slice(h * GDN_DK, (h + 1) * GDN_DK) for h in range(GDN_H)]
    vs = [slice(h * GDN_DV, (h + 1) * GDN_DV) for h in range(GDN_H)]
    q = [q_ref[b, :, ks[h]] for b, h in chains]
    k = [k_ref[b, :, ks[h]] for b, h in chains]
    egc = [egc_x[b][:, ks[h]] for b, h in chains]
    kb = [k[i] * beta_x[b][:, ks[h]] for i, (b, h) in enumerate(chains)]
    dec = [jnp.exp(gc_x[b][:, h * GDN_DK:h * GDN_DK + L] - jnp.broadcast_to(gc_t[b][h:h + 1, :], (L, L)) + incl_add)
           for b, h in chains]
    n_ch = range(len(chains))
    kq = [_dot_nt(jnp.concatenate([kb[i], q[i]], axis=0).astype(BF16), k[i].astype(BF16))
          for i in n_ch]
    t_inv = _unit_lower_inverses([kq[i][0:L] * dec[i] * strict for i in n_ch], eye, same16, same32)
    uw = []
    for i, (b, h) in enumerate(chains):
        beta_h = beta_x[b][:, ks[h]]
        rhs = jnp.concatenate([v_ref[b, :, vs[h]] * jnp.concatenate([beta_h, beta_h], axis=1),
                               kb[i] * egc[i]], axis=1)
        uw.append(_dot(t_inv[i].astype(BF16), rhs.astype(BF16)))
    s_old = [st_scr[b, h] for b, h in chains]
    wq = [_dot(jnp.concatenate([uw[i][:, GDN_DV:], q[i] * egc[i]], axis=0).astype(BF16),
               s_old[i].astype(BF16)) for i in n_ch]
    v_new = [(uw[i][:, :GDN_DV] - wq[i][0:L]).astype(BF16) for i in n_ch]
    o = [wq[i][L:2 * L] + _dot((kq[i][L:2 * L] * dec[i]).astype(BF16), v_new[i]) for i in n_ch]
    for i, (b, h) in enumerate(chains):
        e_last = egc[i][L - 1:L, :]
        st_scr[b, h] = (s_old[i] * jnp.concatenate([e_last, e_last], axis=1)
                        + _dot_tn((k[i] * egl_x[b][:, ks[h]]).astype(BF16), v_new[i]))
    for i, (b, h) in enumerate(chains):
        o_h = o[i] * lax.rsqrt(jnp.mean(o[i] * o[i], axis=-1, keepdims=True) + EPS) * ow_ref[...]
        y_ref[b, :, vs[h]] = (o_h * zs_ref[b, :, vs[h]]).astype(BF16)

    @pl.when(c == n_chunks - 1)
    def _():
        for b in seqs:
            sfin_ref[b] = st_scr[b]


def _gdn_consts():
    k = jnp.arange(LANE)[:, None]
    col = jnp.arange(GDN_KEY)[None, :]
    e3 = ((k % REP == col // GDN_DK) & (k < 3 * REP)).astype(BF16)
    t = jnp.arange(CHUNK)
    r, cc = t[:, None], t[None, :]
    tri = (r >= cc).astype(F32)
    sel = (jnp.arange(LANE)[None, :] == jnp.arange(GDN_H)[:, None]).astype(F32)
    masks = jnp.stack([
        jnp.where(r >= cc, 0.0, NEG_BIG),
        (r > cc).astype(F32),
        (r == cc).astype(F32),
        ((r // (CHUNK // 4)) == (cc // (CHUNK // 4))).astype(F32),
        ((r // (CHUNK // 2)) == (cc // (CHUNK // 2))).astype(F32),
    ]).astype(F32)
    return e3, tri, sel, masks


def _gdn_mixer(q, k, v, zs, g, beta, s0, p, n_valid):
    n_seq, t_len, _ = zs.shape
    n_chunks = t_len // CHUNK
    nb = _seqs_per_step(n_seq)
    shared = s0.shape[0] == 1
    init_spec = (pl.BlockSpec((1, GDN_H, GDN_DK, GDN_DV), lambda s, c: (0, 0, 0, 0)) if shared
                 else pl.BlockSpec((nb, GDN_H, GDN_DK, GDN_DV), lambda s, c: (s, 0, 0, 0)))
    tok_map = lambda s, c: (s, c, 0)
    e3, tri, sel, masks = _gdn_consts()
    consts2 = (p["onorm_w"], e3, tri, sel)
    return pl.pallas_call(
        functools.partial(_gdn_kernel, nb, shared, n_valid, n_chunks),
        grid=(n_seq // nb, n_chunks),
        in_specs=[
            pl.BlockSpec((nb, CHUNK, GDN_KEY), tok_map),
            pl.BlockSpec((nb, CHUNK, GDN_KEY), tok_map),
            pl.BlockSpec((nb, CHUNK, GDN_VAL), tok_map),
            pl.BlockSpec((nb, CHUNK, GDN_VAL), tok_map),
            pl.BlockSpec((nb, CHUNK, LANE), tok_map),
            pl.BlockSpec((nb, CHUNK, LANE), tok_map),
            init_spec,
        ] + [pl.BlockSpec(cst.shape, lambda s, c: (0, 0)) for cst in consts2]
          + [pl.BlockSpec(masks.shape, lambda s, c: (0, 0, 0))],
        out_specs=[
            pl.BlockSpec((nb, CHUNK, GDN_VAL), tok_map),
            pl.BlockSpec((nb, GDN_H, GDN_DK, GDN_DV), lambda s, c: (s, 0, 0, 0)),
        ],
        out_shape=[
            jax.ShapeDtypeStruct((n_seq, t_len, GDN_VAL), BF16),
            jax.ShapeDtypeStruct((n_seq, GDN_H, GDN_DK, GDN_DV), F32),
        ],
        scratch_shapes=[pltpu.VMEM((nb, GDN_H, GDN_DK, GDN_DV), F32)],
        compiler_params=pltpu.CompilerParams(
            dimension_semantics=("parallel", "arbitrary"), vmem_limit_bytes=VMEM_LIMIT),
        name="gdn_mixer",
    )(q, k, v, zs, g, beta, s0, *consts2, masks)


def _rep_lanes(v, n_heads):
    n_rep = LANE // REP if n_heads == REP else 3
    row = jnp.pad(v, (0, REP - n_heads))
    return jnp.pad(jnp.tile(row, n_rep), (0, LANE - n_rep * REP)).reshape(1, LANE)


def _rep_cols(w, n_heads):
    n_rep = LANE // REP if n_heads == REP else 3
    blk = jnp.pad(w, ((0, 0), (0, REP - n_heads)))
    return jnp.pad(jnp.tile(blk, (1, n_rep)), ((0, 0), (0, LANE - n_rep * REP)))


def _ssd_state_in(s):
    return s.reshape(s.shape[0], SSD_INNER, SSD_N)


def _ssd_state_out(s):
    return s.reshape(s.shape[0], SSD_H, SSD_P, SSD_N)


def _tail_from_rows(rows3):
    return jnp.pad(rows3, ((0, 0), (TAIL - (CONV_K - 1), 0), (0, 0)))


def _segment_valid(n_valid, tm_in, spt):
    return tm_in // spt if n_valid == CHUNK else n_valid


def _ssd_layer(h, tail0, s0, p, layer, spt, tm_in, tm_out, n_valid, final_w=None):
    n_grp, rows, _ = h.shape
    weights = [(p["w_in"], layer), (p["w_dt4"][layer], None)]
    consts = [p["norm_w"][layer].reshape(1, D_MODEL), p["conv_w"][layer], p["conv_b"][layer].reshape(1, SSD_CONV),
              p["dt_bias4"][layer]]
    zs, u, dt, tails = _inproj_call(_ssd_inproj_kernel, h, tail0, spt, tm_in, _segment_valid(n_valid, tm_in, spt),
                                    weights, consts, (SSD_INNER, SSD_CONV, LANE), SSD_CONV, "ssd_inproj")
    seq = lambda a: a.reshape(n_grp * spt, rows // spt, a.shape[-1])
    mp = dict(a_log4=p["a_log4"][layer], d_x=p["d_x"][layer], gnorm_w=p["gnorm_w"][layer])
    y, s_fin = _ssd_mixer(seq(zs), seq(u), seq(dt), s0, mp, n_valid)
    h_new = _outproj(y.reshape(n_grp * rows, SSD_INNER), p["w_out"], layer, h.reshape(n_grp * rows, D_MODEL),
                     tm_out, final_w)
    return h_new.reshape(h.shape), tails, s_fin


def _gdn_layer(h, tail0, s0, p, layer, spt, tm_in, tm_out, n_valid, final_w=None):
    n_grp, rows, _ = h.shape
    weights = [(p["w_in"], layer), (p["w_ab"][layer], None)]
    consts = [p["norm_w"][layer].reshape(1, D_MODEL), p["conv_w"][layer], p["dt_bias"][layer], p["a_log"][layer]]
    q, k, v, zs, g, beta, tails = _inproj_call(
        _gdn_inproj_kernel, h, tail0, spt, tm_in, _segment_valid(n_valid, tm_in, spt), weights, consts,
        (GDN_KEY, GDN_KEY, GDN_VAL, GDN_VAL, LANE, LANE), GDN_CONV, "gdn_inproj")
    seq = lambda a: a.reshape(n_grp * spt, rows // spt, a.shape[-1])
    mp = dict(onorm_w=p["onorm_w"][layer].reshape(1, GDN_DV))
    y, s_fin = _gdn_mixer(seq(q), seq(k), seq(v), seq(zs), seq(g), seq(beta), s0, mp, n_valid)
    h_new = _outproj(y.reshape(n_grp * rows, GDN_VAL), p["w_out"], layer, h.reshape(n_grp * rows, D_MODEL),
                     tm_out, final_w)
    return h_new.reshape(h.shape), tails, s_fin


def kernel(x_prompt, x_sample, state_ssd, state_ssd_conv, state_gdn, state_gdn_conv, meta_tokens,
           ssd_norm_w, ssd_w_in, ssd_conv_w, ssd_conv_b, ssd_dt_bias, ssd_a_log, ssd_d, ssd_gnorm_w, ssd_w_out,
           gdn_norm_w, gdn_w_in, gdn_conv_w, gdn_dt_bias, gdn_a_log, gdn_onorm_w, gdn_w_out, final_norm_w):
    n_ssd, n_gdn = ssd_norm_w.shape[0], gdn_norm_w.shape[0]
    depth = n_ssd + n_gdn
    n_dec, dec_t, _ = x_sample.shape
    assert dec_t == N_META and N_META <= CHUNK and x_prompt.shape[1] % TM_PROMPT_OUT == 0
    keep = CONV_K - 1

    ssd_p = dict(
        norm_w=ssd_norm_w, w_in=ssd_w_in.astype(BF16), w_out=ssd_w_out.astype(BF16),
        w_dt4=[_rep_cols(ssd_w_in[j][:, SSD_INNER + SSD_CONV:], SSD_H).astype(BF16) for j in range(n_ssd)],
        conv_w=ssd_conv_w, conv_b=ssd_conv_b,
        dt_bias4=[_rep_lanes(ssd_dt_bias[j], SSD_H) for j in range(n_ssd)],
        a_log4=[_rep_lanes(ssd_a_log[j], SSD_H) for j in range(n_ssd)],
        d_x=[jnp.repeat(ssd_d[j], SSD_P).reshape(1, SSD_INNER) for j in range(n_ssd)],
        gnorm_w=[ssd_gnorm_w[j].reshape(1, SSD_INNER) for j in range(n_ssd)],
    )
    ab0 = GDN_CONV + GDN_VAL
    gdn_p = dict(
        norm_w=gdn_norm_w, w_in=gdn_w_in.astype(BF16), w_out=gdn_w_out.astype(BF16),
        w_ab=[jnp.concatenate([_rep_cols(gdn_w_in[j][:, ab0:ab0 + GDN_H], GDN_H),
                               _rep_cols(gdn_w_in[j][:, ab0 + GDN_H:], GDN_H)], axis=1).astype(BF16)
              for j in range(n_gdn)],
        conv_w=gdn_conv_w,
        dt_bias=[_rep_lanes(gdn_dt_bias[j], GDN_H) for j in range(n_gdn)],
        a_log=[_rep_lanes(gdn_a_log[j], GDN_H) for j in range(n_gdn)],
        onorm_w=gdn_onorm_w,
    )

    n_small = n_dec + 1
    hs = jnp.concatenate([x_sample, meta_tokens.astype(x_sample.dtype)[None]], axis=0)
    hs = jnp.pad(hs, ((0, 0), (0, CHUNK - N_META), (0, 0))).reshape(1, n_small * CHUNK, D_MODEL)
    hp = x_prompt
    tm_small = n_small * CHUNK
    last_rows = lambda tails: tails[:, TAIL - keep:]
    outs = {k: [] for k in ("p_ssd", "p_ssd_conv", "p_gdn", "p_gdn_conv", "s_ssd", "s_ssd_conv", "s_gdn", "s_gdn_conv")}
    for i in range(depth):
        j = i // 2
        fw = final_norm_w if i == depth - 1 else None
        if i % 2 == 0:
            tail_s = _tail_from_rows(jnp.concatenate([state_ssd_conv[j], jnp.zeros((1, keep, SSD_CONV), F32)], axis=0))
            s0_s = _ssd_state_in(jnp.concatenate([state_ssd[j], jnp.zeros((1,) + state_ssd.shape[2:], F32)], axis=0))
            hs, tails_s, sfin_s = _ssd_layer(hs, tail_s, s0_s, ssd_p, j, n_small, tm_small, tm_small, N_META, fw)
            hp, tails_p, sfin_p = _ssd_layer(hp, tails_s[n_dec:], sfin_s[n_dec:], ssd_p, j, 1, TM_PROMPT_IN,
                                             TM_PROMPT_OUT, CHUNK, fw)
            outs["s_ssd"].append(_ssd_state_out(sfin_s[:n_dec]))
            outs["p_ssd"].append(_ssd_state_out(sfin_p))
            outs["s_ssd_conv"].append(last_rows(tails_s[:n_dec]))
            outs["p_ssd_conv"].append(last_rows(tails_p))
        else:
            tail_s = _tail_from_rows(jnp.concatenate([state_gdn_conv[j], jnp.zeros((1, keep, GDN_CONV), F32)], axis=0))
            s0_s = jnp.concatenate([state_gdn[j], jnp.zeros((1,) + state_gdn.shape[2:], F32)], axis=0)
            hs, tails_s, sfin_s = _gdn_layer(hs, tail_s, s0_s, gdn_p, j, n_small, tm_small, tm_small, N_META, fw)
            hp, tails_p, sfin_p = _gdn_layer(hp, tails_s[n_dec:], sfin_s[n_dec:], gdn_p, j, 1, TM_PROMPT_IN,
                                             TM_PROMPT_OUT, CHUNK, fw)
            outs["s_gdn"].append(sfin_s[:n_dec])
            outs["p_gdn"].append(sfin_p)
            outs["s_gdn_conv"].append(last_rows(tails_s[:n_dec]))
            outs["p_gdn_conv"].append(last_rows(tails_p))
    y_sample = hs.reshape(n_small, CHUNK, D_MODEL)[:n_dec, :N_META]
    st = lambda key: jnp.stack(outs[key])
    return (hp, y_sample, st("p_ssd"), st("p_ssd_conv"), st("p_gdn"), st("p_gdn_conv"),
            st("s_ssd"), st("s_ssd_conv"), st("s_gdn"), st("s_gdn_conv"))
```

```python
import functools

import jax
import jax.numpy as jnp
from jax import lax
from jax.experimental import pallas as pl
from jax.experimental.pallas import tpu as pltpu

F32 = jnp.float32
BF16 = jnp.bfloat16
HIGHEST = lax.Precision.HIGHEST

D_MODEL = 1024
N_META = 16
CONV_K = 4
EPS = 1e-6
CHUNK = 64
TAIL = 8
LANE = 128
COL_BLK = 512
SSD_INNER = 2048
SSD_P = 64
SSD_H = 32
SSD_G = 4
SSD_N = 128
SSD_GW = SSD_INNER // SSD_G
SSD_CONV = SSD_INNER + 2 * SSD_G * SSD_N
GDN_DK = 128
GDN_DV = 256
GDN_H = 8
GDN_KEY = GDN_H * GDN_DK
GDN_VAL = GDN_H * GDN_DV
GDN_CONV = 2 * GDN_KEY + GDN_VAL
REP = 32
NEG_BIG = -1e30

VMEM_LIMIT = 48 * 1024 * 1024
TM_PROMPT_IN = 256
TM_PROMPT_OUT = 512


NEG_LOG2E = -1.4426950408889634


def _sigmoid(x):
    return 1.0 / (1.0 + jnp.exp2(x * NEG_LOG2E))


def _silu(x):
    return x * _sigmoid(x)


def _softplus(x):
    return jnp.maximum(x, 0.0) + jnp.log1p(jnp.exp(-jnp.abs(x)))


def _dot(a, b, precision=None):
    return jnp.dot(a, b, preferred_element_type=F32, precision=precision)


def _dot_nt(a, b, precision=None):
    return lax.dot_general(a, b, (((1,), (1,)), ((), ())), preferred_element_type=F32, precision=precision)


def _dot_tn(a, b):
    return lax.dot_general(a, b, (((0,), (0,)), ((), ())), preferred_element_type=F32)


def _iota(shape, dim):
    return lax.broadcasted_iota(jnp.int32, shape, dim)


def _split3_merge(x):
    hi = x.astype(BF16)
    r1 = x - hi.astype(F32)
    mid = r1.astype(BF16)
    lo = (r1 - mid.astype(F32)).astype(BF16)
    lane = _iota(x.shape, 1)
    zero = jnp.zeros_like(hi)
    return jnp.where(lane < REP, hi, jnp.where(lane < 2 * REP, mid, jnp.where(lane < 3 * REP, lo, zero)))


def _rmsnorm_bf16(x, w):
    ms = jnp.mean(x * x, axis=-1, keepdims=True)
    return (x * lax.rsqrt(ms + EPS) * w).astype(BF16)


def _conv_block(tail_scr, s, cols, raw, n_valid, cw_ref, bias, tailo_ref):
    seg = raw.shape[0]
    x = [tail_scr[s, :, cols]] + [raw[i * TAIL:(i + 1) * TAIL] for i in range(seg // TAIL)]
    last = raw[n_valid - TAIL:n_valid]
    tail_scr[s, :, cols] = last
    tailo_ref[s, :, cols] = last
    row = _iota((TAIL, raw.shape[1]), 0)

    def shift_down(tiles, d):
        rolled = [pltpu.roll(t, d, 0) for t in tiles]
        return [rolled[0]] + [jnp.where(row < d, rolled[i - 1], rolled[i]) for i in range(1, len(tiles))]

    assert CONV_K == 4
    w = [cw_ref[k:k + 1, cols] for k in range(CONV_K)]
    x1 = shift_down(x, 1)
    near = [t * w[3] + t1 * w[2] for t, t1 in zip(x, x1)]
    far = shift_down([t * w[1] + t1 * w[0] for t, t1 in zip(x, x1)], 2)
    acc = jnp.concatenate([a + b for a, b in zip(near[1:], far[1:])], axis=0)
    if bias is not None:
        acc = acc + bias
    return acc


def _ssd_inproj_kernel(spt, seg, n_valid, x_ref, nw_ref, w_ref, wdt_ref, cw_ref, cb_ref, dtb_ref, tail_ref,
                       zs_ref, u_ref, dt_ref, tailo_ref, tail_scr):
    @pl.when(pl.program_id(1) == 0)
    def _():
        tail_scr[...] = tail_ref[...]

    xn = _rmsnorm_bf16(x_ref[0], nw_ref[...])
    dt_ref[0] = _softplus(_dot(xn, wdt_ref[...]) + dtb_ref[...])

    def conv_dot(j):
        return _dot(xn, w_ref[:, SSD_INNER + j * COL_BLK:SSD_INNER + (j + 1) * COL_BLK])

    n_conv, n_gate = SSD_CONV // COL_BLK, SSD_INNER // COL_BLK
    raw = conv_dot(0)
    for j in range(n_conv):
        cols = slice(j * COL_BLK, (j + 1) * COL_BLK)
        raw_next = conv_dot(j + 1) if j + 1 < n_conv else None
        gate = _dot(xn, w_ref[:, cols]) if j < n_gate else None
        for s in range(spt):
            acc = _conv_block(tail_scr, s, cols, raw[s * seg:(s + 1) * seg], n_valid, cw_ref, cb_ref[:, cols],
                              tailo_ref)
            u_ref[0, s * seg:(s + 1) * seg, cols] = _silu(acc)
        if gate is not None:
            zs_ref[0, :, cols] = _silu(gate)
        raw = raw_next


def _gdn_inproj_kernel(spt, seg, n_valid, x_ref, nw_ref, w_ref, wab_ref, cw_ref, dtb_ref, alog_ref, tail_ref,
                       q_ref, k_ref, v_ref, zs_ref, g_ref, beta_ref, tailo_ref, tail_scr):
    @pl.when(pl.program_id(1) == 0)
    def _():
        tail_scr[...] = tail_ref[...]

    xn = _rmsnorm_bf16(x_ref[0], nw_ref[...])
    ab = _dot(xn, wab_ref[...])
    lane = _iota((1, LANE), 1)
    head_lane = ((lane & (REP - 1)) < GDN_H) & (lane < 3 * REP)
    coef = jnp.where(head_lane, -jnp.exp(alog_ref[...]), 0.0)
    g_ref[0] = coef * _softplus(ab[:, :LANE] + dtb_ref[...])
    beta_ref[0] = _sigmoid(ab[:, LANE:])
    def conv_dot(j):
        return _dot(xn, w_ref[:, j * COL_BLK:(j + 1) * COL_BLK])

    n_conv, n_gate = GDN_CONV // COL_BLK, GDN_VAL // COL_BLK
    raw_next = conv_dot(0)
    for j in range(n_conv):
        cols = slice(j * COL_BLK, (j + 1) * COL_BLK)
        raw = raw_next
        raw_next = conv_dot(j + 1) if j + 1 < n_conv else None
        if j < n_gate:
            gate = _dot(xn, w_ref[:, GDN_CONV + j * COL_BLK:GDN_CONV + (j + 1) * COL_BLK])
            zs_ref[0, :, cols] = _silu(gate)
        for s in range(spt):
            rows = slice(s * seg, (s + 1) * seg)
            u = _silu(_conv_block(tail_scr, s, cols, raw[rows], n_valid, cw_ref, None, tailo_ref))
            if j * COL_BLK < 2 * GDN_KEY:
                is_q = j * COL_BLK < GDN_KEY
                dst = q_ref if is_q else k_ref
                off = j * COL_BLK - (0 if is_q else GDN_KEY)
                for i in range(COL_BLK // GDN_DK):
                    t = u[:, i * GDN_DK:(i + 1) * GDN_DK]
                    t = t * lax.rsqrt(jnp.sum(t * t, axis=-1, keepdims=True) + EPS)
                    if is_q:
                        t = t * (GDN_DK ** -0.5)
                    dst[0, rows, off + i * GDN_DK:off + (i + 1) * GDN_DK] = t
            else:
                off = j * COL_BLK - 2 * GDN_KEY
                v_ref[0, rows, off:off + COL_BLK] = u


def _inproj_call(body, x, tail0, spt, tm, n_valid, weights, consts, out_widths, conv_dim, name):
    n_grp, rows, _ = x.shape
    assert rows % tm == 0 and tm % spt == 0
    seg = tm // spt
    n_tiles = rows // tm
    assert (spt == 1 and n_valid == seg) or n_tiles == 1
    assert n_valid % TAIL == 0 and TAIL <= n_valid <= seg
    shared = tail0.shape[0] == 1
    tok_map = lambda g, t: (g, t, 0)
    tail_map = (lambda g, t: (0, 0, 0)) if shared else (lambda g, t: (g, 0, 0))
    in_specs = [pl.BlockSpec((1, tm, D_MODEL), tok_map), pl.BlockSpec((1, D_MODEL), lambda g, t: (0, 0))]
    args = [x, consts[0]]
    for w, layer in weights:
        if layer is None:
            in_specs.append(pl.BlockSpec(w.shape, lambda g, t: (0, 0), pipeline_mode=pl.Buffered(1)))
        else:
            in_specs.append(pl.BlockSpec((None,) + w.shape[1:], lambda g, t, layer=layer: (layer, 0, 0),
                                         pipeline_mode=pl.Buffered(1)))
        args.append(w)
    for cst in consts[1:]:
        in_specs.append(pl.BlockSpec(cst.shape, lambda g, t: (0, 0)))
        args.append(cst)
    in_specs.append(pl.BlockSpec((spt, TAIL, conv_dim), tail_map))
    args.append(tail0)
    out_specs = [pl.BlockSpec((1, tm, wd), tok_map) for wd in out_widths]
    out_shape = [jax.ShapeDtypeStruct((n_grp, rows, wd), F32) for wd in out_widths]
    out_specs.append(pl.BlockSpec((spt, TAIL, conv_dim), lambda g, t: (g, 0, 0)))
    out_shape.append(jax.ShapeDtypeStruct((n_grp * spt, TAIL, conv_dim), F32))
    return pl.pallas_call(
        functools.partial(body, spt, seg, n_valid),
        grid=(n_grp, n_tiles),
        in_specs=in_specs, out_specs=out_specs, out_shape=out_shape,
        scratch_shapes=[pltpu.VMEM((spt, TAIL, conv_dim), F32)],
        compiler_params=pltpu.CompilerParams(
            dimension_semantics=("parallel", "arbitrary"), vmem_limit_bytes=VMEM_LIMIT),
        name=name,
    )(*args)


def _outproj_kernel(final, y_ref, w_ref, h_ref, *rest):
    if final:
        fw_ref, o_ref = rest
    else:
        (o_ref,) = rest
    h = h_ref[...] + _dot(y_ref[...], w_ref[...])
    if final:
        ms = jnp.mean(h * h, axis=-1, keepdims=True)
        h = h * lax.rsqrt(ms + EPS) * fw_ref[...]
    o_ref[...] = h


def _outproj(y2d, w_stack, layer, h2d, tm, final_w=None):
    m, k = y2d.shape
    assert m % tm == 0
    final = final_w is not None
    in_specs = [
        pl.BlockSpec((tm, k), lambda i: (i, 0)),
        pl.BlockSpec((None, k, D_MODEL), lambda i: (layer, 0, 0), pipeline_mode=pl.Buffered(1)),
        pl.BlockSpec((tm, D_MODEL), lambda i: (i, 0)),
    ]
    args = [y2d, w_stack, h2d]
    if final:
        in_specs.append(pl.BlockSpec((1, D_MODEL), lambda i: (0, 0)))
        args.append(final_w.reshape(1, D_MODEL))
    return pl.pallas_call(
        functools.partial(_outproj_kernel, final),
        grid=(m // tm,),
        in_specs=in_specs,
        out_specs=pl.BlockSpec((tm, D_MODEL), lambda i: (i, 0)),
        out_shape=jax.ShapeDtypeStruct((m, D_MODEL), F32),
        compiler_params=pltpu.CompilerParams(
            dimension_semantics=("parallel",), vmem_limit_bytes=VMEM_LIMIT),
        name="outproj_final" if final else "outproj",
    )(*args)


def _ssd_kernel(nb, shared, n_valid, n_chunks,
                zs_ref, u_ref, dt_ref, s0_ref, alog_ref, dsk_ref, gw_ref, e3_ref, tri_ref, sel_ref,
                cmask_ref, bmask_ref,
                y_ref, sfin_ref, st_scr):
    L = CHUNK
    c = pl.program_id(1)
    seqs = range(nb)
    groups = range(SSD_G)
    gn = [slice(g * SSD_N, (g + 1) * SSD_N) for g in groups]
    gw = [slice(g * SSD_GW, (g + 1) * SSD_GW) for g in groups]

    @pl.when(c == 0)
    def _():
        for b in seqs:
            st_scr[b] = s0_ref[0 if shared else b].T

    xs = [u_ref[b, :, 0:SSD_INNER] for b in seqs]
    bm_b = [u_ref[b, :, SSD_INNER:SSD_INNER + SSD_G * SSD_N].astype(BF16) for b in seqs]
    cm_b = [u_ref[b, :, SSD_INNER + SSD_G * SSD_N:SSD_CONV].astype(BF16) for b in seqs]

    dt4 = [dt_ref[b] for b in seqs]
    if n_valid < L:
        live = _iota((L, LANE), 0) < n_valid
        dt4 = [jnp.where(live, d, 0.0) for d in dt4]
    neg_a = -jnp.exp(alog_ref[...])
    cs4 = [_dot(tri_ref[...], dt4[b] * neg_a, precision=HIGHEST) for b in seqs]
    stack = []
    for b in seqs:
        stack += [cs4[b], dt4[b] * jnp.exp(cs4[b][L - 1:L, :] - cs4[b])]
    ex = _dot(_split3_merge(jnp.concatenate(stack, axis=0)), e3_ref[...])
    cs_x = [ex[(2 * b) * L:(2 * b + 1) * L] for b in seqs]
    dtw_x = [ex[(2 * b + 1) * L:(2 * b + 2) * L] for b in seqs]
    ecs_x = [jnp.exp(c) for c in cs_x]

    lane4 = _iota((L, LANE), 1)
    lo_lanes = lane4 < REP
    hi_lanes = (lane4 >= REP) & (lane4 < 2 * REP)

    def halves(v):
        return [jnp.where(lo_lanes, v, 0.0), jnp.where(hi_lanes, v, 0.0)]

    t2 = [_dot_nt(sel_ref[...], jnp.concatenate(halves(cs4[b]) + halves(dt4[b]), axis=0), precision=HIGHEST)
          for b in seqs]

    def rows_x(t, off):
        return jnp.concatenate(
            [jnp.broadcast_to(t[j:j + 1, off:off + LANE], (L, LANE)) for j in range(SSD_H // 2)], axis=1)

    decay = [jnp.exp(cs_x[b] - rows_x(t2[b], 0) + cmask_ref[...]) * rows_x(t2[b], LANE) for b in seqs]

    cb2 = [[_dot_nt(cm_b[b][:, gn[g]], jnp.concatenate([bm_b[b][:, gn[g]]] * 2, axis=0)) for b in seqs]
           for g in groups]
    m_x = [(jnp.concatenate([cb2[g][b] for g in groups for _ in range(SSD_GW // LANE)], axis=1)
            * decay[b]).astype(BF16) for b in seqs]

    xs_b = [x.astype(BF16) for x in xs]
    blk = 4 * L
    y_parts = [[] for _ in seqs]
    for j in range(SSD_INNER // blk):
        for b in seqs:
            xj = xs_b[b][:, j * blk:(j + 1) * blk]
            rhs = jnp.concatenate([xj] * 4, axis=0) * bmask_ref[...]
            y_parts[b].append(_dot(m_x[b][:, j * blk:(j + 1) * blk], rhs))

    xw_b = [(xs[b] * dtw_x[b]).astype(BF16) for b in seqs]
    off_parts = [[] for _ in seqs]
    for g in groups:
        for b in seqs:
            st_g = st_scr[b, :, gw[g]]
            off_parts[b].append(_dot(cm_b[b][:, gn[g]], st_g.astype(BF16)))
            st_scr[b, :, gw[g]] = (st_g * ecs_x[b][L - 1:L, gw[g]]
                                   + _dot_tn(bm_b[b][:, gn[g]], xw_b[b][:, gw[g]]))
    for b in seqs:
        y = (jnp.concatenate(y_parts[b], axis=1) + jnp.concatenate(off_parts[b], axis=1) * ecs_x[b]
             + xs[b] * dsk_ref[...])
        yg = y * zs_ref[b]
        for g in groups:
            blk_g = yg[:, gw[g]]
            ms = jnp.mean(blk_g * blk_g, axis=-1, keepdims=True)
            y_ref[b, :, gw[g]] = (blk_g * lax.rsqrt(ms + EPS) * gw_ref[:, gw[g]]).astype(BF16)

    @pl.when(c == n_chunks - 1)
    def _():
        for b in seqs:
            sfin_ref[b] = st_scr[b].T


def _ssd_consts():
    k = jnp.arange(LANE)[:, None]
    col = jnp.arange(SSD_INNER)[None, :]
    e3 = ((k % REP == col // SSD_P) & (k < 3 * REP)).astype(BF16)
    t = jnp.arange(CHUNK)
    tri = (t[:, None] >= t[None, :]).astype(F32)
    j = jnp.arange(SSD_H // 2)[:, None]
    kk = jnp.arange(LANE)[None, :]
    sel = ((kk == 2 * j) | (kk == REP + 2 * j + 1)).astype(F32)
    cmask = jnp.where(t[:, None] >= (col % CHUNK), 0.0, NEG_BIG).astype(F32)
    b = jnp.arange(4 * CHUNK)
    bmask = ((b[:, None] // CHUNK) == (b[None, :] // CHUNK)).astype(BF16)
    return e3, tri, sel, cmask, bmask


def _seqs_per_step(n_seq):
    return 2 if n_seq % 2 == 0 else 1


def _ssd_mixer(zs, u, dt, s0, p, n_valid):
    n_seq, t_len, _ = zs.shape
    n_chunks = t_len // CHUNK
    nb = _seqs_per_step(n_seq)
    shared = s0.shape[0] == 1
    init_spec = (pl.BlockSpec((1, SSD_INNER, SSD_N), lambda s, c: (0, 0, 0)) if shared
                 else pl.BlockSpec((nb, SSD_INNER, SSD_N), lambda s, c: (s, 0, 0)))
    tok_map = lambda s, c: (s, c, 0)
    const2 = lambda s, c: (0, 0)
    consts = (p["a_log4"], p["d_x"], p["gnorm_w"]) + _ssd_consts()
    return pl.pallas_call(
        functools.partial(_ssd_kernel, nb, shared, n_valid, n_chunks),
        grid=(n_seq // nb, n_chunks),
        in_specs=[
            pl.BlockSpec((nb, CHUNK, SSD_INNER), tok_map),
            pl.BlockSpec((nb, CHUNK, SSD_CONV), tok_map),
            pl.BlockSpec((nb, CHUNK, LANE), tok_map),
            init_spec,
        ] + [pl.BlockSpec(cst.shape, const2) for cst in consts],
        out_specs=[
            pl.BlockSpec((nb, CHUNK, SSD_INNER), tok_map),
            pl.BlockSpec((nb, SSD_INNER, SSD_N), lambda s, c: (s, 0, 0)),
        ],
        out_shape=[
            jax.ShapeDtypeStruct((n_seq, t_len, SSD_INNER), BF16),
            jax.ShapeDtypeStruct((n_seq, SSD_INNER, SSD_N), F32),
        ],
        scratch_shapes=[pltpu.VMEM((nb, SSD_N, SSD_INNER), F32)],
        compiler_params=pltpu.CompilerParams(
            dimension_semantics=("parallel", "arbitrary"), vmem_limit_bytes=VMEM_LIMIT),
        name="ssd_mixer",
    )(zs, u, dt, s0, *consts)


def _unit_lower_inverses(n_list, eye, same16, same32):
    def mm(a_list, b_list):
        return [_dot(a.astype(BF16), b.astype(BF16)) for a, b in zip(a_list, b_list)]

    def axpy(t_list, d_list, sign):
        return [t + sign * d for t, d in zip(t_list, d_list)]

    nd = [n * same16 for n in n_list]
    t = [eye - x for x in nd]
    pw = mm(nd, nd)
    for step in range(3):
        t = axpy(t, mm(t, pw), 1.0)
        if step < 2:
            pw = mm(pw, pw)
    n1 = [n * (same32 - same16) for n in n_list]
    t = axpy(t, mm(t, mm(n1, t)), -1.0)
    n2 = [n * (1.0 - same32) for n in n_list]
    t = axpy(t, mm(t, mm(n2, t)), -1.0)
    return t


def _gdn_kernel(nb, shared, n_valid, n_chunks,
                q_ref, k_ref, v_ref, zs_ref, g_ref, beta_ref, s0_ref, ow_ref, e3_ref, tri_ref, sel_ref,
                masks_ref,
                y_ref, sfin_ref, st_scr):
    L = CHUNK
    c = pl.program_id(1)
    seqs = range(nb)

    @pl.when(c == 0)
    def _():
        for b in seqs:
            st_scr[b] = s0_ref[0 if shared else b]

    g = [g_ref[b] for b in seqs]
    beta = [beta_ref[b] for b in seqs]
    if n_valid < L:
        live = _iota((L, LANE), 0) < n_valid
        g = [jnp.where(live, x, 0.0) for x in g]
        beta = [jnp.where(live, x, 0.0) for x in beta]
    gc = [_dot(tri_ref[...], g[b], precision=HIGHEST) for b in seqs]
    stack = []
    for b in seqs:
        stack += [gc[b], beta[b]]
    ex = _dot(_split3_merge(jnp.concatenate(stack, axis=0)), e3_ref[...])
    gc_x = [ex[(2 * b) * L:(2 * b + 1) * L] for b in seqs]
    beta_x = [ex[(2 * b + 1) * L:(2 * b + 2) * L] for b in seqs]
    egc_x = [jnp.exp(x) for x in gc_x]
    egl_x = [jnp.exp(x[L - 1:L, :] - x) for x in gc_x]
    gc_t = [_dot_nt(sel_ref[...], gc[b], precision=HIGHEST) for b in seqs]

    incl_add = masks_ref[0]
    strict = masks_ref[1]
    eye = masks_ref[2]
    same16 = masks_ref[3]
    same32 = masks_ref[4]

    chains = [(b, h) for h in range(GDN_H) for b in seqs]
    ks = [slice(h * GDN_DK, (h + 1) * GDN_DK) for h in range(GDN_H)]
    vs = [slice(h * GDN_DV, (h + 1) * GDN_DV) for h in range(GDN_H)]
    q = [q_ref[b, :, ks[h]] for b, h in chains]
    k = [k_ref[b, :, ks[h]] for b, h in chains]
    egc = [egc_x[b][:, ks[h]] for b, h in chains]
    kb = [k[i] * beta_x[b][:, ks[h]] for i, (b, h) in enumerate(chains)]
    dec = [jnp.exp(gc_x[b][:, h * GDN_DK:h * GDN_DK + L] - jnp.broadcast_to(gc_t[b][h:h + 1, :], (L, L)) + incl_add)
           for b, h in chains]
    n_ch = range(len(chains))
    kq = [_dot_nt(jnp.concatenate([kb[i], q[i]], axis=0).astype(BF16), k[i].astype(BF16))
          for i in n_ch]
    t_inv = _unit_lower_inverses([kq[i][0:L] * dec[i] * strict for i in n_ch], eye, same16, same32)
    uw = []
    for i, (b, h) in enumerate(chains):
        beta_h = beta_x[b][:, ks[h]]
        rhs = jnp.concatenate([v_ref[b, :, vs[h]] * jnp.concatenate([beta_h, beta_h], axis=1),
                               kb[i] * egc[i]], axis=1)
        uw.append(_dot(t_inv[i].astype(BF16), rhs.astype(BF16)))
    s_old = [st_scr[b, h] for b, h in chains]
    wq = [_dot(jnp.concatenate([uw[i][:, GDN_DV:], q[i] * egc[i]], axis=0).astype(BF16),
               s_old[i].astype(BF16)) for i in n_ch]
    v_new = [(uw[i][:, :GDN_DV] - wq[i][0:L]).astype(BF16) for i in n_ch]
    o = [wq[i][L:2 * L] + _dot((kq[i][L:2 * L] * dec[i]).astype(BF16), v_new[i]) for i in n_ch]
    for i, (b, h) in enumerate(chains):
        e_last = egc[i][L - 1:L, :]
        st_scr[b, h] = (s_old[i] * jnp.concatenate([e_last, e_last], axis=1)
                        + _dot_tn((k[i] * egl_x[b][:, ks[h]]).astype(BF16), v_new[i]))
    for i, (b, h) in enumerate(chains):
        o_h = o[i] * lax.rsqrt(jnp.mean(o[i] * o[i], axis=-1, keepdims=True) + EPS) * ow_ref[...]
        y_ref[b, :, vs[h]] = (o_h * zs_ref[b, :, vs[h]]).astype(BF16)

    @pl.when(c == n_chunks - 1)
    def _():
        for b in seqs:
            sfin_ref[b] = st_scr[b]


def _gdn_consts():
    k = jnp.arange(LANE)[:, None]
    col = jnp.arange(GDN_KEY)[None, :]
    e3 = ((k % REP == col // GDN_DK) & (k < 3 * REP)).astype(BF16)
    t = jnp.arange(CHUNK)
    r, cc = t[:, None], t[None, :]
    tri = (r >= cc).astype(F32)
    sel = (jnp.arange(LANE)[None, :] == jnp.arange(GDN_H)[:, None]).astype(F32)
    masks = jnp.stack([
        jnp.where(r >= cc, 0.0, NEG_BIG),
        (r > cc).astype(F32),
        (r == cc).astype(F32),
        ((r // (CHUNK // 4)) == (cc // (CHUNK // 4))).astype(F32),
        ((r // (CHUNK // 2)) == (cc // (CHUNK // 2))).astype(F32),
    ]).astype(F32)
    return e3, tri, sel, masks


def _gdn_mixer(q, k, v, zs, g, beta, s0, p, n_valid):
    n_seq, t_len, _ = zs.shape
    n_chunks = t_len // CHUNK
    nb = _seqs_per_step(n_seq)
    shared = s0.shape[0] == 1
    init_spec = (pl.BlockSpec((1, GDN_H, GDN_DK, GDN_DV), lambda s, c: (0, 0, 0, 0)) if shared
                 else pl.BlockSpec((nb, GDN_H, GDN_DK, GDN_DV), lambda s, c: (s, 0, 0, 0)))
    tok_map = lambda s, c: (s, c, 0)
    e3, tri, sel, masks = _gdn_consts()
    consts2 = (p["onorm_w"], e3, tri, sel)
    return pl.pallas_call(
        functools.partial(_gdn_kernel, nb, shared, n_valid, n_chunks),
        grid=(n_seq // nb, n_chunks),
        in_specs=[
            pl.BlockSpec((nb, CHUNK, GDN_KEY), tok_map),
            pl.BlockSpec((nb, CHUNK, GDN_KEY), tok_map),
            pl.BlockSpec((nb, CHUNK, GDN_VAL), tok_map),
            pl.BlockSpec((nb, CHUNK, GDN_VAL), tok_map),
            pl.BlockSpec((nb, CHUNK, LANE), tok_map),
            pl.BlockSpec((nb, CHUNK, LANE), tok_map),
            init_spec,
        ] + [pl.BlockSpec(cst.shape, lambda s, c: (0, 0)) for cst in consts2]
          + [pl.BlockSpec(masks.shape, lambda s, c: (0, 0, 0))],
        out_specs=[
            pl.BlockSpec((nb, CHUNK, GDN_VAL), tok_map),
            pl.BlockSpec((nb, GDN_H, GDN_DK, GDN_DV), lambda s, c: (s, 0, 0, 0)),
        ],
        out_shape=[
            jax.ShapeDtypeStruct((n_seq, t_len, GDN_VAL), BF16),
            jax.ShapeDtypeStruct((n_seq, GDN_H, GDN_DK, GDN_DV), F32),
        ],
        scratch_shapes=[pltpu.VMEM((nb, GDN_H, GDN_DK, GDN_DV), F32)],
        compiler_params=pltpu.CompilerParams(
            dimension_semantics=("parallel", "arbitrary"), vmem_limit_bytes=VMEM_LIMIT),
        name="gdn_mixer",
    )(q, k, v, zs, g, beta, s0, *consts2, masks)


def _rep_lanes(v, n_heads):
    n_rep = LANE // REP if n_heads == REP else 3
    row = jnp.pad(v, (0, REP - n_heads))
    return jnp.pad(jnp.tile(row, n_rep), (0, LANE - n_rep * REP)).reshape(1, LANE)


def _rep_cols(w, n_heads):
    n_rep = LANE // REP if n_heads == REP else 3
    blk = jnp.pad(w, ((0, 0), (0, REP - n_heads)))
    return jnp.pad(jnp.tile(blk, (1, n_rep)), ((0, 0), (0, LANE - n_rep * REP)))


def _ssd_state_in(s):
    return s.reshape(s.shape[0], SSD_INNER, SSD_N)


def _ssd_state_out(s):
    return s.reshape(s.shape[0], SSD_H, SSD_P, SSD_N)


def _tail_from_rows(rows3):
    return jnp.pad(rows3, ((0, 0), (TAIL - (CONV_K - 1), 0), (0, 0)))


def _segment_valid(n_valid, tm_in, spt):
    return tm_in // spt if n_valid == CHUNK else n_valid


def _ssd_layer(h, tail0, s0, p, layer, spt, tm_in, tm_out, n_valid, final_w=None):
    n_grp, rows, _ = h.shape
    weights = [(p["w_in"], layer), (p["w_dt4"][layer], None)]
    consts = [p["norm_w"][layer].reshape(1, D_MODEL), p["conv_w"][layer], p["conv_b"][layer].reshape(1, SSD_CONV),
              p["dt_bias4"][layer]]
    zs, u, dt, tails = _inproj_call(_ssd_inproj_kernel, h, tail0, spt, tm_in, _segment_valid(n_valid, tm_in, spt),
                                    weights, consts, (SSD_INNER, SSD_CONV, LANE), SSD_CONV, "ssd_inproj")
    seq = lambda a: a.reshape(n_grp * spt, rows // spt, a.shape[-1])
    mp = dict(a_log4=p["a_log4"][layer], d_x=p["d_x"][layer], gnorm_w=p["gnorm_w"][layer])
    y, s_fin = _ssd_mixer(seq(zs), seq(u), seq(dt), s0, mp, n_valid)
    h_new = _outproj(y.reshape(n_grp * rows, SSD_INNER), p["w_out"], layer, h.reshape(n_grp * rows, D_MODEL),
                     tm_out, final_w)
    return h_new.reshape(h.shape), tails, s_fin


def _gdn_layer(h, tail0, s0, p, layer, spt, tm_in, tm_out, n_valid, final_w=None):
    n_grp, rows, _ = h.shape
    weights = [(p["w_in"], layer), (p["w_ab"][layer], None)]
    consts = [p["norm_w"][layer].reshape(1, D_MODEL), p["conv_w"][layer], p["dt_bias"][layer], p["a_log"][layer]]
    q, k, v, zs, g, beta, tails = _inproj_call(
        _gdn_inproj_kernel, h, tail0, spt, tm_in, _segment_valid(n_valid, tm_in, spt), weights, consts,
        (GDN_KEY, GDN_KEY, GDN_VAL, GDN_VAL, LANE, LANE), GDN_CONV, "gdn_inproj")
    seq = lambda a: a.reshape(n_grp * spt, rows // spt, a.shape[-1])
    mp = dict(onorm_w=p["onorm_w"][layer].reshape(1, GDN_DV))
    y, s_fin = _gdn_mixer(seq(q), seq(k), seq(v), seq(zs), seq(g), seq(beta), s0, mp, n_valid)
    h_new = _outproj(y.reshape(n_grp * rows, GDN_VAL), p["w_out"], layer, h.reshape(n_grp * rows, D_MODEL),
                     tm_out, final_w)
    return h_new.reshape(h.shape), tails, s_fin


def kernel(x_prompt, x_sample, state_ssd, state_ssd_conv, state_gdn, state_gdn_conv, meta_tokens,
           ssd_norm_w, ssd_w_in, ssd_conv_w, ssd_conv_b, ssd_dt_bias, ssd_a_log, ssd_d, ssd_gnorm_w, ssd_w_out,
           gdn_norm_w, gdn_w_in, gdn_conv_w, gdn_dt_bias, gdn_a_log, gdn_onorm_w, gdn_w_out, final_norm_w):
    n_ssd, n_gdn = ssd_norm_w.shape[0], gdn_norm_w.shape[0]
    depth = n_ssd + n_gdn
    n_dec, dec_t, _ = x_sample.shape
    assert dec_t == N_META and N_META <= CHUNK and x_prompt.shape[1] % TM_PROMPT_OUT == 0
    keep = CONV_K - 1

    ssd_p = dict(
        norm_w=ssd_norm_w, w_in=ssd_w_in.astype(BF16), w_out=ssd_w_out.astype(BF16),
        w_dt4=[_rep_cols(ssd_w_in[j][:, SSD_INNER + SSD_CONV:], SSD_H).astype(BF16) for j in range(n_ssd)],
        conv_w=ssd_conv_w, conv_b=ssd_conv_b,
        dt_bias4=[_rep_lanes(ssd_dt_bias[j], SSD_H) for j in range(n_ssd)],
        a_log4=[_rep_lanes(ssd_a_log[j], SSD_H) for j in range(n_ssd)],
        d_x=[jnp.repeat(ssd_d[j], SSD_P).reshape(1, SSD_INNER) for j in range(n_ssd)],
        gnorm_w=[ssd_gnorm_w[j].reshape(1, SSD_INNER) for j in range(n_ssd)],
    )
    ab0 = GDN_CONV + GDN_VAL
    gdn_p = dict(
        norm_w=gdn_norm_w, w_in=gdn_w_in.astype(BF16), w_out=gdn_w_out.astype(BF16),
        w_ab=[jnp.concatenate([_rep_cols(gdn_w_in[j][:, ab0:ab0 + GDN_H], GDN_H),
                               _rep_cols(gdn_w_in[j][:, ab0 + GDN_H:], GDN_H)], axis=1).astype(BF16)
              for j in range(n_gdn)],
        conv_w=gdn_conv_w,
        dt_bias=[_rep_lanes(gdn_dt_bias[j], GDN_H) for j in range(n_gdn)],
        a_log=[_rep_lanes(gdn_a_log[j], GDN_H) for j in range(n_gdn)],
        onorm_w=gdn_onorm_w,
    )

    n_small = n_dec + 1
    hs = jnp.concatenate([x_sample, meta_tokens.astype(x_sample.dtype)[None]], axis=0)
    hs = jnp.pad(hs, ((0, 0), (0, CHUNK - N_META), (0, 0))).reshape(1, n_small * CHUNK, D_MODEL)
    hp = x_prompt
    tm_small = n_small * CHUNK
    last_rows = lambda tails: tails[:, TAIL - keep:]
    outs = {k: [] for k in ("p_ssd", "p_ssd_conv", "p_gdn", "p_gdn_conv", "s_ssd", "s_ssd_conv", "s_gdn", "s_gdn_conv")}
    for i in range(depth):
        j = i // 2
        fw = final_norm_w if i == depth - 1 else None
        if i % 2 == 0:
            tail_s = _tail_from_rows(jnp.concatenate([state_ssd_conv[j], jnp.zeros((1, keep, SSD_CONV), F32)], axis=0))
            s0_s = _ssd_state_in(jnp.concatenate([state_ssd[j], jnp.zeros((1,) + state_ssd.shape[2:], F32)], axis=0))
            hs, tails_s, sfin_s = _ssd_layer(hs, tail_s, s0_s, ssd_p, j, n_small, tm_small, tm_small, N_META, fw)
            hp, tails_p, sfin_p = _ssd_layer(hp, tails_s[n_dec:], sfin_s[n_dec:], ssd_p, j, 1, TM_PROMPT_IN,
                                             TM_PROMPT_OUT, CHUNK, fw)
            outs["s_ssd"].append(_ssd_state_out(sfin_s[:n_dec]))
            outs["p_ssd"].append(_ssd_state_out(sfin_p))
            outs["s_ssd_conv"].append(last_rows(tails_s[:n_dec]))
            outs["p_ssd_conv"].append(last_rows(tails_p))
        else:
            tail_s = _tail_from_rows(jnp.concatenate([state_gdn_conv[j], jnp.zeros((1, keep, GDN_CONV), F32)], axis=0))
            s0_s = jnp.concatenate([state_gdn[j], jnp.zeros((1,) + state_gdn.shape[2:], F32)], axis=0)
            hs, tails_s, sfin_s = _gdn_layer(hs, tail_s, s0_s, gdn_p, j, n_small, tm_small, tm_small, N_META, fw)
            hp, tails_p, sfin_p = _gdn_layer(hp, tails_s[n_dec:], sfin_s[n_dec:], gdn_p, j, 1, TM_PROMPT_IN,
                                             TM_PROMPT_OUT, CHUNK, fw)
            outs["s_gdn"].append(sfin_s[:n_dec])
            outs["p_gdn"].append(sfin_p)
            outs["s_gdn_conv"].append(last_rows(tails_s[:n_dec]))
            outs["p_gdn_conv"].append(last_rows(tails_p))
    y_sample = hs.reshape(n_small, CHUNK, D_MODEL)[:n_dec, :N_META]
    st = lambda key: jnp.stack(outs[key])
    return (hp, y_sample, st("p_ssd"), st("p_ssd_conv"), st("p_gdn"), st("p_gdn_conv"),
            st("s_ssd"), st("s_ssd_conv"), st("s_gdn"), st("s_gdn_conv"))
```

```python
import functools

import jax
import jax.numpy as jnp
from jax import lax
from jax.experimental import pallas as pl
from jax.experimental.pallas import tpu as pltpu

F32 = jnp.float32
BF16 = jnp.bfloat16
HIGHEST = lax.Precision.HIGHEST

D_MODEL = 1024
N_META = 16
CONV_K = 4
EPS = 1e-6
CHUNK = 64
TAIL = 8
LANE = 128
COL_BLK = 512
SSD_INNER = 2048
SSD_P = 64
SSD_H = 32
SSD_G = 4
SSD_N = 128
SSD_GW = SSD_INNER // SSD_G
SSD_CONV = SSD_INNER + 2 * SSD_G * SSD_N
GDN_DK = 128
GDN_DV = 256
GDN_H = 8
GDN_KEY = GDN_H * GDN_DK
GDN_VAL = GDN_H * GDN_DV
GDN_CONV = 2 * GDN_KEY + GDN_VAL
REP = 32
NEG_BIG = -1e30

VMEM_LIMIT = 48 * 1024 * 1024
TM_PROMPT_IN = 256
TM_PROMPT_OUT = 512
GDN_CHUNKS_PER_STEP = 2


NEG_LOG2E = -1.4426950408889634


def _sigmoid(x):
    return 1.0 / (1.0 + jnp.exp2(x * NEG_LOG2E))


def _silu(x):
    return x * _sigmoid(x)


def _softplus(x):
    return jnp.maximum(x, 0.0) + jnp.log1p(jnp.exp(-jnp.abs(x)))


def _dot(a, b, precision=None):
    return jnp.dot(a, b, preferred_element_type=F32, precision=precision)


def _dot_nt(a, b, precision=None):
    return lax.dot_general(a, b, (((1,), (1,)), ((), ())), preferred_element_type=F32, precision=precision)


def _dot_tn(a, b):
    return lax.dot_general(a, b, (((0,), (0,)), ((), ())), preferred_element_type=F32)


def _iota(shape, dim):
    return lax.broadcasted_iota(jnp.int32, shape, dim)


def _split3_merge(x):
    hi = x.astype(BF16)
    r1 = x - hi.astype(F32)
    mid = r1.astype(BF16)
    lo = (r1 - mid.astype(F32)).astype(BF16)
    lane = _iota(x.shape, 1)
    zero = jnp.zeros_like(hi)
    return jnp.where(lane < REP, hi, jnp.where(lane < 2 * REP, mid, jnp.where(lane < 3 * REP, lo, zero)))


def _rmsnorm_bf16(x, w):
    ms = jnp.mean(x * x, axis=-1, keepdims=True)
    return (x * lax.rsqrt(ms + EPS) * w).astype(BF16)


def _conv_block(tail_scr, s, cols, raw, n_valid, cw_ref, bias, tailo_ref):
    seg = raw.shape[0]
    x = [tail_scr[s, :, cols]] + [raw[i * TAIL:(i + 1) * TAIL] for i in range(seg // TAIL)]
    last = raw[n_valid - TAIL:n_valid]
    tail_scr[s, :, cols] = last
    tailo_ref[s, :, cols] = last
    row = _iota((TAIL, raw.shape[1]), 0)

    def shift_down(tiles, d):
        rolled = [pltpu.roll(t, d, 0) for t in tiles]
        return [rolled[0]] + [jnp.where(row < d, rolled[i - 1], rolled[i]) for i in range(1, len(tiles))]

    assert CONV_K == 4
    w = [cw_ref[k:k + 1, cols] for k in range(CONV_K)]
    x1 = shift_down(x, 1)
    near = [t * w[3] + t1 * w[2] for t, t1 in zip(x, x1)]
    far = shift_down([t * w[1] + t1 * w[0] for t, t1 in zip(x, x1)], 2)
    acc = jnp.concatenate([a + b for a, b in zip(near[1:], far[1:])], axis=0)
    if bias is not None:
        acc = acc + bias
    return acc


def _ssd_inproj_kernel(spt, seg, n_valid, x_ref, nw_ref, w_ref, wdt_ref, cw_ref, cb_ref, dtb_ref, tail_ref,
                       zs_ref, u_ref, dt_ref, tailo_ref, tail_scr):
    @pl.when(pl.program_id(1) == 0)
    def _():
        tail_scr[...] = tail_ref[...]

    xn = _rmsnorm_bf16(x_ref[0], nw_ref[...])
    dt_ref[0] = _softplus(_dot(xn, wdt_ref[...]) + dtb_ref[...])

    def conv_dot(j):
        return _dot(xn, w_ref[:, SSD_INNER + j * COL_BLK:SSD_INNER + (j + 1) * COL_BLK])

    n_conv, n_gate = SSD_CONV // COL_BLK, SSD_INNER // COL_BLK
    raw = conv_dot(0)
    for j in range(n_conv):
        cols = slice(j * COL_BLK, (j + 1) * COL_BLK)
        raw_next = conv_dot(j + 1) if j + 1 < n_conv else None
        gate = _dot(xn, w_ref[:, cols]) if j < n_gate else None
        for s in range(spt):
            acc = _conv_block(tail_scr, s, cols, raw[s * seg:(s + 1) * seg], n_valid, cw_ref, cb_ref[:, cols],
                              tailo_ref)
            u_ref[0, s * seg:(s + 1) * seg, cols] = _silu(acc)
        if gate is not None:
            zs_ref[0, :, cols] = _silu(gate)
        raw = raw_next


def _gdn_inproj_kernel(spt, seg, n_valid, x_ref, nw_ref, w_ref, wab_ref, cw_ref, dtb_ref, alog_ref, tail_ref,
                       q_ref, k_ref, v_ref, zs_ref, g_ref, beta_ref, tailo_ref, tail_scr):
    @pl.when(pl.program_id(1) == 0)
    def _():
        tail_scr[...] = tail_ref[...]

    xn = _rmsnorm_bf16(x_ref[0], nw_ref[...])
    ab = _dot(xn, wab_ref[...])
    lane = _iota((1, LANE), 1)
    head_lane = ((lane & (REP - 1)) < GDN_H) & (lane < 3 * REP)
    coef = jnp.where(head_lane, -jnp.exp(alog_ref[...]), 0.0)
    g_ref[0] = coef * _softplus(ab[:, :LANE] + dtb_ref[...])
    beta_ref[0] = _sigmoid(ab[:, LANE:])
    def conv_dot(j):
        return _dot(xn, w_ref[:, j * COL_BLK:(j + 1) * COL_BLK])

    n_conv, n_gate = GDN_CONV // COL_BLK, GDN_VAL // COL_BLK
    raw_next = conv_dot(0)
    for j in range(n_conv):
        cols = slice(j * COL_BLK, (j + 1) * COL_BLK)
        raw = raw_next
        raw_next = conv_dot(j + 1) if j + 1 < n_conv else None
        if j < n_gate:
            gate = _dot(xn, w_ref[:, GDN_CONV + j * COL_BLK:GDN_CONV + (j + 1) * COL_BLK])
            zs_ref[0, :, cols] = _silu(gate)
        for s in range(spt):
            rows = slice(s * seg, (s + 1) * seg)
            u = _silu(_conv_block(tail_scr, s, cols, raw[rows], n_valid, cw_ref, None, tailo_ref))
            if j * COL_BLK < 2 * GDN_KEY:
                is_q = j * COL_BLK < GDN_KEY
                dst = q_ref if is_q else k_ref
                off = j * COL_BLK - (0 if is_q else GDN_KEY)
                for i in range(COL_BLK // GDN_DK):
                    t = u[:, i * GDN_DK:(i + 1) * GDN_DK]
                    t = t * lax.rsqrt(jnp.sum(t * t, axis=-1, keepdims=True) + EPS)
                    if is_q:
                        t = t * (GDN_DK ** -0.5)
                    dst[0, rows, off + i * GDN_DK:off + (i + 1) * GDN_DK] = t
            else:
                off = j * COL_BLK - 2 * GDN_KEY
                v_ref[0, rows, off:off + COL_BLK] = u


def _inproj_call(body, x, tail0, spt, tm, n_valid, weights, consts, out_widths, conv_dim, name):
    n_grp, rows, _ = x.shape
    assert rows % tm == 0 and tm % spt == 0
    seg = tm // spt
    n_tiles = rows // tm
    assert (spt == 1 and n_valid == seg) or n_tiles == 1
    assert n_valid % TAIL == 0 and TAIL <= n_valid <= seg
    shared = tail0.shape[0] == 1
    tok_map = lambda g, t: (g, t, 0)
    tail_map = (lambda g, t: (0, 0, 0)) if shared else (lambda g, t: (g, 0, 0))
    in_specs = [pl.BlockSpec((1, tm, D_MODEL), tok_map), pl.BlockSpec((1, D_MODEL), lambda g, t: (0, 0))]
    args = [x, consts[0]]
    for w, layer in weights:
        if layer is None:
            in_specs.append(pl.BlockSpec(w.shape, lambda g, t: (0, 0), pipeline_mode=pl.Buffered(1)))
        else:
            in_specs.append(pl.BlockSpec((None,) + w.shape[1:], lambda g, t, layer=layer: (layer, 0, 0),
                                         pipeline_mode=pl.Buffered(1)))
        args.append(w)
    for cst in consts[1:]:
        in_specs.append(pl.BlockSpec(cst.shape, lambda g, t: (0, 0)))
        args.append(cst)
    in_specs.append(pl.BlockSpec((spt, TAIL, conv_dim), tail_map))
    args.append(tail0)
    out_specs = [pl.BlockSpec((1, tm, wd), tok_map) for wd in out_widths]
    out_shape = [jax.ShapeDtypeStruct((n_grp, rows, wd), F32) for wd in out_widths]
    out_specs.append(pl.BlockSpec((spt, TAIL, conv_dim), lambda g, t: (g, 0, 0)))
    out_shape.append(jax.ShapeDtypeStruct((n_grp * spt, TAIL, conv_dim), F32))
    return pl.pallas_call(
        functools.partial(body, spt, seg, n_valid),
        grid=(n_grp, n_tiles),
        in_specs=in_specs, out_specs=out_specs, out_shape=out_shape,
        scratch_shapes=[pltpu.VMEM((spt, TAIL, conv_dim), F32)],
        compiler_params=pltpu.CompilerParams(
            dimension_semantics=("parallel", "arbitrary"), vmem_limit_bytes=VMEM_LIMIT),
        name=name,
    )(*args)


def _outproj_kernel(final, y_ref, w_ref, h_ref, *rest):
    if final:
        fw_ref, o_ref = rest
    else:
        (o_ref,) = rest
    h = h_ref[...] + _dot(y_ref[...], w_ref[...])
    if final:
        ms = jnp.mean(h * h, axis=-1, keepdims=True)
        h = h * lax.rsqrt(ms + EPS) * fw_ref[...]
    o_ref[...] = h


def _outproj(y2d, w_stack, layer, h2d, tm, final_w=None):
    m, k = y2d.shape
    assert m % tm == 0
    final = final_w is not None
    in_specs = [
        pl.BlockSpec((tm, k), lambda i: (i, 0)),
        pl.BlockSpec((None, k, D_MODEL), lambda i: (layer, 0, 0), pipeline_mode=pl.Buffered(1)),
        pl.BlockSpec((tm, D_MODEL), lambda i: (i, 0)),
    ]
    args = [y2d, w_stack, h2d]
    if final:
        in_specs.append(pl.BlockSpec((1, D_MODEL), lambda i: (0, 0)))
        args.append(final_w.reshape(1, D_MODEL))
    return pl.pallas_call(
        functools.partial(_outproj_kernel, final),
        grid=(m // tm,),
        in_specs=in_specs,
        out_specs=pl.BlockSpec((tm, D_MODEL), lambda i: (i, 0)),
        out_shape=jax.ShapeDtypeStruct((m, D_MODEL), F32),
        compiler_params=pltpu.CompilerParams(
            dimension_semantics=("parallel",), vmem_limit_bytes=VMEM_LIMIT),
        name="outproj_final" if final else "outproj",
    )(*args)


def _ssd_kernel(nb, shared, n_valid, n_chunks,
                zs_ref, u_ref, dt_ref, s0_ref, alog_ref, dsk_ref, gw_ref, e3_ref, tri_ref, sel_ref,
                cmask_ref, bmask_ref,
                y_ref, sfin_ref, st_scr):
    L = CHUNK
    c = pl.program_id(1)
    seqs = range(nb)
    groups = range(SSD_G)
    gn = [slice(g * SSD_N, (g + 1) * SSD_N) for g in groups]
    gw = [slice(g * SSD_GW, (g + 1) * SSD_GW) for g in groups]

    @pl.when(c == 0)
    def _():
        for b in seqs:
            st_scr[b] = s0_ref[0 if shared else b].T

    bm_b = [u_ref[b, :, SSD_INNER:SSD_INNER + SSD_G * SSD_N].astype(BF16) for b in seqs]
    cm_b = [u_ref[b, :, SSD_INNER + SSD_G * SSD_N:SSD_CONV].astype(BF16) for b in seqs]

    dt4 = [dt_ref[b] for b in seqs]
    if n_valid < L:
        live = _iota((L, LANE), 0) < n_valid
        dt4 = [jnp.where(live, d, 0.0) for d in dt4]
    neg_a = -jnp.exp(alog_ref[...])
    cs4 = [_dot(tri_ref[...], dt4[b] * neg_a, precision=HIGHEST) for b in seqs]
    stack3 = [_split3_merge(jnp.concatenate([cs4[b], dt4[b] * jnp.exp(cs4[b][L - 1:L, :] - cs4[b])], axis=0))
              for b in seqs]

    lane4 = _iota((L, LANE), 1)
    lo_lanes = lane4 < REP
    hi_lanes = (lane4 >= REP) & (lane4 < 2 * REP)

    def halves(v):
        return [jnp.where(lo_lanes, v, 0.0), jnp.where(hi_lanes, v, 0.0)]

    t2 = [_dot_nt(sel_ref[...], jnp.concatenate(halves(cs4[b]) + halves(dt4[b]), axis=0), precision=HIGHEST)
          for b in seqs]
    cb2 = [[_dot_nt(cm_b[b][:, gn[g]], jnp.concatenate([bm_b[b][:, gn[g]]] * 2, axis=0)) for b in seqs]
           for g in groups]

    slab = 4 * L
    n_slab = SSD_INNER // slab
    per_group = SSD_GW // slab

    def expand(j):
        return [_dot(stack3[b], e3_ref[:, j * slab:(j + 1) * slab]) for b in seqs]

    ex_next = expand(0)
    gated = [[] for _ in seqs]
    for j in range(n_slab):
        g = j // per_group
        lanes = slice(j * slab, (j + 1) * slab)
        ex = ex_next
        if j + 1 < n_slab:
            ex_next = expand(j + 1)
        st_old = [st_scr[b, :, lanes] for b in seqs]
        y_off = [_dot(cm_b[b][:, gn[g]], st_old[b].astype(BF16)) for b in seqs]
        xs, y_in, ecs, dtw = [], [], [], []
        for b in seqs:
            cs_x, dtw_x = ex[b][0:L], ex[b][L:2 * L]
            rows = [jnp.concatenate([jnp.broadcast_to(t2[b][2 * j + i:2 * j + i + 1, off:off + LANE], (L, LANE))
                                     for i in range(slab // LANE)], axis=1) for off in (0, LANE)]
            decay = jnp.exp(cs_x - rows[0] + cmask_ref[...]) * rows[1]
            m = (jnp.concatenate([cb2[g][b]] * (slab // LANE), axis=1) * decay).astype(BF16)
            x = u_ref[b, :, lanes]
            rhs = jnp.concatenate([x.astype(BF16)] * 4, axis=0) * bmask_ref[...]
            y_in.append(_dot(m, rhs))
            xs.append(x)
            ecs.append(jnp.exp(cs_x))
            dtw.append(dtw_x)
        for b in seqs:
            st_scr[b, :, lanes] = (st_old[b] * ecs[b][L - 1:L, :]
                                   + _dot_tn(bm_b[b][:, gn[g]], (xs[b] * dtw[b]).astype(BF16)))
            y = y_in[b] + y_off[b] * ecs[b] + xs[b] * dsk_ref[:, lanes]
            gated[b].append(y * zs_ref[b, :, lanes])
        if (j + 1) % per_group == 0:
            for b in seqs:
                blk_g = jnp.concatenate(gated[b], axis=1)
                gated[b] = []
                ms = jnp.mean(blk_g * blk_g, axis=-1, keepdims=True)
                y_ref[b, :, gw[g]] = (blk_g * lax.rsqrt(ms + EPS) * gw_ref[:, gw[g]]).astype(BF16)

    @pl.when(c == n_chunks - 1)
    def _():
        for b in seqs:
            sfin_ref[b] = st_scr[b].T


def _ssd_consts():
    k = jnp.arange(LANE)[:, None]
    col = jnp.arange(SSD_INNER)[None, :]
    e3 = ((k % REP == col // SSD_P) & (k < 3 * REP)).astype(BF16)
    t = jnp.arange(CHUNK)
    tri = (t[:, None] >= t[None, :]).astype(F32)
    j = jnp.arange(SSD_H // 2)[:, None]
    kk = jnp.arange(LANE)[None, :]
    sel = ((kk == 2 * j) | (kk == REP + 2 * j + 1)).astype(F32)
    b = jnp.arange(4 * CHUNK)
    cmask = jnp.where(t[:, None] >= (b[None, :] % CHUNK), 0.0, NEG_BIG).astype(F32)
    bmask = ((b[:, None] // CHUNK) == (b[None, :] // CHUNK)).astype(BF16)
    return e3, tri, sel, cmask, bmask


def _seqs_per_step(n_seq):
    return 2 if n_seq % 2 == 0 else (3 if n_seq % 3 == 0 else 1)


def _ssd_mixer(zs, u, dt, s0, p, n_valid):
    n_seq, t_len, _ = zs.shape
    n_chunks = t_len // CHUNK
    nb = _seqs_per_step(n_seq)
    shared = s0.shape[0] == 1
    init_spec = (pl.BlockSpec((1, SSD_INNER, SSD_N), lambda s, c: (0, 0, 0)) if shared
                 else pl.BlockSpec((nb, SSD_INNER, SSD_N), lambda s, c: (s, 0, 0)))
    tok_map = lambda s, c: (s, c, 0)
    const2 = lambda s, c: (0, 0)
    consts = (p["a_log4"], p["d_x"], p["gnorm_w"]) + _ssd_consts()
    return pl.pallas_call(
        functools.partial(_ssd_kernel, nb, shared, n_valid, n_chunks),
        grid=(n_seq // nb, n_chunks),
        in_specs=[
            pl.BlockSpec((nb, CHUNK, SSD_INNER), tok_map),
            pl.BlockSpec((nb, CHUNK, SSD_CONV), tok_map),
            pl.BlockSpec((nb, CHUNK, LANE), tok_map),
            init_spec,
        ] + [pl.BlockSpec(cst.shape, const2) for cst in consts],
        out_specs=[
            pl.BlockSpec((nb, CHUNK, SSD_INNER), tok_map),
            pl.BlockSpec((nb, SSD_INNER, SSD_N), lambda s, c: (s, 0, 0)),
        ],
        out_shape=[
            jax.ShapeDtypeStruct((n_seq, t_len, SSD_INNER), BF16),
            jax.ShapeDtypeStruct((n_seq, SSD_INNER, SSD_N), F32),
        ],
        scratch_shapes=[pltpu.VMEM((nb, SSD_N, SSD_INNER), F32)],
        compiler_params=pltpu.CompilerParams(
            dimension_semantics=("parallel", "arbitrary"), vmem_limit_bytes=VMEM_LIMIT),
        name="ssd_mixer",
    )(zs, u, dt, s0, *consts)


def _pair_blockdiag(y2, bd):
    y16 = y2.astype(BF16)
    return jnp.concatenate([y16, y16], axis=0) * bd


def _unit_lower_inverses(n_list, eye, same16, same32, bd):
    def mm(a_list, b_list):
        return [_dot(a.astype(BF16), _pair_blockdiag(b, bd)) for a, b in zip(a_list, b_list)]

    def axpy(t_list, d_list, sign):
        return [t + sign * d for t, d in zip(t_list, d_list)]

    nd = [n * same16 for n in n_list]
    t = [eye - x for x in nd]
    pw = mm(nd, nd)
    for step in range(3):
        t = axpy(t, mm(t, pw), 1.0)
        if step < 2:
            pw = mm(pw, pw)
    n1 = [n * (same32 - same16) for n in n_list]
    t = axpy(t, mm(t, mm(n1, t)), -1.0)
    n2 = [n * (1.0 - same32) for n in n_list]
    t = axpy(t, mm(t, mm(n2, t)), -1.0)
    return t


def _gdn_kernel(nb, nch, shared, n_valid, n_steps,
                q_ref, k_ref, v_ref, zs_ref, g_ref, beta_ref, s0_ref, ow_ref, e3_ref, tri_ref, sel_ref, bd_ref,
                masks_ref,
                y_ref, sfin_ref, st_scr):
    L = CHUNK
    c = pl.program_id(1)
    seqs = range(nb * nch)

    def tok(ref, b, cols):
        return ref[b % nb, (b // nb) * L:(b // nb + 1) * L, cols]

    @pl.when(c == 0)
    def _():
        for s in range(nb):
            st_scr[s] = s0_ref[0 if shared else s]

    g = [tok(g_ref, b, slice(None)) for b in seqs]
    beta = [tok(beta_ref, b, slice(None)) for b in seqs]
    if n_valid < L:
        live = _iota((L, LANE), 0) < n_valid
        g = [jnp.where(live, x, 0.0) for x in g]
        beta = [jnp.where(live, x, 0.0) for x in beta]
    gc = [_dot(tri_ref[...], g[b], precision=HIGHEST) for b in seqs]
    stack = []
    for b in seqs:
        stack += [gc[b], beta[b]]
    ex = _dot(_split3_merge(jnp.concatenate(stack, axis=0)), e3_ref[...])
    gc_x = [ex[(2 * b) * L:(2 * b + 1) * L] for b in seqs]
    beta_x = [ex[(2 * b + 1) * L:(2 * b + 2) * L] for b in seqs]
    egc_x = [jnp.exp(x) for x in gc_x]
    egl_x = [jnp.exp(x[L - 1:L, :] - x) for x in gc_x]
    lane = _iota((L, LANE), 1)
    gc_t2 = [_dot_nt(sel_ref[...], jnp.concatenate([jnp.where(lane < REP, gc[b], 0.0),
                                                    jnp.where((lane >= REP) & (lane < 2 * REP), gc[b], 0.0)],
                                                   axis=0), precision=HIGHEST) for b in seqs]

    incl_add = masks_ref[0]
    strict = masks_ref[1]
    eye = masks_ref[2]
    same16 = masks_ref[3]
    same32 = masks_ref[4]
    bd = bd_ref[...]
    first = lane < L

    chains = [(b, h) for h in range(GDN_H) for b in seqs]
    pairs = [(b, p) for p in range(GDN_H // 2) for b in seqs]
    chain_of = lambda b, h: h * len(seqs) + b
    ks = [slice(h * GDN_DK, (h + 1) * GDN_DK) for h in range(GDN_H)]
    vs = [slice(h * GDN_DV, (h + 1) * GDN_DV) for h in range(GDN_H)]
    q = [tok(q_ref, b, ks[h]) for b, h in chains]
    k = [tok(k_ref, b, ks[h]) for b, h in chains]
    egc = [egc_x[b][:, ks[h]] for b, h in chains]
    kb = [k[i] * beta_x[b][:, ks[h]] for i, (b, h) in enumerate(chains)]
    n_ch = range(len(chains))
    zero_k = jnp.zeros((L, GDN_DK), BF16)

    kq2, dec2 = [], []
    for b, p in pairs:
        ia, ib = chain_of(b, 2 * p), chain_of(b, 2 * p + 1)
        lhs = jnp.concatenate([jnp.concatenate([kb[ia], kb[ib]], axis=1),
                               jnp.concatenate([q[ia], q[ib]], axis=1)], axis=0).astype(BF16)
        rhs_nt = jnp.concatenate([jnp.concatenate([k[ia].astype(BF16), zero_k], axis=1),
                                  jnp.concatenate([zero_k, k[ib].astype(BF16)], axis=1)], axis=0)
        kq2.append(_dot_nt(lhs, rhs_nt))
        col2 = jnp.where(first, gc_x[b][:, ks[2 * p]], gc_x[b][:, ks[2 * p + 1]])
        dec2.append(jnp.exp(col2 - jnp.broadcast_to(gc_t2[b][p:p + 1, :], (L, LANE)) + incl_add))
    n_pr = range(len(pairs))
    t_inv2 = _unit_lower_inverses([kq2[j][0:L] * dec2[j] * strict for j in n_pr], eye, same16, same32, bd)

    zero_r = jnp.zeros((L, GDN_DV + GDN_DK), BF16)
    uw = [None] * len(chains)
    for j, (b, p) in enumerate(pairs):
        halves = []
        for h in (2 * p, 2 * p + 1):
            i = chain_of(b, h)
            beta_h = beta_x[b][:, ks[h]]
            halves.append(jnp.concatenate([tok(v_ref, b, vs[h]) * jnp.concatenate([beta_h, beta_h], axis=1),
                                           kb[i] * egc[i]], axis=1).astype(BF16))
        rhs_bd = jnp.concatenate([jnp.concatenate([halves[0], zero_r], axis=1),
                                  jnp.concatenate([zero_r, halves[1]], axis=1)], axis=0)
        uw2 = _dot(t_inv2[j].astype(BF16), rhs_bd)
        width = GDN_DV + GDN_DK
        uw[chain_of(b, 2 * p)] = uw2[:, 0:width]
        uw[chain_of(b, 2 * p + 1)] = uw2[:, width:2 * width]
    zero_v = jnp.zeros((L, GDN_DV), BF16)
    for cc in range(nch):
        mine = [(i, b, h) for i, (b, h) in enumerate(chains) if b // nb == cc]
        s_old = {i: st_scr[b % nb, h] for i, b, h in mine}
        wq = {i: _dot(jnp.concatenate([uw[i][:, GDN_DV:], q[i] * egc[i]], axis=0).astype(BF16),
                      s_old[i].astype(BF16)) for i, b, h in mine}
        v_new = {i: (uw[i][:, :GDN_DV] - wq[i][0:L]).astype(BF16) for i, b, h in mine}
        o = {}
        for j, (b, p) in enumerate(pairs):
            if b // nb != cc:
                continue
            ia, ib = chain_of(b, 2 * p), chain_of(b, 2 * p + 1)
            v_bd = jnp.concatenate([jnp.concatenate([v_new[ia], zero_v], axis=1),
                                    jnp.concatenate([zero_v, v_new[ib]], axis=1)], axis=0)
            o2 = _dot((kq2[j][L:2 * L] * dec2[j]).astype(BF16), v_bd)
            o[ia] = wq[ia][L:2 * L] + o2[:, 0:GDN_DV]
            o[ib] = wq[ib][L:2 * L] + o2[:, GDN_DV:]
        for i, b, h in mine:
            e_last = egc[i][L - 1:L, :]
            st_scr[b % nb, h] = (s_old[i] * jnp.concatenate([e_last, e_last], axis=1)
                                 + _dot_tn((k[i] * egl_x[b][:, ks[h]]).astype(BF16), v_new[i]))
        for i, b, h in mine:
            o_h = o[i] * lax.rsqrt(jnp.mean(o[i] * o[i], axis=-1, keepdims=True) + EPS) * ow_ref[...]
            y_ref[b % nb, cc * L:(cc + 1) * L, vs[h]] = (o_h * tok(zs_ref, b, vs[h])).astype(BF16)

    @pl.when(c == n_steps - 1)
    def _():
        for s in range(nb):
            sfin_ref[s] = st_scr[s]


def _gdn_consts():
    k = jnp.arange(LANE)[:, None]
    col = jnp.arange(GDN_KEY)[None, :]
    e3 = ((k % REP == col // GDN_DK) & (k < 3 * REP)).astype(BF16)
    t = jnp.arange(CHUNK)
    r, cc = t[:, None], t[None, :]
    tri = (r >= cc).astype(F32)
    p = jnp.arange(GDN_H)[:, None]
    kk = jnp.arange(LANE)[None, :]
    sel = (((kk == 2 * p) | (kk == REP + 2 * p + 1)) & (p < GDN_H // 2)).astype(F32)
    masks = jnp.stack([
        jnp.where(r >= cc, 0.0, NEG_BIG),
        (r > cc).astype(F32),
        (r == cc).astype(F32),
        ((r // (CHUNK // 4)) == (cc // (CHUNK // 4))).astype(F32),
        ((r // (CHUNK // 2)) == (cc // (CHUNK // 2))).astype(F32),
    ]).astype(F32)
    masks = jnp.concatenate([masks, masks], axis=-1)
    b = jnp.arange(2 * CHUNK)
    bd = ((b[:, None] // CHUNK) == (b[None, :] // CHUNK)).astype(BF16)
    return e3, tri, sel, bd, masks


def _gdn_mixer(q, k, v, zs, g, beta, s0, p, n_valid):
    n_seq, t_len, _ = zs.shape
    n_chunks = t_len // CHUNK
    nb = _seqs_per_step(n_seq)
    nch = GDN_CHUNKS_PER_STEP if (nb <= 2 and n_chunks % GDN_CHUNKS_PER_STEP == 0) else 1
    rows = nch * CHUNK
    shared = s0.shape[0] == 1
    init_spec = (pl.BlockSpec((1, GDN_H, GDN_DK, GDN_DV), lambda s, c: (0, 0, 0, 0)) if shared
                 else pl.BlockSpec((nb, GDN_H, GDN_DK, GDN_DV), lambda s, c: (s, 0, 0, 0)))
    tok_map = lambda s, c: (s, c, 0)
    e3, tri, sel, bd, masks = _gdn_consts()
    consts2 = (p["onorm_w"], e3, tri, sel, bd)
    return pl.pallas_call(
        functools.partial(_gdn_kernel, nb, nch, shared, n_valid, n_chunks // nch),
        grid=(n_seq // nb, n_chunks // nch),
        in_specs=[
            pl.BlockSpec((nb, rows, GDN_KEY), tok_map),
            pl.BlockSpec((nb, rows, GDN_KEY), tok_map),
            pl.BlockSpec((nb, rows, GDN_VAL), tok_map),
            pl.BlockSpec((nb, rows, GDN_VAL), tok_map),
            pl.BlockSpec((nb, rows, LANE), tok_map),
            pl.BlockSpec((nb, rows, LANE), tok_map),
            init_spec,
        ] + [pl.BlockSpec(cst.shape, lambda s, c: (0, 0)) for cst in consts2]
          + [pl.BlockSpec(masks.shape, lambda s, c: (0, 0, 0))],
        out_specs=[
            pl.BlockSpec((nb, rows, GDN_VAL), tok_map),
            pl.BlockSpec((nb, GDN_H, GDN_DK, GDN_DV), lambda s, c: (s, 0, 0, 0)),
        ],
        out_shape=[
            jax.ShapeDtypeStruct((n_seq, t_len, GDN_VAL), BF16),
            jax.ShapeDtypeStruct((n_seq, GDN_H, GDN_DK, GDN_DV), F32),
        ],
        scratch_shapes=[pltpu.VMEM((nb, GDN_H, GDN_DK, GDN_DV), F32)],
        compiler_params=pltpu.CompilerParams(
            dimension_semantics=("parallel", "arbitrary"), vmem_limit_bytes=VMEM_LIMIT),
        name="gdn_mixer",
    )(q, k, v, zs, g, beta, s0, *consts2, masks)


def _rep_lanes(v, n_heads):
    n_rep = LANE // REP if n_heads == REP else 3
    row = jnp.pad(v, (0, REP - n_heads))
    return jnp.pad(jnp.tile(row, n_rep), (0, LANE - n_rep * REP)).reshape(1, LANE)


def _rep_cols(w, n_heads):
    n_rep = LANE // REP if n_heads == REP else 3
    blk = jnp.pad(w, ((0, 0), (0, REP - n_heads)))
    return jnp.pad(jnp.tile(blk, (1, n_rep)), ((0, 0), (0, LANE - n_rep * REP)))


def _ssd_state_in(s):
    return s.reshape(s.shape[0], SSD_INNER, SSD_N)


def _ssd_state_out(s):
    return s.reshape(s.shape[0], SSD_H, SSD_P, SSD_N)


def _tail_from_rows(rows3):
    return jnp.pad(rows3, ((0, 0), (TAIL - (CONV_K - 1), 0), (0, 0)))


def _segment_valid(n_valid, tm_in, spt):
    return tm_in // spt if n_valid == CHUNK else n_valid


def _ssd_layer(h, tail0, s0, p, layer, spt, tm_in, tm_out, n_valid, final_w=None):
    n_grp, rows, _ = h.shape
    weights = [(p["w_in"], layer), (p["w_dt4"][layer], None)]
    consts = [p["norm_w"][layer].reshape(1, D_MODEL), p["conv_w"][layer], p["conv_b"][layer].reshape(1, SSD_CONV),
              p["dt_bias4"][layer]]
    zs, u, dt, tails = _inproj_call(_ssd_inproj_kernel, h, tail0, spt, tm_in, _segment_valid(n_valid, tm_in, spt),
                                    weights, consts, (SSD_INNER, SSD_CONV, LANE), SSD_CONV, "ssd_inproj")
    seq = lambda a: a.reshape(n_grp * spt, rows // spt, a.shape[-1])
    mp = dict(a_log4=p["a_log4"][layer], d_x=p["d_x"][layer], gnorm_w=p["gnorm_w"][layer])
    y, s_fin = _ssd_mixer(seq(zs), seq(u), seq(dt), s0, mp, n_valid)
    h_new = _outproj(y.reshape(n_grp * rows, SSD_INNER), p["w_out"], layer, h.reshape(n_grp * rows, D_MODEL),
                     tm_out, final_w)
    return h_new.reshape(h.shape), tails, s_fin


def _gdn_layer(h, tail0, s0, p, layer, spt, tm_in, tm_out, n_valid, final_w=None):
    n_grp, rows, _ = h.shape
    weights = [(p["w_in"], layer), (p["w_ab"][layer], None)]
    consts = [p["norm_w"][layer].reshape(1, D_MODEL), p["conv_w"][layer], p["dt_bias"][layer], p["a_log"][layer]]
    q, k, v, zs, g, beta, tails = _inproj_call(
        _gdn_inproj_kernel, h, tail0, spt, tm_in, _segment_valid(n_valid, tm_in, spt), weights, consts,
        (GDN_KEY, GDN_KEY, GDN_VAL, GDN_VAL, LANE, LANE), GDN_CONV, "gdn_inproj")
    seq = lambda a: a.reshape(n_grp * spt, rows // spt, a.shape[-1])
    mp = dict(onorm_w=p["onorm_w"][layer].reshape(1, GDN_DV))
    y, s_fin = _gdn_mixer(seq(q), seq(k), seq(v), seq(zs), seq(g), seq(beta), s0, mp, n_valid)
    h_new = _outproj(y.reshape(n_grp * rows, GDN_VAL), p["w_out"], layer, h.reshape(n_grp * rows, D_MODEL),
                     tm_out, final_w)
    return h_new.reshape(h.shape), tails, s_fin


def kernel(x_prompt, x_sample, state_ssd, state_ssd_conv, state_gdn, state_gdn_conv, meta_tokens,
           ssd_norm_w, ssd_w_in, ssd_conv_w, ssd_conv_b, ssd_dt_bias, ssd_a_log, ssd_d, ssd_gnorm_w, ssd_w_out,
           gdn_norm_w, gdn_w_in, gdn_conv_w, gdn_dt_bias, gdn_a_log, gdn_onorm_w, gdn_w_out, final_norm_w):
    n_ssd, n_gdn = ssd_norm_w.shape[0], gdn_norm_w.shape[0]
    depth = n_ssd + n_gdn
    n_dec, dec_t, _ = x_sample.shape
    assert dec_t == N_META and N_META <= CHUNK and x_prompt.shape[1] % TM_PROMPT_OUT == 0
    keep = CONV_K - 1

    ssd_p = dict(
        norm_w=ssd_norm_w, w_in=ssd_w_in.astype(BF16), w_out=ssd_w_out.astype(BF16),
        w_dt4=[_rep_cols(ssd_w_in[j][:, SSD_INNER + SSD_CONV:], SSD_H).astype(BF16) for j in range(n_ssd)],
        conv_w=ssd_conv_w, conv_b=ssd_conv_b,
        dt_bias4=[_rep_lanes(ssd_dt_bias[j], SSD_H) for j in range(n_ssd)],
        a_log4=[_rep_lanes(ssd_a_log[j], SSD_H) for j in range(n_ssd)],
        d_x=[jnp.repeat(ssd_d[j], SSD_P).reshape(1, SSD_INNER) for j in range(n_ssd)],
        gnorm_w=[ssd_gnorm_w[j].reshape(1, SSD_INNER) for j in range(n_ssd)],
    )
    ab0 = GDN_CONV + GDN_VAL
    gdn_p = dict(
        norm_w=gdn_norm_w, w_in=gdn_w_in.astype(BF16), w_out=gdn_w_out.astype(BF16),
        w_ab=[jnp.concatenate([_rep_cols(gdn_w_in[j][:, ab0:ab0 + GDN_H], GDN_H),
                               _rep_cols(gdn_w_in[j][:, ab0 + GDN_H:], GDN_H)], axis=1).astype(BF16)
              for j in range(n_gdn)],
        conv_w=gdn_conv_w,
        dt_bias=[_rep_lanes(gdn_dt_bias[j], GDN_H) for j in range(n_gdn)],
        a_log=[_rep_lanes(gdn_a_log[j], GDN_H) for j in range(n_gdn)],
        onorm_w=gdn_onorm_w,
    )

    n_small = n_dec + 1
    hs = jnp.concatenate([x_sample, meta_tokens.astype(x_sample.dtype)[None]], axis=0)
    hs = jnp.pad(hs, ((0, 0), (0, CHUNK - N_META), (0, 0))).reshape(1, n_small * CHUNK, D_MODEL)
    hp = x_prompt
    tm_small = n_small * CHUNK
    last_rows = lambda tails: tails[:, TAIL - keep:]
    outs = {k: [] for k in ("p_ssd", "p_ssd_conv", "p_gdn", "p_gdn_conv", "s_ssd", "s_ssd_conv", "s_gdn", "s_gdn_conv")}
    for i in range(depth):
        j = i // 2
        fw = final_norm_w if i == depth - 1 else None
        if i % 2 == 0:
            tail_s = _tail_from_rows(jnp.concatenate([state_ssd_conv[j], jnp.zeros((1, keep, SSD_CONV), F32)], axis=0))
            s0_s = _ssd_state_in(jnp.concatenate([state_ssd[j], jnp.zeros((1,) + state_ssd.shape[2:], F32)], axis=0))
            hs, tails_s, sfin_s = _ssd_layer(hs, tail_s, s0_s, ssd_p, j, n_small, tm_small, tm_small, N_META, fw)
            hp, tails_p, sfin_p = _ssd_layer(hp, tails_s[n_dec:], sfin_s[n_dec:], ssd_p, j, 1, TM_PROMPT_IN,
                                             TM_PROMPT_OUT, CHUNK, fw)
            outs["s_ssd"].append(_ssd_state_out(sfin_s[:n_dec]))
            outs["p_ssd"].append(_ssd_state_out(sfin_p))
            outs["s_ssd_conv"].append(last_rows(tails_s[:n_dec]))
            outs["p_ssd_conv"].append(last_rows(tails_p))
        else:
            tail_s = _tail_from_rows(jnp.concatenate([state_gdn_conv[j], jnp.zeros((1, keep, GDN_CONV), F32)], axis=0))
            s0_s = jnp.concatenate([state_gdn[j], jnp.zeros((1,) + state_gdn.shape[2:], F32)], axis=0)
            hs, tails_s, sfin_s = _gdn_layer(hs, tail_s, s0_s, gdn_p, j, n_small, tm_small, tm_small, N_META, fw)
            hp, tails_p, sfin_p = _gdn_layer(hp, tails_s[n_dec:], sfin_s[n_dec:], gdn_p, j, 1, TM_PROMPT_IN,
                                             TM_PROMPT_OUT, CHUNK, fw)
            outs["s_gdn"].append(sfin_s[:n_dec])
            outs["p_gdn"].append(sfin_p)
            outs["s_gdn_conv"].append(last_rows(tails_s[:n_dec]))
            outs["p_gdn_conv"].append(last_rows(tails_p))
    y_sample = hs.reshape(n_small, CHUNK, D_MODEL)[:n_dec, :N_META]
    st = lambda key: jnp.stack(outs[key])
    return (hp, y_sample, st("p_ssd"), st("p_ssd_conv"), st("p_gdn"), st("p_gdn_conv"),
            st("s_ssd"), st("s_ssd_conv"), st("s_gdn"), st("s_gdn_conv"))
```

```python
import functools

import jax
import jax.numpy as jnp
from jax import lax
from jax.experimental import pallas as pl
from jax.experimental.pallas import tpu as pltpu

F32 = jnp.float32
BF16 = jnp.bfloat16
HIGHEST = lax.Precision.HIGHEST

D_MODEL = 1024
N_META = 16
CONV_K = 4
EPS = 1e-6
CHUNK = 64
TAIL = 8
LANE = 128
COL_BLK = 512
SSD_INNER = 2048
SSD_P = 64
SSD_H = 32
SSD_G = 4
SSD_N = 128
SSD_GW = SSD_INNER // SSD_G
SSD_CONV = SSD_INNER + 2 * SSD_G * SSD_N
GDN_DK = 128
GDN_DV = 256
GDN_H = 8
GDN_KEY = GDN_H * GDN_DK
GDN_VAL = GDN_H * GDN_DV
GDN_CONV = 2 * GDN_KEY + GDN_VAL
REP = 32
NEG_BIG = -1e30

VMEM_LIMIT = 48 * 1024 * 1024
TM_PROMPT_IN = 256
TM_PROMPT_OUT = 512
GDN_CHUNKS_PER_STEP = 2


NEG_LOG2E = -1.4426950408889634


def _sigmoid(x):
    return 1.0 / (1.0 + jnp.exp2(x * NEG_LOG2E))


def _silu(x):
    return x * _sigmoid(x)


def _softplus(x):
    return jnp.maximum(x, 0.0) + jnp.log1p(jnp.exp(-jnp.abs(x)))


def _dot(a, b, precision=None):
    return jnp.dot(a, b, preferred_element_type=F32, precision=precision)


def _dot_nt(a, b, precision=None):
    return lax.dot_general(a, b, (((1,), (1,)), ((), ())), preferred_element_type=F32, precision=precision)


def _dot_tn(a, b):
    return lax.dot_general(a, b, (((0,), (0,)), ((), ())), preferred_element_type=F32)


def _iota(shape, dim):
    return lax.broadcasted_iota(jnp.int32, shape, dim)


def _split3_merge(x):
    hi = x.astype(BF16)
    r1 = x - hi.astype(F32)
    mid = r1.astype(BF16)
    lo = (r1 - mid.astype(F32)).astype(BF16)
    lane = _iota(x.shape, 1)
    zero = jnp.zeros_like(hi)
    return jnp.where(lane < REP, hi, jnp.where(lane < 2 * REP, mid, jnp.where(lane < 3 * REP, lo, zero)))


def _rmsnorm_bf16(x, w):
    ms = jnp.mean(x * x, axis=-1, keepdims=True)
    return (x * lax.rsqrt(ms + EPS) * w).astype(BF16)


def _conv_block(tail_scr, s, cols, raw, n_valid, cw_ref, bias, tailo_ref):
    seg = raw.shape[0]
    x = [tail_scr[s, :, cols]] + [raw[i * TAIL:(i + 1) * TAIL] for i in range(seg // TAIL)]
    last = raw[n_valid - TAIL:n_valid]
    tail_scr[s, :, cols] = last
    tailo_ref[s, :, cols] = last
    row = _iota((TAIL, raw.shape[1]), 0)

    def shift_down(tiles, d):
        mixed = [tiles[0]] + [jnp.where(row >= TAIL - d, tiles[i - 1], tiles[i]) for i in range(1, len(tiles))]
        return [pltpu.roll(t, d, 0) for t in mixed]

    assert CONV_K == 4
    w = [cw_ref[k:k + 1, cols] for k in range(CONV_K)]
    x1 = shift_down(x, 1)
    near = [t * w[3] + t1 * w[2] for t, t1 in zip(x, x1)]
    far = shift_down([t * w[1] + t1 * w[0] for t, t1 in zip(x, x1)], 2)
    acc = jnp.concatenate([a + b for a, b in zip(near[1:], far[1:])], axis=0)
    if bias is not None:
        acc = acc + bias
    return acc


def _ssd_inproj_kernel(spt, seg, n_valid, x_ref, nw_ref, w_ref, wdt_ref, cw_ref, cb_ref, dtb_ref, tail_ref,
                       zs_ref, u_ref, dt_ref, tailo_ref, tail_scr):
    @pl.when(pl.program_id(1) == 0)
    def _():
        tail_scr[...] = tail_ref[...]

    xn = _rmsnorm_bf16(x_ref[0], nw_ref[...])
    dt_ref[0] = _softplus(_dot(xn, wdt_ref[...]) + dtb_ref[...])

    def conv_dot(j):
        return _dot(xn, w_ref[:, SSD_INNER + j * COL_BLK:SSD_INNER + (j + 1) * COL_BLK])

    n_conv, n_gate = SSD_CONV // COL_BLK, SSD_INNER // COL_BLK
    raw = conv_dot(0)
    for j in range(n_conv):
        cols = slice(j * COL_BLK, (j + 1) * COL_BLK)
        raw_next = conv_dot(j + 1) if j + 1 < n_conv else None
        gate = _dot(xn, w_ref[:, cols]) if j < n_gate else None
        for s in range(spt):
            acc = _conv_block(tail_scr, s, cols, raw[s * seg:(s + 1) * seg], n_valid, cw_ref, cb_ref[:, cols],
                              tailo_ref)
            u_ref[0, s * seg:(s + 1) * seg, cols] = _silu(acc)
        if gate is not None:
            zs_ref[0, :, cols] = _silu(gate)
        raw = raw_next


def _gdn_inproj_kernel(spt, seg, n_valid, x_ref, nw_ref, w_ref, wab_ref, cw_ref, dtb_ref, alog_ref, tail_ref,
                       q_ref, k_ref, v_ref, zs_ref, g_ref, beta_ref, tailo_ref, tail_scr):
    @pl.when(pl.program_id(1) == 0)
    def _():
        tail_scr[...] = tail_ref[...]

    xn = _rmsnorm_bf16(x_ref[0], nw_ref[...])
    ab = _dot(xn, wab_ref[...])
    lane = _iota((1, LANE), 1)
    head_lane = ((lane & (REP - 1)) < GDN_H) & (lane < 3 * REP)
    coef = jnp.where(head_lane, -jnp.exp(alog_ref[...]), 0.0)
    g_ref[0] = coef * _softplus(ab[:, :LANE] + dtb_ref[...])
    beta_ref[0] = _sigmoid(ab[:, LANE:])
    def conv_dot(j):
        return _dot(xn, w_ref[:, j * COL_BLK:(j + 1) * COL_BLK])

    n_conv, n_gate = GDN_CONV // COL_BLK, GDN_VAL // COL_BLK
    raw_next = conv_dot(0)
    for j in range(n_conv):
        cols = slice(j * COL_BLK, (j + 1) * COL_BLK)
        raw = raw_next
        raw_next = conv_dot(j + 1) if j + 1 < n_conv else None
        if j < n_gate:
            gate = _dot(xn, w_ref[:, GDN_CONV + j * COL_BLK:GDN_CONV + (j + 1) * COL_BLK])
            zs_ref[0, :, cols] = _silu(gate)
        for s in range(spt):
            rows = slice(s * seg, (s + 1) * seg)
            u = _silu(_conv_block(tail_scr, s, cols, raw[rows], n_valid, cw_ref, None, tailo_ref))
            if j * COL_BLK < 2 * GDN_KEY:
                is_q = j * COL_BLK < GDN_KEY
                dst = q_ref if is_q else k_ref
                off = j * COL_BLK - (0 if is_q else GDN_KEY)
                for i in range(COL_BLK // GDN_DK):
                    t = u[:, i * GDN_DK:(i + 1) * GDN_DK]
                    t = t * lax.rsqrt(jnp.sum(t * t, axis=-1, keepdims=True) + EPS)
                    if is_q:
                        t = t * (GDN_DK ** -0.5)
                    dst[0, rows, off + i * GDN_DK:off + (i + 1) * GDN_DK] = t
            else:
                off = j * COL_BLK - 2 * GDN_KEY
                v_ref[0, rows, off:off + COL_BLK] = u


def _inproj_call(body, x, tail0, spt, tm, n_valid, weights, consts, out_widths, conv_dim, name):
    n_grp, rows, _ = x.shape
    assert rows % tm == 0 and tm % spt == 0
    seg = tm // spt
    n_tiles = rows // tm
    assert (spt == 1 and n_valid == seg) or n_tiles == 1
    assert n_valid % TAIL == 0 and TAIL <= n_valid <= seg
    shared = tail0.shape[0] == 1
    tok_map = lambda g, t: (g, t, 0)
    tail_map = (lambda g, t: (0, 0, 0)) if shared else (lambda g, t: (g, 0, 0))
    in_specs = [pl.BlockSpec((1, tm, D_MODEL), tok_map), pl.BlockSpec((1, D_MODEL), lambda g, t: (0, 0))]
    args = [x, consts[0]]
    for w, layer in weights:
        if layer is None:
            in_specs.append(pl.BlockSpec(w.shape, lambda g, t: (0, 0), pipeline_mode=pl.Buffered(1)))
        else:
            in_specs.append(pl.BlockSpec((None,) + w.shape[1:], lambda g, t, layer=layer: (layer, 0, 0),
                                         pipeline_mode=pl.Buffered(1)))
        args.append(w)
    for cst in consts[1:]:
        in_specs.append(pl.BlockSpec(cst.shape, lambda g, t: (0, 0)))
        args.append(cst)
    in_specs.append(pl.BlockSpec((spt, TAIL, conv_dim), tail_map))
    args.append(tail0)
    out_specs = [pl.BlockSpec((1, tm, wd), tok_map) for wd in out_widths]
    out_shape = [jax.ShapeDtypeStruct((n_grp, rows, wd), F32) for wd in out_widths]
    out_specs.append(pl.BlockSpec((spt, TAIL, conv_dim), lambda g, t: (g, 0, 0)))
    out_shape.append(jax.ShapeDtypeStruct((n_grp * spt, TAIL, conv_dim), F32))
    return pl.pallas_call(
        functools.partial(body, spt, seg, n_valid),
        grid=(n_grp, n_tiles),
        in_specs=in_specs, out_specs=out_specs, out_shape=out_shape,
        scratch_shapes=[pltpu.VMEM((spt, TAIL, conv_dim), F32)],
        compiler_params=pltpu.CompilerParams(
            dimension_semantics=("parallel", "arbitrary"), vmem_limit_bytes=VMEM_LIMIT),
        name=name,
    )(*args)


def _outproj_kernel(final, y_ref, w_ref, h_ref, *rest):
    if final:
        fw_ref, o_ref = rest
    else:
        (o_ref,) = rest
    h = h_ref[...] + _dot(y_ref[...], w_ref[...])
    if final:
        ms = jnp.mean(h * h, axis=-1, keepdims=True)
        h = h * lax.rsqrt(ms + EPS) * fw_ref[...]
    o_ref[...] = h


def _outproj(y2d, w_stack, layer, h2d, tm, final_w=None):
    m, k = y2d.shape
    assert m % tm == 0
    final = final_w is not None
    in_specs = [
        pl.BlockSpec((tm, k), lambda i: (i, 0)),
        pl.BlockSpec((None, k, D_MODEL), lambda i: (layer, 0, 0), pipeline_mode=pl.Buffered(1)),
        pl.BlockSpec((tm, D_MODEL), lambda i: (i, 0)),
    ]
    args = [y2d, w_stack, h2d]
    if final:
        in_specs.append(pl.BlockSpec((1, D_MODEL), lambda i: (0, 0)))
        args.append(final_w.reshape(1, D_MODEL))
    return pl.pallas_call(
        functools.partial(_outproj_kernel, final),
        grid=(m // tm,),
        in_specs=in_specs,
        out_specs=pl.BlockSpec((tm, D_MODEL), lambda i: (i, 0)),
        out_shape=jax.ShapeDtypeStruct((m, D_MODEL), F32),
        compiler_params=pltpu.CompilerParams(
            dimension_semantics=("parallel",), vmem_limit_bytes=VMEM_LIMIT),
        name="outproj_final" if final else "outproj",
    )(*args)


def _ssd_kernel(nb, shared, n_valid, n_chunks,
                zs_ref, u_ref, dt_ref, s0_ref, alog_ref, dsk_ref, gw_ref, e3_ref, tri_ref, sel_ref,
                cmask_ref, bmask_ref,
                y_ref, sfin_ref, st_scr):
    L = CHUNK
    c = pl.program_id(1)
    seqs = range(nb)
    groups = range(SSD_G)
    gn = [slice(g * SSD_N, (g + 1) * SSD_N) for g in groups]
    gw = [slice(g * SSD_GW, (g + 1) * SSD_GW) for g in groups]

    @pl.when(c == 0)
    def _():
        for b in seqs:
            st_scr[b] = s0_ref[0 if shared else b].T

    bm_b = [u_ref[b, :, SSD_INNER:SSD_INNER + SSD_G * SSD_N].astype(BF16) for b in seqs]
    cm_b = [u_ref[b, :, SSD_INNER + SSD_G * SSD_N:SSD_CONV].astype(BF16) for b in seqs]

    dt4 = [dt_ref[b] for b in seqs]
    if n_valid < L:
        live = _iota((L, LANE), 0) < n_valid
        dt4 = [jnp.where(live, d, 0.0) for d in dt4]
    neg_a = -jnp.exp(alog_ref[...])
    cs4 = [_dot(tri_ref[...], dt4[b] * neg_a, precision=HIGHEST) for b in seqs]
    stack3 = [_split3_merge(jnp.concatenate([cs4[b], dt4[b] * jnp.exp(cs4[b][L - 1:L, :] - cs4[b])], axis=0))
              for b in seqs]

    lane4 = _iota((L, LANE), 1)
    lo_lanes = lane4 < REP
    hi_lanes = (lane4 >= REP) & (lane4 < 2 * REP)

    def halves(v):
        return [jnp.where(lo_lanes, v, 0.0), jnp.where(hi_lanes, v, 0.0)]

    t2 = [_dot_nt(sel_ref[...], jnp.concatenate(halves(cs4[b]) + halves(dt4[b]), axis=0), precision=HIGHEST)
          for b in seqs]
    cb2 = [[_dot_nt(cm_b[b][:, gn[g]], jnp.concatenate([bm_b[b][:, gn[g]]] * 2, axis=0)) for b in seqs]
           for g in groups]

    slab = 4 * L
    n_slab = SSD_INNER // slab
    per_group = SSD_GW // slab

    def expand(j):
        return [_dot(stack3[b], e3_ref[:, j * slab:(j + 1) * slab]) for b in seqs]

    ex_next = expand(0)
    gated = [[] for _ in seqs]
    for j in range(n_slab):
        g = j // per_group
        lanes = slice(j * slab, (j + 1) * slab)
        ex = ex_next
        if j + 1 < n_slab:
            ex_next = expand(j + 1)
        st_old = [st_scr[b, :, lanes] for b in seqs]
        y_off = [_dot(cm_b[b][:, gn[g]], st_old[b].astype(BF16)) for b in seqs]
        xs, y_in, ecs, dtw = [], [], [], []
        for b in seqs:
            cs_x, dtw_x = ex[b][0:L], ex[b][L:2 * L]
            rows = [jnp.concatenate([jnp.broadcast_to(t2[b][2 * j + i:2 * j + i + 1, off:off + LANE], (L, LANE))
                                     for i in range(slab // LANE)], axis=1) for off in (0, LANE)]
            decay = jnp.exp(cs_x - rows[0] + cmask_ref[...]) * rows[1]
            m = (jnp.concatenate([cb2[g][b]] * (slab // LANE), axis=1) * decay).astype(BF16)
            x = u_ref[b, :, lanes]
            rhs = jnp.concatenate([x.astype(BF16)] * 4, axis=0) * bmask_ref[...]
            y_in.append(_dot(m, rhs))
            xs.append(x)
            ecs.append(jnp.exp(cs_x))
            dtw.append(dtw_x)
        for b in seqs:
            st_scr[b, :, lanes] = (st_old[b] * ecs[b][L - 1:L, :]
                                   + _dot_tn(bm_b[b][:, gn[g]], (xs[b] * dtw[b]).astype(BF16)))
            y = y_in[b] + y_off[b] * ecs[b] + xs[b] * dsk_ref[:, lanes]
            gated[b].append(y * zs_ref[b, :, lanes])
        if (j + 1) % per_group == 0:
            for b in seqs:
                blk_g = jnp.concatenate(gated[b], axis=1)
                gated[b] = []
                ms = jnp.mean(blk_g * blk_g, axis=-1, keepdims=True)
                y_ref[b, :, gw[g]] = (blk_g * lax.rsqrt(ms + EPS) * gw_ref[:, gw[g]]).astype(BF16)

    @pl.when(c == n_chunks - 1)
    def _():
        for b in seqs:
            sfin_ref[b] = st_scr[b].T


def _ssd_consts():
    k = jnp.arange(LANE)[:, None]
    col = jnp.arange(SSD_INNER)[None, :]
    e3 = ((k % REP == col // SSD_P) & (k < 3 * REP)).astype(BF16)
    t = jnp.arange(CHUNK)
    tri = (t[:, None] >= t[None, :]).astype(F32)
    j = jnp.arange(SSD_H // 2)[:, None]
    kk = jnp.arange(LANE)[None, :]
    sel = ((kk == 2 * j) | (kk == REP + 2 * j + 1)).astype(F32)
    b = jnp.arange(4 * CHUNK)
    cmask = jnp.where(t[:, None] >= (b[None, :] % CHUNK), 0.0, NEG_BIG).astype(F32)
    bmask = ((b[:, None] // CHUNK) == (b[None, :] // CHUNK)).astype(BF16)
    return e3, tri, sel, cmask, bmask


def _seqs_per_step(n_seq):
    return 2 if n_seq % 2 == 0 else (3 if n_seq % 3 == 0 else 1)


def _ssd_mixer(zs, u, dt, s0, p, n_valid):
    n_seq, t_len, _ = zs.shape
    n_chunks = t_len // CHUNK
    nb = _seqs_per_step(n_seq)
    shared = s0.shape[0] == 1
    init_spec = (pl.BlockSpec((1, SSD_INNER, SSD_N), lambda s, c: (0, 0, 0)) if shared
                 else pl.BlockSpec((nb, SSD_INNER, SSD_N), lambda s, c: (s, 0, 0)))
    tok_map = lambda s, c: (s, c, 0)
    const2 = lambda s, c: (0, 0)
    consts = (p["a_log4"], p["d_x"], p["gnorm_w"]) + _ssd_consts()
    return pl.pallas_call(
        functools.partial(_ssd_kernel, nb, shared, n_valid, n_chunks),
        grid=(n_seq // nb, n_chunks),
        in_specs=[
            pl.BlockSpec((nb, CHUNK, SSD_INNER), tok_map),
            pl.BlockSpec((nb, CHUNK, SSD_CONV), tok_map),
            pl.BlockSpec((nb, CHUNK, LANE), tok_map),
            init_spec,
        ] + [pl.BlockSpec(cst.shape, const2) for cst in consts],
        out_specs=[
            pl.BlockSpec((nb, CHUNK, SSD_INNER), tok_map),
            pl.BlockSpec((nb, SSD_INNER, SSD_N), lambda s, c: (s, 0, 0)),
        ],
        out_shape=[
            jax.ShapeDtypeStruct((n_seq, t_len, SSD_INNER), BF16),
            jax.ShapeDtypeStruct((n_seq, SSD_INNER, SSD_N), F32),
        ],
        scratch_shapes=[pltpu.VMEM((nb, SSD_N, SSD_INNER), F32)],
        compiler_params=pltpu.CompilerParams(
            dimension_semantics=("parallel", "arbitrary"), vmem_limit_bytes=VMEM_LIMIT),
        name="ssd_mixer",
    )(zs, u, dt, s0, *consts)


def _pair_blockdiag(y2, bd):
    y16 = y2.astype(BF16)
    return jnp.concatenate([y16, y16], axis=0) * bd


def _unit_lower_inverses(n_list, eye, same16, same32, bd):
    def mm(a_list, b_list):
        return [_dot(a.astype(BF16), _pair_blockdiag(b, bd)) for a, b in zip(a_list, b_list)]

    def axpy(t_list, d_list, sign):
        return [t + sign * d for t, d in zip(t_list, d_list)]

    nd = [n * same16 for n in n_list]
    t = [eye - x for x in nd]
    pw = mm(nd, nd)
    for step in range(3):
        t = axpy(t, mm(t, pw), 1.0)
        if step < 2:
            pw = mm(pw, pw)
    n1 = [n * (same32 - same16) for n in n_list]
    t = axpy(t, mm(t, mm(n1, t)), -1.0)
    n2 = [n * (1.0 - same32) for n in n_list]
    t = axpy(t, mm(t, mm(n2, t)), -1.0)
    return t


def _gdn_kernel(nb, nch, shared, n_valid, n_steps,
                q_ref, k_ref, v_ref, zs_ref, g_ref, beta_ref, s0_ref, ow_ref, e3_ref, tri_ref, sel_ref, bd_ref,
                masks_ref,
                y_ref, sfin_ref, st_scr):
    L = CHUNK
    c = pl.program_id(1)
    seqs = range(nb * nch)

    def tok(ref, b, cols):
        return ref[b % nb, (b // nb) * L:(b // nb + 1) * L, cols]

    @pl.when(c == 0)
    def _():
        for s in range(nb):
            st_scr[s] = s0_ref[0 if shared else s]

    g = [tok(g_ref, b, slice(None)) for b in seqs]
    beta = [tok(beta_ref, b, slice(None)) for b in seqs]
    if n_valid < L:
        live = _iota((L, LANE), 0) < n_valid
        g = [jnp.where(live, x, 0.0) for x in g]
        beta = [jnp.where(live, x, 0.0) for x in beta]
    gc = [_dot(tri_ref[...], g[b], precision=HIGHEST) for b in seqs]
    stack = []
    for b in seqs:
        stack += [gc[b], beta[b]]
    ex = _dot(_split3_merge(jnp.concatenate(stack, axis=0)), e3_ref[...])
    gc_x = [ex[(2 * b) * L:(2 * b + 1) * L] for b in seqs]
    beta_x = [ex[(2 * b + 1) * L:(2 * b + 2) * L] for b in seqs]
    egc_x = [jnp.exp(x) for x in gc_x]
    egl_x = [jnp.exp(x[L - 1:L, :] - x) for x in gc_x]
    lane = _iota((L, LANE), 1)
    gc_t2 = [_dot_nt(sel_ref[...], jnp.concatenate([jnp.where(lane < REP, gc[b], 0.0),
                                                    jnp.where((lane >= REP) & (lane < 2 * REP), gc[b], 0.0)],
                                                   axis=0), precision=HIGHEST) for b in seqs]

    incl_add = masks_ref[0]
    strict = masks_ref[1]
    eye = masks_ref[2]
    same16 = masks_ref[3]
    same32 = masks_ref[4]
    bd = bd_ref[...]
    first = lane < L

    chains = [(b, h) for h in range(GDN_H) for b in seqs]
    pairs = [(b, p) for p in range(GDN_H // 2) for b in seqs]
    chain_of = lambda b, h: h * len(seqs) + b
    ks = [slice(h * GDN_DK, (h + 1) * GDN_DK) for h in range(GDN_H)]
    vs = [slice(h * GDN_DV, (h + 1) * GDN_DV) for h in range(GDN_H)]
    q = [tok(q_ref, b, ks[h]) for b, h in chains]
    k = [tok(k_ref, b, ks[h]) for b, h in chains]
    egc = [egc_x[b][:, ks[h]] for b, h in chains]
    kb = [k[i] * beta_x[b][:, ks[h]] for i, (b, h) in enumerate(chains)]
    n_ch = range(len(chains))
    zero_k = jnp.zeros((L, GDN_DK), BF16)

    kq2, dec2 = [], []
    for b, p in pairs:
        ia, ib = chain_of(b, 2 * p), chain_of(b, 2 * p + 1)
        lhs = jnp.concatenate([jnp.concatenate([kb[ia], kb[ib]], axis=1),
                               jnp.concatenate([q[ia], q[ib]], axis=1)], axis=0).astype(BF16)
        rhs_nt = jnp.concatenate([jnp.concatenate([k[ia].astype(BF16), zero_k], axis=1),
                                  jnp.concatenate([zero_k, k[ib].astype(BF16)], axis=1)], axis=0)
        kq2.append(_dot_nt(lhs, rhs_nt))
        col2 = jnp.where(first, gc_x[b][:, ks[2 * p]], gc_x[b][:, ks[2 * p + 1]])
        dec2.append(jnp.exp(col2 - jnp.broadcast_to(gc_t2[b][p:p + 1, :], (L, LANE)) + incl_add))
    n_pr = range(len(pairs))
    t_inv2 = _unit_lower_inverses([kq2[j][0:L] * dec2[j] * strict for j in n_pr], eye, same16, same32, bd)

    zero_r = jnp.zeros((L, GDN_DV + GDN_DK), BF16)
    uw = [None] * len(chains)
    for j, (b, p) in enumerate(pairs):
        halves = []
        for h in (2 * p, 2 * p + 1):
            i = chain_of(b, h)
            beta_h = beta_x[b][:, ks[h]]
            halves.append(jnp.concatenate([tok(v_ref, b, vs[h]) * jnp.concatenate([beta_h, beta_h], axis=1),
                                           kb[i] * egc[i]], axis=1).astype(BF16))
        rhs_bd = jnp.concatenate([jnp.concatenate([halves[0], zero_r], axis=1),
                                  jnp.concatenate([zero_r, halves[1]], axis=1)], axis=0)
        uw2 = _dot(t_inv2[j].astype(BF16), rhs_bd)
        width = GDN_DV + GDN_DK
        uw[chain_of(b, 2 * p)] = uw2[:, 0:width]
        uw[chain_of(b, 2 * p + 1)] = uw2[:, width:2 * width]
    zero_v = jnp.zeros((L, GDN_DV), BF16)
    for cc in range(nch):
        mine = [(i, b, h) for i, (b, h) in enumerate(chains) if b // nb == cc]
        s_old = {i: st_scr[b % nb, h] for i, b, h in mine}
        wq = {i: _dot(jnp.concatenate([uw[i][:, GDN_DV:], q[i] * egc[i]], axis=0).astype(BF16),
                      s_old[i].astype(BF16)) for i, b, h in mine}
        v_new = {i: (uw[i][:, :GDN_DV] - wq[i][0:L]).astype(BF16) for i, b, h in mine}
        o = {}
        for j, (b, p) in enumerate(pairs):
            if b // nb != cc:
                continue
            ia, ib = chain_of(b, 2 * p), chain_of(b, 2 * p + 1)
            v_bd = jnp.concatenate([jnp.concatenate([v_new[ia], zero_v], axis=1),
                                    jnp.concatenate([zero_v, v_new[ib]], axis=1)], axis=0)
            o2 = _dot((kq2[j][L:2 * L] * dec2[j]).astype(BF16), v_bd)
            o[ia] = wq[ia][L:2 * L] + o2[:, 0:GDN_DV]
            o[ib] = wq[ib][L:2 * L] + o2[:, GDN_DV:]
        for i, b, h in mine:
            e_last = egc[i][L - 1:L, :]
            st_scr[b % nb, h] = (s_old[i] * jnp.concatenate([e_last, e_last], axis=1)
                                 + _dot_tn((k[i] * egl_x[b][:, ks[h]]).astype(BF16), v_new[i]))
        for i, b, h in mine:
            o_h = o[i] * lax.rsqrt(jnp.mean(o[i] * o[i], axis=-1, keepdims=True) + EPS) * ow_ref[...]
            y_ref[b % nb, cc * L:(cc + 1) * L, vs[h]] = (o_h * tok(zs_ref, b, vs[h])).astype(BF16)

    @pl.when(c == n_steps - 1)
    def _():
        for s in range(nb):
            sfin_ref[s] = st_scr[s]


def _gdn_consts():
    k = jnp.arange(LANE)[:, None]
    col = jnp.arange(GDN_KEY)[None, :]
    e3 = ((k % REP == col // GDN_DK) & (k < 3 * REP)).astype(BF16)
    t = jnp.arange(CHUNK)
    r, cc = t[:, None], t[None, :]
    tri = (r >= cc).astype(F32)
    p = jnp.arange(GDN_H)[:, None]
    kk = jnp.arange(LANE)[None, :]
    sel = (((kk == 2 * p) | (kk == REP + 2 * p + 1)) & (p < GDN_H // 2)).astype(F32)
    masks = jnp.stack([
        jnp.where(r >= cc, 0.0, NEG_BIG),
        (r > cc).astype(F32),
        (r == cc).astype(F32),
        ((r // (CHUNK // 4)) == (cc // (CHUNK // 4))).astype(F32),
        ((r // (CHUNK // 2)) == (cc // (CHUNK // 2))).astype(F32),
    ]).astype(F32)
    masks = jnp.concatenate([masks, masks], axis=-1)
    b = jnp.arange(2 * CHUNK)
    bd = ((b[:, None] // CHUNK) == (b[None, :] // CHUNK)).astype(BF16)
    return e3, tri, sel, bd, masks


def _gdn_mixer(q, k, v, zs, g, beta, s0, p, n_valid):
    n_seq, t_len, _ = zs.shape
    n_chunks = t_len // CHUNK
    nb = _seqs_per_step(n_seq)
    nch = GDN_CHUNKS_PER_STEP if (nb <= 2 and n_chunks % GDN_CHUNKS_PER_STEP == 0) else 1
    rows = nch * CHUNK
    shared = s0.shape[0] == 1
    init_spec = (pl.BlockSpec((1, GDN_H, GDN_DK, GDN_DV), lambda s, c: (0, 0, 0, 0)) if shared
                 else pl.BlockSpec((nb, GDN_H, GDN_DK, GDN_DV), lambda s, c: (s, 0, 0, 0)))
    tok_map = lambda s, c: (s, c, 0)
    e3, tri, sel, bd, masks = _gdn_consts()
    consts2 = (p["onorm_w"], e3, tri, sel, bd)
    return pl.pallas_call(
        functools.partial(_gdn_kernel, nb, nch, shared, n_valid, n_chunks // nch),
        grid=(n_seq // nb, n_chunks // nch),
        in_specs=[
            pl.BlockSpec((nb, rows, GDN_KEY), tok_map),
            pl.BlockSpec((nb, rows, GDN_KEY), tok_map),
            pl.BlockSpec((nb, rows, GDN_VAL), tok_map),
            pl.BlockSpec((nb, rows, GDN_VAL), tok_map),
            pl.BlockSpec((nb, rows, LANE), tok_map),
            pl.BlockSpec((nb, rows, LANE), tok_map),
            init_spec,
        ] + [pl.BlockSpec(cst.shape, lambda s, c: (0, 0)) for cst in consts2]
          + [pl.BlockSpec(masks.shape, lambda s, c: (0, 0, 0))],
        out_specs=[
            pl.BlockSpec((nb, rows, GDN_VAL), tok_map),
            pl.BlockSpec((nb, GDN_H, GDN_DK, GDN_DV), lambda s, c: (s, 0, 0, 0)),
        ],
        out_shape=[
            jax.ShapeDtypeStruct((n_seq, t_len, GDN_VAL), BF16),
            jax.ShapeDtypeStruct((n_seq, GDN_H, GDN_DK, GDN_DV), F32),
        ],
        scratch_shapes=[pltpu.VMEM((nb, GDN_H, GDN_DK, GDN_DV), F32)],
        compiler_params=pltpu.CompilerParams(
            dimension_semantics=("parallel", "arbitrary"), vmem_limit_bytes=VMEM_LIMIT),
        name="gdn_mixer",
    )(q, k, v, zs, g, beta, s0, *consts2, masks)


def _rep_lanes(v, n_heads):
    n_rep = LANE // REP if n_heads == REP else 3
    row = jnp.pad(v, (0, REP - n_heads))
    return jnp.pad(jnp.tile(row, n_rep), (0, LANE - n_rep * REP)).reshape(1, LANE)


def _rep_cols(w, n_heads):
    n_rep = LANE // REP if n_heads == REP else 3
    blk = jnp.pad(w, ((0, 0), (0, REP - n_heads)))
    return jnp.pad(jnp.tile(blk, (1, n_rep)), ((0, 0), (0, LANE - n_rep * REP)))


def _ssd_state_in(s):
    return s.reshape(s.shape[0], SSD_INNER, SSD_N)


def _ssd_state_out(s):
    return s.reshape(s.shape[0], SSD_H, SSD_P, SSD_N)


def _tail_from_rows(rows3):
    return jnp.pad(rows3, ((0, 0), (TAIL - (CONV_K - 1), 0), (0, 0)))


def _segment_valid(n_valid, tm_in, spt):
    return tm_in // spt if n_valid == CHUNK else n_valid


def _ssd_layer(h, tail0, s0, p, layer, spt, tm_in, tm_out, n_valid, final_w=None):
    n_grp, rows, _ = h.shape
    weights = [(p["w_in"], layer), (p["w_dt4"][layer], None)]
    consts = [p["norm_w"][layer].reshape(1, D_MODEL), p["conv_w"][layer], p["conv_b"][layer].reshape(1, SSD_CONV),
              p["dt_bias4"][layer]]
    zs, u, dt, tails = _inproj_call(_ssd_inproj_kernel, h, tail0, spt, tm_in, _segment_valid(n_valid, tm_in, spt),
                                    weights, consts, (SSD_INNER, SSD_CONV, LANE), SSD_CONV, "ssd_inproj")
    seq = lambda a: a.reshape(n_grp * spt, rows // spt, a.shape[-1])
    mp = dict(a_log4=p["a_log4"][layer], d_x=p["d_x"][layer], gnorm_w=p["gnorm_w"][layer])
    y, s_fin = _ssd_mixer(seq(zs), seq(u), seq(dt), s0, mp, n_valid)
    h_new = _outproj(y.reshape(n_grp * rows, SSD_INNER), p["w_out"], layer, h.reshape(n_grp * rows, D_MODEL),
                     tm_out, final_w)
    return h_new.reshape(h.shape), tails, s_fin


def _gdn_layer(h, tail0, s0, p, layer, spt, tm_in, tm_out, n_valid, final_w=None):
    n_grp, rows, _ = h.shape
    weights = [(p["w_in"], layer), (p["w_ab"][layer], None)]
    consts = [p["norm_w"][layer].reshape(1, D_MODEL), p["conv_w"][layer], p["dt_bias"][layer], p["a_log"][layer]]
    q, k, v, zs, g, beta, tails = _inproj_call(
        _gdn_inproj_kernel, h, tail0, spt, tm_in, _segment_valid(n_valid, tm_in, spt), weights, consts,
        (GDN_KEY, GDN_KEY, GDN_VAL, GDN_VAL, LANE, LANE), GDN_CONV, "gdn_inproj")
    seq = lambda a: a.reshape(n_grp * spt, rows // spt, a.shape[-1])
    mp = dict(onorm_w=p["onorm_w"][layer].reshape(1, GDN_DV))
    y, s_fin = _gdn_mixer(seq(q), seq(k), seq(v), seq(zs), seq(g), seq(beta), s0, mp, n_valid)
    h_new = _outproj(y.reshape(n_grp * rows, GDN_VAL), p["w_out"], layer, h.reshape(n_grp * rows, D_MODEL),
                     tm_out, final_w)
    return h_new.reshape(h.shape), tails, s_fin


def kernel(x_prompt, x_sample, state_ssd, state_ssd_conv, state_gdn, state_gdn_conv, meta_tokens,
           ssd_norm_w, ssd_w_in, ssd_conv_w, ssd_conv_b, ssd_dt_bias, ssd_a_log, ssd_d, ssd_gnorm_w, ssd_w_out,
           gdn_norm_w, gdn_w_in, gdn_conv_w, gdn_dt_bias, gdn_a_log, gdn_onorm_w, gdn_w_out, final_norm_w):
    n_ssd, n_gdn = ssd_norm_w.shape[0], gdn_norm_w.shape[0]
    depth = n_ssd + n_gdn
    n_dec, dec_t, _ = x_sample.shape
    assert dec_t == N_META and N_META <= CHUNK and x_prompt.shape[1] % TM_PROMPT_OUT == 0
    keep = CONV_K - 1

    ssd_p = dict(
        norm_w=ssd_norm_w, w_in=ssd_w_in.astype(BF16), w_out=ssd_w_out.astype(BF16),
        w_dt4=[_rep_cols(ssd_w_in[j][:, SSD_INNER + SSD_CONV:], SSD_H).astype(BF16) for j in range(n_ssd)],
        conv_w=ssd_conv_w, conv_b=ssd_conv_b,
        dt_bias4=[_rep_lanes(ssd_dt_bias[j], SSD_H) for j in range(n_ssd)],
        a_log4=[_rep_lanes(ssd_a_log[j], SSD_H) for j in range(n_ssd)],
        d_x=[jnp.repeat(ssd_d[j], SSD_P).reshape(1, SSD_INNER) for j in range(n_ssd)],
        gnorm_w=[ssd_gnorm_w[j].reshape(1, SSD_INNER) for j in range(n_ssd)],
    )
    ab0 = GDN_CONV + GDN_VAL
    gdn_p = dict(
        norm_w=gdn_norm_w, w_in=gdn_w_in.astype(BF16), w_out=gdn_w_out.astype(BF16),
        w_ab=[jnp.concatenate([_rep_cols(gdn_w_in[j][:, ab0:ab0 + GDN_H], GDN_H),
                               _rep_cols(gdn_w_in[j][:, ab0 + GDN_H:], GDN_H)], axis=1).astype(BF16)
              for j in range(n_gdn)],
        conv_w=gdn_conv_w,
        dt_bias=[_rep_lanes(gdn_dt_bias[j], GDN_H) for j in range(n_gdn)],
        a_log=[_rep_lanes(gdn_a_log[j], GDN_H) for j in range(n_gdn)],
        onorm_w=gdn_onorm_w,
    )

    n_small = n_dec + 1
    hs = jnp.concatenate([x_sample, meta_tokens.astype(x_sample.dtype)[None]], axis=0)
    hs = jnp.pad(hs, ((0, 0), (0, CHUNK - N_META), (0, 0))).reshape(1, n_small * CHUNK, D_MODEL)
    hp = x_prompt
    tm_small = n_small * CHUNK
    last_rows = lambda tails: tails[:, TAIL - keep:]
    outs = {k: [] for k in ("p_ssd", "p_ssd_conv", "p_gdn", "p_gdn_conv", "s_ssd", "s_ssd_conv", "s_gdn", "s_gdn_conv")}
    for i in range(depth):
        j = i // 2
        fw = final_norm_w if i == depth - 1 else None
        if i % 2 == 0:
            tail_s = _tail_from_rows(jnp.concatenate([state_ssd_conv[j], jnp.zeros((1, keep, SSD_CONV), F32)], axis=0))
            s0_s = _ssd_state_in(jnp.concatenate([state_ssd[j], jnp.zeros((1,) + state_ssd.shape[2:], F32)], axis=0))
            hs, tails_s, sfin_s = _ssd_layer(hs, tail_s, s0_s, ssd_p, j, n_small, tm_small, tm_small, N_META, fw)
            hp, tails_p, sfin_p = _ssd_layer(hp, tails_s[n_dec:], sfin_s[n_dec:], ssd_p, j, 1, TM_PROMPT_IN,
                                             TM_PROMPT_OUT, CHUNK, fw)
            outs["s_ssd"].append(_ssd_state_out(sfin_s[:n_dec]))
            outs["p_ssd"].append(_ssd_state_out(sfin_p))
            outs["s_ssd_conv"].append(last_rows(tails_s[:n_dec]))
            outs["p_ssd_conv"].append(last_rows(tails_p))
        else:
            tail_s = _tail_from_rows(jnp.concatenate([state_gdn_conv[j], jnp.zeros((1, keep, GDN_CONV), F32)], axis=0))
            s0_s = jnp.concatenate([state_gdn[j], jnp.zeros((1,) + state_gdn.shape[2:], F32)], axis=0)
            hs, tails_s, sfin_s = _gdn_layer(hs, tail_s, s0_s, gdn_p, j, n_small, tm_small, tm_small, N_META, fw)
            hp, tails_p, sfin_p = _gdn_layer(hp, tails_s[n_dec:], sfin_s[n_dec:], gdn_p, j, 1, TM_PROMPT_IN,
                                             TM_PROMPT_OUT, CHUNK, fw)
            outs["s_gdn"].append(sfin_s[:n_dec])
            outs["p_gdn"].append(sfin_p)
            outs["s_gdn_conv"].append(last_rows(tails_s[:n_dec]))
            outs["p_gdn_conv"].append(last_rows(tails_p))
    y_sample = hs.reshape(n_small, CHUNK, D_MODEL)[:n_dec, :N_META]
    st = lambda key: jnp.stack(outs[key])
    return (hp, y_sample, st("p_ssd"), st("p_ssd_conv"), st("p_gdn"), st("p_gdn_conv"),
            st("s_ssd"), st("s_ssd_conv"), st("s_gdn"), st("s_gdn_conv"))
```

```python
import functools

import jax
import jax.numpy as jnp
from jax import lax
from jax.experimental import pallas as pl
from jax.experimental.pallas import tpu as pltpu

F32 = jnp.float32
BF16 = jnp.bfloat16

D_MODEL = 1024
N_META = 16
CONV_K = 4
EPS = 1e-6
CHUNK = 64
TAIL = 8
LANE = 128
COL_BLK = 512
SSD_INNER = 2048
SSD_P = 64
SSD_H = 32
SSD_G = 4
SSD_N = 128
SSD_GW = SSD_INNER // SSD_G
SSD_CONV = SSD_INNER + 2 * SSD_G * SSD_N
GDN_DK = 128
GDN_DV = 256
GDN_H = 8
GDN_KEY = GDN_H * GDN_DK
GDN_VAL = GDN_H * GDN_DV
GDN_CONV = 2 * GDN_KEY + GDN_VAL
REP = 32
NEG_BIG = -1e30

VMEM_LIMIT = 48 * 1024 * 1024
TM_PROMPT_IN = 256
TM_PROMPT_OUT = 1024
GDN_CHUNKS_PER_STEP = 4


NEG_LOG2E = -1.4426950408889634


def _sigmoid(x):
    return 1.0 / (1.0 + jnp.exp2(x * NEG_LOG2E))


def _silu(x):
    return x * _sigmoid(x)


def _softplus(x):
    return jnp.maximum(x, 0.0) + jnp.log1p(jnp.exp(-jnp.abs(x)))


def _dot(a, b, precision=None):
    return jnp.dot(a, b, preferred_element_type=F32, precision=precision)


def _dot_nt(a, b, precision=None):
    return lax.dot_general(a, b, (((1,), (1,)), ((), ())), preferred_element_type=F32, precision=precision)


def _dot_tn(a, b):
    return lax.dot_general(a, b, (((0,), (0,)), ((), ())), preferred_element_type=F32)


def _iota(shape, dim):
    return lax.broadcasted_iota(jnp.int32, shape, dim)


def _split3(x):
    hi = x.astype(BF16)
    r1 = x - hi.astype(F32)
    mid = r1.astype(BF16)
    lo = (r1 - mid.astype(F32)).astype(BF16)
    return hi, mid, lo


def _split3_merge(x):
    hi, mid, lo = _split3(x)
    lane = _iota(x.shape, 1)
    zero = jnp.zeros_like(hi)
    return jnp.where(lane < REP, hi, jnp.where(lane < 2 * REP, mid, jnp.where(lane < 3 * REP, lo, zero)))


def _cumsum_rows(tri3, x):
    return _dot(tri3, jnp.concatenate(_split3(x), axis=0))


def _pair_rows(sel3, values):
    odd = (_iota(values[0].shape, 1) & 1) == 1
    zero = jnp.zeros(values[0].shape, BF16)
    rows = []
    for v in values:
        m3 = _split3_merge(v)
        rows += [jnp.where(odd, zero, m3), jnp.where(odd, m3, zero)]
    return _dot_nt(sel3, jnp.concatenate(rows, axis=0))


def _rmsnorm_bf16(x, w):
    ms = jnp.mean(x * x, axis=-1, keepdims=True)
    return (x * lax.rsqrt(ms + EPS) * w).astype(BF16)


def _cast_kernel(w_ref, o_ref):
    o_ref[...] = w_ref[...].astype(BF16)


def _cast_cols(w_stack, n_cols):
    n_layers, k, _ = w_stack.shape
    assert n_cols % COL_BLK == 0
    spec = pl.BlockSpec((1, k, COL_BLK), lambda l, j: (l, 0, j))
    return pl.pallas_call(
        _cast_kernel,
        grid=(n_layers, n_cols // COL_BLK),
        in_specs=[spec], out_specs=spec,
        out_shape=jax.ShapeDtypeStruct((n_layers, k, n_cols), BF16),
        compiler_params=pltpu.CompilerParams(dimension_semantics=("parallel", "parallel")),
        name="cast_weights",
    )(w_stack)


def _conv_block(tail_scr, s, cols, raw, n_valid, cw_ref, bias, tailo_ref):
    seg = raw.shape[0]
    x = [tail_scr[s, :, cols]] + [raw[i * TAIL:(i + 1) * TAIL] for i in range(seg // TAIL)]
    last = raw[n_valid - TAIL:n_valid]
    tail_scr[s, :, cols] = last
    tailo_ref[s, :, cols] = last
    row = _iota((TAIL, raw.shape[1]), 0)

    def shift_down(tiles, d):
        mixed = [tiles[0]] + [jnp.where(row >= TAIL - d, tiles[i - 1], tiles[i]) for i in range(1, len(tiles))]
        return [pltpu.roll(t, d, 0) for t in mixed]

    assert CONV_K == 4
    w = [cw_ref[k:k + 1, cols] for k in range(CONV_K)]
    x1 = shift_down(x, 1)
    near = [t * w[3] + t1 * w[2] for t, t1 in zip(x, x1)]
    far = shift_down([t * w[1] + t1 * w[0] for t, t1 in zip(x, x1)], 2)
    acc = jnp.concatenate([a + b for a, b in zip(near[1:], far[1:])], axis=0)
    if bias is not None:
        acc = acc + bias
    return acc


def _ssd_inproj_kernel(spt, seg, n_valid, x_ref, nw_ref, w_ref, wdt_ref, cw_ref, cb_ref, dtb_ref, tail_ref,
                       zs_ref, u_ref, dt_ref, tailo_ref, tail_scr):
    @pl.when(pl.program_id(1) == 0)
    def _():
        tail_scr[...] = tail_ref[...]

    xn = _rmsnorm_bf16(x_ref[0], nw_ref[...])
    dt_ref[0] = _softplus(_dot(xn, wdt_ref[...]) + dtb_ref[...])

    def conv_dot(j):
        return _dot(xn, w_ref[:, SSD_INNER + j * COL_BLK:SSD_INNER + (j + 1) * COL_BLK])

    n_conv, n_gate = SSD_CONV // COL_BLK, SSD_INNER // COL_BLK
    raw = conv_dot(0)
    for j in range(n_conv):
        cols = slice(j * COL_BLK, (j + 1) * COL_BLK)
        raw_next = conv_dot(j + 1) if j + 1 < n_conv else None
        gate = _dot(xn, w_ref[:, cols]) if j < n_gate else None
        for s in range(spt):
            acc = _conv_block(tail_scr, s, cols, raw[s * seg:(s + 1) * seg], n_valid, cw_ref, cb_ref[:, cols],
                              tailo_ref)
            u_ref[0, s * seg:(s + 1) * seg, cols] = _silu(acc)
        if gate is not None:
            zs_ref[0, :, cols] = _silu(gate)
        raw = raw_next


def _gdn_inproj_kernel(spt, seg, n_valid, x_ref, nw_ref, w_ref, wab_ref, cw_ref, dtb_ref, alog_ref, tail_ref,
                       q_ref, k_ref, v_ref, zs_ref, g_ref, beta_ref, tailo_ref, tail_scr):
    @pl.when(pl.program_id(1) == 0)
    def _():
        tail_scr[...] = tail_ref[...]

    xn = _rmsnorm_bf16(x_ref[0], nw_ref[...])
    ab = _dot(xn, wab_ref[...])
    lane = _iota((1, LANE), 1)
    head_lane = ((lane & (REP - 1)) < GDN_H) & (lane < 3 * REP)
    coef = jnp.where(head_lane, -jnp.exp(alog_ref[...]), 0.0)
    g_ref[0] = coef * _softplus(ab[:, :LANE] + dtb_ref[...])
    beta_ref[0] = _sigmoid(ab[:, LANE:])
    def conv_dot(j):
        return _dot(xn, w_ref[:, j * COL_BLK:(j + 1) * COL_BLK])

    n_conv, n_gate = GDN_CONV // COL_BLK, GDN_VAL // COL_BLK
    raw_next = conv_dot(0)
    for j in range(n_conv):
        cols = slice(j * COL_BLK, (j + 1) * COL_BLK)
        raw = raw_next
        raw_next = conv_dot(j + 1) if j + 1 < n_conv else None
        if j < n_gate:
            gate = _dot(xn, w_ref[:, GDN_CONV + j * COL_BLK:GDN_CONV + (j + 1) * COL_BLK])
            zs_ref[0, :, cols] = _silu(gate)
        for s in range(spt):
            rows = slice(s * seg, (s + 1) * seg)
            u = _silu(_conv_block(tail_scr, s, cols, raw[rows], n_valid, cw_ref, None, tailo_ref))
            if j * COL_BLK < 2 * GDN_KEY:
                is_q = j * COL_BLK < GDN_KEY
                dst = q_ref if is_q else k_ref
                off = j * COL_BLK - (0 if is_q else GDN_KEY)
                for i in range(COL_BLK // GDN_DK):
                    t = u[:, i * GDN_DK:(i + 1) * GDN_DK]
                    t = t * lax.rsqrt(jnp.sum(t * t, axis=-1, keepdims=True) + EPS)
                    if is_q:
                        t = t * (GDN_DK ** -0.5)
                    dst[0, rows, off + i * GDN_DK:off + (i + 1) * GDN_DK] = t
            else:
                off = j * COL_BLK - 2 * GDN_KEY
                v_ref[0, rows, off:off + COL_BLK] = u


def _inproj_call(body, x, tail0, spt, tm, n_valid, weights, consts, out_widths, conv_dim, name):
    n_grp, rows, _ = x.shape
    assert rows % tm == 0 and tm % spt == 0
    seg = tm // spt
    n_tiles = rows // tm
    assert (spt == 1 and n_valid == seg) or n_tiles == 1
    assert n_valid % TAIL == 0 and TAIL <= n_valid <= seg
    shared = tail0.shape[0] == 1
    tok_map = lambda g, t: (g, t, 0)
    tail_map = (lambda g, t: (0, 0, 0)) if shared else (lambda g, t: (g, 0, 0))
    in_specs = [pl.BlockSpec((1, tm, D_MODEL), tok_map), pl.BlockSpec((1, D_MODEL), lambda g, t: (0, 0))]
    args = [x, consts[0]]
    for w, layer in weights:
        if layer is None:
            in_specs.append(pl.BlockSpec(w.shape, lambda g, t: (0, 0), pipeline_mode=pl.Buffered(1)))
        else:
            in_specs.append(pl.BlockSpec((None,) + w.shape[1:], lambda g, t, layer=layer: (layer, 0, 0),
                                         pipeline_mode=pl.Buffered(1)))
        args.append(w)
    for cst in consts[1:]:
        in_specs.append(pl.BlockSpec(cst.shape, lambda g, t: (0, 0)))
        args.append(cst)
    in_specs.append(pl.BlockSpec((spt, TAIL, conv_dim), tail_map))
    args.append(tail0)
    out_specs = [pl.BlockSpec((1, tm, wd), tok_map) for wd in out_widths]
    out_shape = [jax.ShapeDtypeStruct((n_grp, rows, wd), F32) for wd in out_widths]
    out_specs.append(pl.BlockSpec((spt, TAIL, conv_dim), lambda g, t: (g, 0, 0)))
    out_shape.append(jax.ShapeDtypeStruct((n_grp * spt, TAIL, conv_dim), F32))
    return pl.pallas_call(
        functools.partial(body, spt, seg, n_valid),
        grid=(n_grp, n_tiles),
        in_specs=in_specs, out_specs=out_specs, out_shape=out_shape,
        scratch_shapes=[pltpu.VMEM((spt, TAIL, conv_dim), F32)],
        compiler_params=pltpu.CompilerParams(
            dimension_semantics=("parallel", "arbitrary"), vmem_limit_bytes=VMEM_LIMIT),
        name=name,
    )(*args)


def _outproj_kernel(final, y_ref, w_ref, h_ref, *rest):
    if final:
        fw_ref, o_ref = rest
    else:
        (o_ref,) = rest
    h = h_ref[...] + _dot(y_ref[...], w_ref[...])
    if final:
        ms = jnp.mean(h * h, axis=-1, keepdims=True)
        h = h * lax.rsqrt(ms + EPS) * fw_ref[...]
    o_ref[...] = h


def _outproj(y2d, w_stack, layer, h2d, tm, final_w=None):
    m, k = y2d.shape
    assert m % tm == 0
    final = final_w is not None
    in_specs = [
        pl.BlockSpec((tm, k), lambda i: (i, 0)),
        pl.BlockSpec((None, k, D_MODEL), lambda i: (layer, 0, 0), pipeline_mode=pl.Buffered(1)),
        pl.BlockSpec((tm, D_MODEL), lambda i: (i, 0)),
    ]
    args = [y2d, w_stack, h2d]
    if final:
        in_specs.append(pl.BlockSpec((1, D_MODEL), lambda i: (0, 0)))
        args.append(final_w.reshape(1, D_MODEL))
    return pl.pallas_call(
        functools.partial(_outproj_kernel, final),
        grid=(m // tm,),
        in_specs=in_specs,
        out_specs=pl.BlockSpec((tm, D_MODEL), lambda i: (i, 0)),
        out_shape=jax.ShapeDtypeStruct((m, D_MODEL), F32),
        compiler_params=pltpu.CompilerParams(
            dimension_semantics=("parallel",), vmem_limit_bytes=VMEM_LIMIT),
        name="outproj_final" if final else "outproj",
    )(*args)


def _ssd_kernel(nb, shared, n_valid, n_chunks,
                zs_ref, u_ref, dt_ref, s0_ref, alog_ref, dsk_ref, gw_ref, e3_ref, tri_ref, sel_ref,
                cmask_ref, bmask_ref,
                y_ref, sfin_ref, st_scr):
    L = CHUNK
    c = pl.program_id(1)
    seqs = range(nb)
    groups = range(SSD_G)
    gn = [slice(g * SSD_N, (g + 1) * SSD_N) for g in groups]
    gw = [slice(g * SSD_GW, (g + 1) * SSD_GW) for g in groups]

    @pl.when(c == 0)
    def _():
        for b in seqs:
            st_scr[b] = s0_ref[0 if shared else b].T

    bm_b = [u_ref[b, :, SSD_INNER:SSD_INNER + SSD_G * SSD_N].astype(BF16) for b in seqs]
    cm_b = [u_ref[b, :, SSD_INNER + SSD_G * SSD_N:SSD_CONV].astype(BF16) for b in seqs]

    dt4 = [dt_ref[b] for b in seqs]
    if n_valid < L:
        live = _iota((L, LANE), 0) < n_valid
        dt4 = [jnp.where(live, d, 0.0) for d in dt4]
    neg_a = -jnp.exp(alog_ref[...])
    cs4 = [_cumsum_rows(tri_ref[...], dt4[b] * neg_a) for b in seqs]
    stack3 = [_split3_merge(jnp.concatenate([cs4[b], dt4[b] * jnp.exp(cs4[b][L - 1:L, :] - cs4[b])], axis=0))
              for b in seqs]

    t2 = [_pair_rows(sel_ref[...], [cs4[b], dt4[b]]) for b in seqs]
    cb2 = [[_dot_nt(cm_b[b][:, gn[g]], jnp.concatenate([bm_b[b][:, gn[g]]] * 2, axis=0)) for b in seqs]
           for g in groups]

    slab = 4 * L
    n_slab = SSD_INNER // slab
    per_group = SSD_GW // slab

    def expand(j):
        return [_dot(stack3[b], e3_ref[:, j * slab:(j + 1) * slab]) for b in seqs]

    ex_next = expand(0)
    gated = [[] for _ in seqs]
    for j in range(n_slab):
        g = j // per_group
        lanes = slice(j * slab, (j + 1) * slab)
        ex = ex_next
        if j + 1 < n_slab:
            ex_next = expand(j + 1)
        st_old = [st_scr[b, :, lanes] for b in seqs]
        y_off = [_dot(cm_b[b][:, gn[g]], st_old[b].astype(BF16)) for b in seqs]
        xs, y_in, ecs, dtw = [], [], [], []
        for b in seqs:
            cs_x, dtw_x = ex[b][0:L], ex[b][L:2 * L]
            rows = [jnp.concatenate([jnp.broadcast_to(t2[b][2 * j + i:2 * j + i + 1, off:off + LANE], (L, LANE))
                                     for i in range(slab // LANE)], axis=1) for off in (0, LANE)]
            decay = jnp.exp(cs_x - rows[0] + cmask_ref[...]) * rows[1]
            m = (jnp.concatenate([cb2[g][b]] * (slab // LANE), axis=1) * decay).astype(BF16)
            x = u_ref[b, :, lanes]
            rhs = jnp.concatenate([x.astype(BF16)] * 4, axis=0) * bmask_ref[...]
            y_in.append(_dot(m, rhs))
            xs.append(x)
            ecs.append(jnp.exp(cs_x))
            dtw.append(dtw_x)
        for b in seqs:
            st_scr[b, :, lanes] = (st_old[b] * ecs[b][L - 1:L, :]
                                   + _dot_tn(bm_b[b][:, gn[g]], (xs[b] * dtw[b]).astype(BF16)))
            y = y_in[b] + y_off[b] * ecs[b] + xs[b] * dsk_ref[:, lanes]
            gated[b].append(y * zs_ref[b, :, lanes])
        if (j + 1) % per_group == 0:
            for b in seqs:
                blk_g = jnp.concatenate(gated[b], axis=1)
                gated[b] = []
                ms = jnp.mean(blk_g * blk_g, axis=-1, keepdims=True)
                y_ref[b, :, gw[g]] = (blk_g * lax.rsqrt(ms + EPS) * gw_ref[:, gw[g]]).astype(BF16)

    @pl.when(c == n_chunks - 1)
    def _():
        for b in seqs:
            sfin_ref[b] = st_scr[b].T


def _tri3():
    t = jnp.arange(CHUNK)
    tri = (t[:, None] >= t[None, :]).astype(BF16)
    return jnp.concatenate([tri, tri, tri], axis=1)


def _pair_sel(n_rows, n_pairs):
    j = jnp.arange(n_rows)[:, None]
    kk = jnp.arange(LANE)[None, :]
    return (((kk % REP) // 2 == j) & (kk < 3 * REP) & (j < n_pairs)).astype(BF16)


def _ssd_consts():
    k = jnp.arange(LANE)[:, None]
    col = jnp.arange(SSD_INNER)[None, :]
    e3 = ((k % REP == col // SSD_P) & (k < 3 * REP)).astype(BF16)
    t = jnp.arange(CHUNK)
    tri, sel = _tri3(), _pair_sel(SSD_H // 2, SSD_H // 2)
    b = jnp.arange(4 * CHUNK)
    cmask = jnp.where(t[:, None] >= (b[None, :] % CHUNK), 0.0, NEG_BIG).astype(F32)
    bmask = ((b[:, None] // CHUNK) == (b[None, :] // CHUNK)).astype(BF16)
    return e3, tri, sel, cmask, bmask


def _seqs_per_step(n_seq):
    return 2 if n_seq % 2 == 0 else (3 if n_seq % 3 == 0 else 1)


def _ssd_mixer(zs, u, dt, s0, p, n_valid):
    n_seq, t_len, _ = zs.shape
    n_chunks = t_len // CHUNK
    nb = _seqs_per_step(n_seq)
    shared = s0.shape[0] == 1
    init_spec = (pl.BlockSpec((1, SSD_INNER, SSD_N), lambda s, c: (0, 0, 0)) if shared
                 else pl.BlockSpec((nb, SSD_INNER, SSD_N), lambda s, c: (s, 0, 0)))
    tok_map = lambda s, c: (s, c, 0)
    const2 = lambda s, c: (0, 0)
    consts = (p["a_log4"], p["d_x"], p["gnorm_w"]) + _ssd_consts()
    return pl.pallas_call(
        functools.partial(_ssd_kernel, nb, shared, n_valid, n_chunks),
        grid=(n_seq // nb, n_chunks),
        in_specs=[
            pl.BlockSpec((nb, CHUNK, SSD_INNER), tok_map),
            pl.BlockSpec((nb, CHUNK, SSD_CONV), tok_map),
            pl.BlockSpec((nb, CHUNK, LANE), tok_map),
            init_spec,
        ] + [pl.BlockSpec(cst.shape, const2) for cst in consts],
        out_specs=[
            pl.BlockSpec((nb, CHUNK, SSD_INNER), tok_map),
            pl.BlockSpec((nb, SSD_INNER, SSD_N), lambda s, c: (s, 0, 0)),
        ],
        out_shape=[
            jax.ShapeDtypeStruct((n_seq, t_len, SSD_INNER), BF16),
            jax.ShapeDtypeStruct((n_seq, SSD_INNER, SSD_N), F32),
        ],
        scratch_shapes=[pltpu.VMEM((nb, SSD_N, SSD_INNER), F32)],
        compiler_params=pltpu.CompilerParams(
            dimension_semantics=("parallel", "arbitrary"), vmem_limit_bytes=VMEM_LIMIT),
        name="ssd_mixer",
    )(zs, u, dt, s0, *consts)


def _pair_blockdiag(y2, bd):
    y16 = y2.astype(BF16)
    return jnp.concatenate([y16, y16], axis=0) * bd


def _unit_lower_inverses(n_list, eye, same16, same32, bd):
    def mm(a_list, b_list):
        return [_dot(a.astype(BF16), _pair_blockdiag(b, bd)) for a, b in zip(a_list, b_list)]

    def axpy(t_list, d_list, sign):
        return [t + sign * d for t, d in zip(t_list, d_list)]

    nd = [n * same16 for n in n_list]
    t = [eye - x for x in nd]
    pw = mm(nd, nd)
    for step in range(3):
        t = axpy(t, mm(t, pw), 1.0)
        if step < 2:
            pw = mm(pw, pw)
    n1 = [n * (same32 - same16) for n in n_list]
    t = axpy(t, mm(t, mm(n1, t)), -1.0)
    n2 = [n * (1.0 - same32) for n in n_list]
    t = axpy(t, mm(t, mm(n2, t)), -1.0)
    return t


def _gdn_kernel(nb, nch, shared, n_valid, n_steps,
                q_ref, k_ref, v_ref, zs_ref, g_ref, beta_ref, s0_ref, ow_ref, e3_ref, tri_ref, sel_ref, bd_ref,
                masks_ref,
                y_ref, sfin_ref, st_scr):
    L = CHUNK
    c = pl.program_id(1)
    seqs = range(nb * nch)

    def tok(ref, b, cols):
        return ref[b % nb, (b // nb) * L:(b // nb + 1) * L, cols]

    @pl.when(c == 0)
    def _():
        for s in range(nb):
            st_scr[s] = s0_ref[0 if shared else s]

    g = [tok(g_ref, b, slice(None)) for b in seqs]
    beta = [tok(beta_ref, b, slice(None)) for b in seqs]
    if n_valid < L:
        live = _iota((L, LANE), 0) < n_valid
        g = [jnp.where(live, x, 0.0) for x in g]
        beta = [jnp.where(live, x, 0.0) for x in beta]
    gc = [_cumsum_rows(tri_ref[...], g[b]) for b in seqs]
    stack = []
    for b in seqs:
        stack += [gc[b], beta[b]]
    ex = _dot(_split3_merge(jnp.concatenate(stack, axis=0)), e3_ref[...])
    gc_x = [ex[(2 * b) * L:(2 * b + 1) * L] for b in seqs]
    beta_x = [ex[(2 * b + 1) * L:(2 * b + 2) * L] for b in seqs]
    egc_x = [jnp.exp(x) for x in gc_x]
    egl_x = [jnp.exp(x[L - 1:L, :] - x) for x in gc_x]
    lane = _iota((L, LANE), 1)
    gc_t2 = [_pair_rows(sel_ref[...], [gc[b]]) for b in seqs]

    incl_add = masks_ref[0]
    strict = masks_ref[1]
    eye = masks_ref[2]
    same16 = masks_ref[3]
    same32 = masks_ref[4]
    bd = bd_ref[...]
    first = lane < L

    chains = [(b, h) for h in range(GDN_H) for b in seqs]
    pairs = [(b, p) for p in range(GDN_H // 2) for b in seqs]
    chain_of = lambda b, h: h * len(seqs) + b
    ks = [slice(h * GDN_DK, (h + 1) * GDN_DK) for h in range(GDN_H)]
    vs = [slice(h * GDN_DV, (h + 1) * GDN_DV) for h in range(GDN_H)]
    q = [tok(q_ref, b, ks[h]) for b, h in chains]
    k = [tok(k_ref, b, ks[h]) for b, h in chains]
    egc = [egc_x[b][:, ks[h]] for b, h in chains]
    kb = [k[i] * beta_x[b][:, ks[h]] for i, (b, h) in enumerate(chains)]
    n_ch = range(len(chains))
    zero_k = jnp.zeros((L, GDN_DK), BF16)

    kq2, dec2 = [], []
    for b, p in pairs:
        ia, ib = chain_of(b, 2 * p), chain_of(b, 2 * p + 1)
        lhs = jnp.concatenate([jnp.concatenate([kb[ia], kb[ib]], axis=1),
                               jnp.concatenate([q[ia], q[ib]], axis=1)], axis=0).astype(BF16)
        rhs_nt = jnp.concatenate([jnp.concatenate([k[ia].astype(BF16), zero_k], axis=1),
                                  jnp.concatenate([zero_k, k[ib].astype(BF16)], axis=1)], axis=0)
        kq2.append(_dot_nt(lhs, rhs_nt))
        col2 = jnp.where(first, gc_x[b][:, ks[2 * p]], gc_x[b][:, ks[2 * p + 1]])
        dec2.append(jnp.exp(col2 - jnp.broadcast_to(gc_t2[b][p:p + 1, :], (L, LANE)) + incl_add))
    n_pr = range(len(pairs))
    t_inv2 = _unit_lower_inverses([kq2[j][0:L] * dec2[j] * strict for j in n_pr], eye, same16, same32, bd)

    zero_r = jnp.zeros((L, GDN_DV + GDN_DK), BF16)
    uw = [None] * len(chains)
    for j, (b, p) in enumerate(pairs):
        halves = []
        for h in (2 * p, 2 * p + 1):
            i = chain_of(b, h)
            beta_h = beta_x[b][:, ks[h]]
            halves.append(jnp.concatenate([tok(v_ref, b, vs[h]) * jnp.concatenate([beta_h, beta_h], axis=1),
                                           kb[i] * egc[i]], axis=1).astype(BF16))
        rhs_bd = jnp.concatenate([jnp.concatenate([halves[0], zero_r], axis=1),
                                  jnp.concatenate([zero_r, halves[1]], axis=1)], axis=0)
        uw2 = _dot(t_inv2[j].astype(BF16), rhs_bd)
        width = GDN_DV + GDN_DK
        uw[chain_of(b, 2 * p)] = uw2[:, 0:width]
        uw[chain_of(b, 2 * p + 1)] = uw2[:, width:2 * width]
    zero_v = jnp.zeros((L, GDN_DV), BF16)
    for cc in range(nch):
        mine = [(i, b, h) for i, (b, h) in enumerate(chains) if b // nb == cc]
        s_old = {i: st_scr[b % nb, h] for i, b, h in mine}
        wq = {i: _dot(jnp.concatenate([uw[i][:, GDN_DV:], q[i] * egc[i]], axis=0).astype(BF16),
                      s_old[i].astype(BF16)) for i, b, h in mine}
        v_new = {i: (uw[i][:, :GDN_DV] - wq[i][0:L]).astype(BF16) for i, b, h in mine}
        o = {}
        for j, (b, p) in enumerate(pairs):
            if b // nb != cc:
                continue
            ia, ib = chain_of(b, 2 * p), chain_of(b, 2 * p + 1)
            v_bd = jnp.concatenate([jnp.concatenate([v_new[ia], zero_v], axis=1),
                                    jnp.concatenate([zero_v, v_new[ib]], axis=1)], axis=0)
            o2 = _dot((kq2[j][L:2 * L] * dec2[j]).astype(BF16), v_bd)
            o[ia] = wq[ia][L:2 * L] + o2[:, 0:GDN_DV]
            o[ib] = wq[ib][L:2 * L] + o2[:, GDN_DV:]
        for i, b, h in mine:
            e_last = egc[i][L - 1:L, :]
            st_scr[b % nb, h] = (s_old[i] * jnp.concatenate([e_last, e_last], axis=1)
                                 + _dot_tn((k[i] * egl_x[b][:, ks[h]]).astype(BF16), v_new[i]))
        for i, b, h in mine:
            o_h = o[i] * lax.rsqrt(jnp.mean(o[i] * o[i], axis=-1, keepdims=True) + EPS) * ow_ref[...]
            y_ref[b % nb, cc * L:(cc + 1) * L, vs[h]] = (o_h * tok(zs_ref, b, vs[h])).astype(BF16)

    @pl.when(c == n_steps - 1)
    def _():
        for s in range(nb):
            sfin_ref[s] = st_scr[s]


def _gdn_consts():
    k = jnp.arange(LANE)[:, None]
    col = jnp.arange(GDN_KEY)[None, :]
    e3 = ((k % REP == col // GDN_DK) & (k < 3 * REP)).astype(BF16)
    t = jnp.arange(CHUNK)
    r, cc = t[:, None], t[None, :]
    tri, sel = _tri3(), _pair_sel(2 * GDN_H, GDN_H // 2)
    masks = jnp.stack([
        jnp.where(r >= cc, 0.0, NEG_BIG),
        (r > cc).astype(F32),
        (r == cc).astype(F32),
        ((r // (CHUNK // 4)) == (cc // (CHUNK // 4))).astype(F32),
        ((r // (CHUNK // 2)) == (cc // (CHUNK // 2))).astype(F32),
    ]).astype(F32)
    masks = jnp.concatenate([masks, masks], axis=-1)
    b = jnp.arange(2 * CHUNK)
    bd = ((b[:, None] // CHUNK) == (b[None, :] // CHUNK)).astype(BF16)
    return e3, tri, sel, bd, masks


def _gdn_mixer(q, k, v, zs, g, beta, s0, p, n_valid):
    n_seq, t_len, _ = zs.shape
    n_chunks = t_len // CHUNK
    nb = _seqs_per_step(n_seq)
    nch = GDN_CHUNKS_PER_STEP if (nb <= 2 and n_chunks % GDN_CHUNKS_PER_STEP == 0) else 1
    rows = nch * CHUNK
    shared = s0.shape[0] == 1
    init_spec = (pl.BlockSpec((1, GDN_H, GDN_DK, GDN_DV), lambda s, c: (0, 0, 0, 0)) if shared
                 else pl.BlockSpec((nb, GDN_H, GDN_DK, GDN_DV), lambda s, c: (s, 0, 0, 0)))
    tok_map = lambda s, c: (s, c, 0)
    e3, tri, sel, bd, masks = _gdn_consts()
    consts2 = (p["onorm_w"], e3, tri, sel, bd)
    return pl.pallas_call(
        functools.partial(_gdn_kernel, nb, nch, shared, n_valid, n_chunks // nch),
        grid=(n_seq // nb, n_chunks // nch),
        in_specs=[
            pl.BlockSpec((nb, rows, GDN_KEY), tok_map),
            pl.BlockSpec((nb, rows, GDN_KEY), tok_map),
            pl.BlockSpec((nb, rows, GDN_VAL), tok_map),
            pl.BlockSpec((nb, rows, GDN_VAL), tok_map),
            pl.BlockSpec((nb, rows, LANE), tok_map),
            pl.BlockSpec((nb, rows, LANE), tok_map),
            init_spec,
        ] + [pl.BlockSpec(cst.shape, lambda s, c: (0, 0)) for cst in consts2]
          + [pl.BlockSpec(masks.shape, lambda s, c: (0, 0, 0))],
        out_specs=[
            pl.BlockSpec((nb, rows, GDN_VAL), tok_map),
            pl.BlockSpec((nb, GDN_H, GDN_DK, GDN_DV), lambda s, c: (s, 0, 0, 0)),
        ],
        out_shape=[
            jax.ShapeDtypeStruct((n_seq, t_len, GDN_VAL), BF16),
            jax.ShapeDtypeStruct((n_seq, GDN_H, GDN_DK, GDN_DV), F32),
        ],
        scratch_shapes=[pltpu.VMEM((nb, GDN_H, GDN_DK, GDN_DV), F32)],
        compiler_params=pltpu.CompilerParams(
            dimension_semantics=("parallel", "arbitrary"), vmem_limit_bytes=VMEM_LIMIT),
        name="gdn_mixer",
    )(q, k, v, zs, g, beta, s0, *consts2, masks)


def _rep_lanes(v, n_heads):
    n_rep = LANE // REP if n_heads == REP else 3
    row = jnp.pad(v, (0, REP - n_heads))
    return jnp.pad(jnp.tile(row, n_rep), (0, LANE - n_rep * REP)).reshape(1, LANE)


def _rep_cols(w, n_heads):
    n_rep = LANE // REP if n_heads == REP else 3
    blk = jnp.pad(w, ((0, 0), (0, REP - n_heads)))
    return jnp.pad(jnp.tile(blk, (1, n_rep)), ((0, 0), (0, LANE - n_rep * REP)))


def _ssd_state_in(s):
    return s.reshape(s.shape[0], SSD_INNER, SSD_N)


def _ssd_state_out(s):
    return s.reshape(s.shape[0], SSD_H, SSD_P, SSD_N)


def _tail_from_rows(rows3):
    return jnp.pad(rows3, ((0, 0), (TAIL - (CONV_K - 1), 0), (0, 0)))


def _segment_valid(n_valid, tm_in, spt):
    return tm_in // spt if n_valid == CHUNK else n_valid


def _ssd_layer(h, tail0, s0, p, layer, spt, tm_in, tm_out, n_valid, final_w=None):
    n_grp, rows, _ = h.shape
    weights = [(p["w_in"], layer), (p["w_dt4"][layer], None)]
    consts = [p["norm_w"][layer].reshape(1, D_MODEL), p["conv_w"][layer], p["conv_b"][layer].reshape(1, SSD_CONV),
              p["dt_bias4"][layer]]
    zs, u, dt, tails = _inproj_call(_ssd_inproj_kernel, h, tail0, spt, tm_in, _segment_valid(n_valid, tm_in, spt),
                                    weights, consts, (SSD_INNER, SSD_CONV, LANE), SSD_CONV, "ssd_inproj")
    seq = lambda a: a.reshape(n_grp * spt, rows // spt, a.shape[-1])
    mp = dict(a_log4=p["a_log4"][layer], d_x=p["d_x"][layer], gnorm_w=p["gnorm_w"][layer])
    y, s_fin = _ssd_mixer(seq(zs), seq(u), seq(dt), s0, mp, n_valid)
    h_new = _outproj(y.reshape(n_grp * rows, SSD_INNER), p["w_out"], layer, h.reshape(n_grp * rows, D_MODEL),
                     tm_out, final_w)
    return h_new.reshape(h.shape), tails, s_fin


def _gdn_layer(h, tail0, s0, p, layer, spt, tm_in, tm_out, n_valid, final_w=None):
    n_grp, rows, _ = h.shape
    weights = [(p["w_in"], layer), (p["w_ab"][layer], None)]
    consts = [p["norm_w"][layer].reshape(1, D_MODEL), p["conv_w"][layer], p["dt_bias"][layer], p["a_log"][layer]]
    q, k, v, zs, g, beta, tails = _inproj_call(
        _gdn_inproj_kernel, h, tail0, spt, tm_in, _segment_valid(n_valid, tm_in, spt), weights, consts,
        (GDN_KEY, GDN_KEY, GDN_VAL, GDN_VAL, LANE, LANE), GDN_CONV, "gdn_inproj")
    seq = lambda a: a.reshape(n_grp * spt, rows // spt, a.shape[-1])
    mp = dict(onorm_w=p["onorm_w"][layer].reshape(1, GDN_DV))
    y, s_fin = _gdn_mixer(seq(q), seq(k), seq(v), seq(zs), seq(g), seq(beta), s0, mp, n_valid)
    h_new = _outproj(y.reshape(n_grp * rows, GDN_VAL), p["w_out"], layer, h.reshape(n_grp * rows, D_MODEL),
                     tm_out, final_w)
    return h_new.reshape(h.shape), tails, s_fin


def kernel(x_prompt, x_sample, state_ssd, state_ssd_conv, state_gdn, state_gdn_conv, meta_tokens,
           ssd_norm_w, ssd_w_in, ssd_conv_w, ssd_conv_b, ssd_dt_bias, ssd_a_log, ssd_d, ssd_gnorm_w, ssd_w_out,
           gdn_norm_w, gdn_w_in, gdn_conv_w, gdn_dt_bias, gdn_a_log, gdn_onorm_w, gdn_w_out, final_norm_w):
    n_ssd, n_gdn = ssd_norm_w.shape[0], gdn_norm_w.shape[0]
    depth = n_ssd + n_gdn
    n_dec, dec_t, _ = x_sample.shape
    assert dec_t == N_META and N_META <= CHUNK and x_prompt.shape[1] % TM_PROMPT_OUT == 0
    keep = CONV_K - 1

    ssd_p = dict(
        norm_w=ssd_norm_w, w_in=_cast_cols(ssd_w_in, SSD_INNER + SSD_CONV), w_out=_cast_cols(ssd_w_out, D_MODEL),
        w_dt4=[_rep_cols(ssd_w_in[j][:, SSD_INNER + SSD_CONV:], SSD_H).astype(BF16) for j in range(n_ssd)],
        conv_w=ssd_conv_w, conv_b=ssd_conv_b,
        dt_bias4=[_rep_lanes(ssd_dt_bias[j], SSD_H) for j in range(n_ssd)],
        a_log4=[_rep_lanes(ssd_a_log[j], SSD_H) for j in range(n_ssd)],
        d_x=[jnp.repeat(ssd_d[j], SSD_P).reshape(1, SSD_INNER) for j in range(n_ssd)],
        gnorm_w=[ssd_gnorm_w[j].reshape(1, SSD_INNER) for j in range(n_ssd)],
    )
    ab0 = GDN_CONV + GDN_VAL
    gdn_p = dict(
        norm_w=gdn_norm_w, w_in=_cast_cols(gdn_w_in, GDN_CONV + GDN_VAL), w_out=_cast_cols(gdn_w_out, D_MODEL),
        w_ab=[jnp.concatenate([_rep_cols(gdn_w_in[j][:, ab0:ab0 + GDN_H], GDN_H),
                               _rep_cols(gdn_w_in[j][:, ab0 + GDN_H:], GDN_H)], axis=1).astype(BF16)
              for j in range(n_gdn)],
        conv_w=gdn_conv_w,
        dt_bias=[_rep_lanes(gdn_dt_bias[j], GDN_H) for j in range(n_gdn)],
        a_log=[_rep_lanes(gdn_a_log[j], GDN_H) for j in range(n_gdn)],
        onorm_w=gdn_onorm_w,
    )

    n_small = n_dec + 1
    hs = jnp.concatenate([x_sample, meta_tokens.astype(x_sample.dtype)[None]], axis=0)
    hs = jnp.pad(hs, ((0, 0), (0, CHUNK - N_META), (0, 0))).reshape(1, n_small * CHUNK, D_MODEL)
    hp = x_prompt
    tm_small = n_small * CHUNK
    last_rows = lambda tails: tails[:, TAIL - keep:]
    outs = {k: [] for k in ("p_ssd", "p_ssd_conv", "p_gdn", "p_gdn_conv", "s_ssd", "s_ssd_conv", "s_gdn", "s_gdn_conv")}
    for i in range(depth):
        j = i // 2
        fw = final_norm_w if i == depth - 1 else None
        if i % 2 == 0:
            tail_s = _tail_from_rows(jnp.concatenate([state_ssd_conv[j], jnp.zeros((1, keep, SSD_CONV), F32)], axis=0))
            s0_s = _ssd_state_in(jnp.concatenate([state_ssd[j], jnp.zeros((1,) + state_ssd.shape[2:], F32)], axis=0))
            hs, tails_s, sfin_s = _ssd_layer(hs, tail_s, s0_s, ssd_p, j, n_small, tm_small, tm_small, N_META, fw)
            hp, tails_p, sfin_p = _ssd_layer(hp, tails_s[n_dec:], sfin_s[n_dec:], ssd_p, j, 1, TM_PROMPT_IN,
                                             TM_PROMPT_OUT, CHUNK, fw)
            outs["s_ssd"].append(_ssd_state_out(sfin_s[:n_dec]))
            outs["p_ssd"].append(_ssd_state_out(sfin_p))
            outs["s_ssd_conv"].append(last_rows(tails_s[:n_dec]))
            outs["p_ssd_conv"].append(last_rows(tails_p))
        else:
            tail_s = _tail_from_rows(jnp.concatenate([state_gdn_conv[j], jnp.zeros((1, keep, GDN_CONV), F32)], axis=0))
            s0_s = jnp.concatenate([state_gdn[j], jnp.zeros((1,) + state_gdn.shape[2:], F32)], axis=0)
            hs, tails_s, sfin_s = _gdn_layer(hs, tail_s, s0_s, gdn_p, j, n_small, tm_small, tm_small, N_META, fw)
            hp, tails_p, sfin_p = _gdn_layer(hp, tails_s[n_dec:], sfin_s[n_dec:], gdn_p, j, 1, TM_PROMPT_IN,
                                             TM_PROMPT_OUT, CHUNK, fw)
            outs["s_gdn"].append(sfin_s[:n_dec])
            outs["p_gdn"].append(sfin_p)
            outs["s_gdn_conv"].append(last_rows(tails_s[:n_dec]))
            outs["p_gdn_conv"].append(last_rows(tails_p))
    y_sample = hs.reshape(n_small, CHUNK, D_MODEL)[:n_dec, :N_META]
    st = lambda key: jnp.stack(outs[key])
    return (hp, y_sample, st("p_ssd"), st("p_ssd_conv"), st("p_gdn"), st("p_gdn_conv"),
            st("s_ssd"), st("s_ssd_conv"), st("s_gdn"), st("s_gdn_conv"))
```

```python
import functools

import jax
import jax.numpy as jnp
import numpy as np
from jax import lax
from jax.experimental import pallas as pl
from jax.experimental.pallas import tpu as pltpu

F32 = jnp.float32
BF16 = jnp.bfloat16

D_MODEL = 1024
N_META = 16
CONV_K = 4
EPS = 1e-6
CHUNK = 64
TAIL = 8
LANE = 128
COL_BLK = 512
SSD_INNER = 2048
SSD_P = 64
SSD_H = 32
SSD_G = 4
SSD_N = 128
SSD_GW = SSD_INNER // SSD_G
SSD_CONV = SSD_INNER + 2 * SSD_G * SSD_N
GDN_DK = 128
GDN_DV = 256
GDN_H = 8
GDN_KEY = GDN_H * GDN_DK
GDN_VAL = GDN_H * GDN_DV
GDN_CONV = 2 * GDN_KEY + GDN_VAL
REP = 32
NEG_BIG = -1e30

VMEM_LIMIT = 48 * 1024 * 1024
TM_PROMPT_IN = 256
TM_PROMPT_OUT = 1024
GDN_CHUNKS_PER_STEP = 4


NEG_LOG2E = -1.4426950408889634


def _sigmoid(x):
    return 1.0 / (1.0 + jnp.exp2(x * NEG_LOG2E))


def _silu(x):
    return x * _sigmoid(x)


def _softplus(x):
    return jnp.maximum(x, 0.0) + jnp.log1p(jnp.exp(-jnp.abs(x)))


def _dot(a, b, precision=None):
    return jnp.dot(a, b, preferred_element_type=F32, precision=precision)


def _dot_nt(a, b, precision=None):
    return lax.dot_general(a, b, (((1,), (1,)), ((), ())), preferred_element_type=F32, precision=precision)


def _dot_tn(a, b):
    return lax.dot_general(a, b, (((0,), (0,)), ((), ())), preferred_element_type=F32)


def _iota(shape, dim):
    return lax.broadcasted_iota(jnp.int32, shape, dim)


def _pad_rows(x, n_rows):
    if x.shape[0] == n_rows:
        return x
    return jnp.concatenate([x, jnp.zeros((n_rows - x.shape[0], x.shape[1]), x.dtype)], axis=0)


def _split3(x):
    hi = x.astype(BF16)
    r1 = x - hi.astype(F32)
    mid = r1.astype(BF16)
    lo = (r1 - mid.astype(F32)).astype(BF16)
    return hi, mid, lo


def _split3_merge(x):
    hi, mid, lo = _split3(x)
    lane = _iota(x.shape, 1)
    zero = jnp.zeros_like(hi)
    return jnp.where(lane < REP, hi, jnp.where(lane < 2 * REP, mid, jnp.where(lane < 3 * REP, lo, zero)))


def _cumsum_rows(tri3, x):
    return _dot(tri3, jnp.concatenate(_split3(x), axis=0))


def _pair_rows(sel3, values):
    odd = (_iota(values[0].shape, 1) & 1) == 1
    zero = jnp.zeros(values[0].shape, BF16)
    rows = []
    for v in values:
        m3 = _split3_merge(v)
        rows += [jnp.where(odd, zero, m3), jnp.where(odd, m3, zero)]
    return _dot_nt(sel3, jnp.concatenate(rows, axis=0))


def _rmsnorm_bf16(x, w):
    ms = jnp.mean(x * x, axis=-1, keepdims=True)
    return (x * lax.rsqrt(ms + EPS) * w).astype(BF16)


def _conv_block(tail_scr, s, cols, raw, n_valid, cw_ref, bias, tailo_ref):
    seg = raw.shape[0]
    x = [tail_scr[s, :, cols]] + [raw[i * TAIL:(i + 1) * TAIL] for i in range(seg // TAIL)]
    last = raw[n_valid - TAIL:n_valid]
    tail_scr[s, :, cols] = last
    tailo_ref[s, :, cols] = last
    row = _iota((TAIL, raw.shape[1]), 0)

    def shift_down(tiles, d):
        mixed = [tiles[0]] + [jnp.where(row >= TAIL - d, tiles[i - 1], tiles[i]) for i in range(1, len(tiles))]
        return [pltpu.roll(t, d, 0) for t in mixed]

    assert CONV_K == 4
    w = [cw_ref[k:k + 1, cols] for k in range(CONV_K)]
    x1 = shift_down(x, 1)
    near = [t * w[3] + t1 * w[2] for t, t1 in zip(x, x1)]
    far = shift_down([t * w[1] + t1 * w[0] for t, t1 in zip(x, x1)], 2)
    acc = jnp.concatenate([a + b for a, b in zip(near[1:], far[1:])], axis=0)
    if bias is not None:
        acc = acc + bias
    return acc


def _ssd_inproj_kernel(spt, seg, n_valid, x_ref, nw_ref, w_ref, wdt_ref, cw_ref, cb_ref, dtb_ref, tail_ref,
                       zs_ref, u_ref, dt_ref, tailo_ref, tail_scr):
    @pl.when(pl.program_id(1) == 0)
    def _():
        tail_scr[...] = tail_ref[...]

    xn = _rmsnorm_bf16(x_ref[0], nw_ref[...])
    dt_ref[0] = _softplus(_dot(xn, wdt_ref[...]) + dtb_ref[...])

    def conv_dot(j):
        return _dot(xn, w_ref[:, SSD_INNER + j * COL_BLK:SSD_INNER + (j + 1) * COL_BLK])

    n_conv, n_gate = SSD_CONV // COL_BLK, SSD_INNER // COL_BLK
    raw = conv_dot(0)
    for j in range(n_conv):
        cols = slice(j * COL_BLK, (j + 1) * COL_BLK)
        raw_next = conv_dot(j + 1) if j + 1 < n_conv else None
        gate = _dot(xn, w_ref[:, cols]) if j < n_gate else None
        for s in range(spt):
            acc = _conv_block(tail_scr, s, cols, raw[s * seg:(s + 1) * seg], n_valid, cw_ref, cb_ref[:, cols],
                              tailo_ref)
            u_ref[0, s * seg:(s + 1) * seg, cols] = _silu(acc)
        if gate is not None:
            zs_ref[0, :, cols] = _silu(gate)
        raw = raw_next


def _gdn_inproj_kernel(spt, seg, n_valid, x_ref, nw_ref, w_ref, wab_ref, cw_ref, dtb_ref, alog_ref, tail_ref,
                       q_ref, k_ref, v_ref, zs_ref, g_ref, beta_ref, tailo_ref, tail_scr):
    @pl.when(pl.program_id(1) == 0)
    def _():
        tail_scr[...] = tail_ref[...]

    xn = _rmsnorm_bf16(x_ref[0], nw_ref[...])
    ab = _dot(xn, wab_ref[...])
    lane = _iota((1, LANE), 1)
    head_lane = ((lane & (REP - 1)) < GDN_H) & (lane < 3 * REP)
    coef = jnp.where(head_lane, -jnp.exp(alog_ref[...]), 0.0)
    g_ref[0] = coef * _softplus(ab[:, :LANE] + dtb_ref[...])
    beta_ref[0] = _sigmoid(ab[:, LANE:])
    def conv_dot(j):
        return _dot(xn, w_ref[:, j * COL_BLK:(j + 1) * COL_BLK])

    n_conv, n_gate = GDN_CONV // COL_BLK, GDN_VAL // COL_BLK
    raw_next = conv_dot(0)
    for j in range(n_conv):
        cols = slice(j * COL_BLK, (j + 1) * COL_BLK)
        raw = raw_next
        raw_next = conv_dot(j + 1) if j + 1 < n_conv else None
        if j < n_gate:
            gate = _dot(xn, w_ref[:, GDN_CONV + j * COL_BLK:GDN_CONV + (j + 1) * COL_BLK])
            zs_ref[0, :, cols] = _silu(gate)
        for s in range(spt):
            rows = slice(s * seg, (s + 1) * seg)
            u = _silu(_conv_block(tail_scr, s, cols, raw[rows], n_valid, cw_ref, None, tailo_ref))
            if j * COL_BLK < 2 * GDN_KEY:
                is_q = j * COL_BLK < GDN_KEY
                dst = q_ref if is_q else k_ref
                off = j * COL_BLK - (0 if is_q else GDN_KEY)
                for i in range(COL_BLK // GDN_DK):
                    t = u[:, i * GDN_DK:(i + 1) * GDN_DK]
                    t = t * lax.rsqrt(jnp.sum(t * t, axis=-1, keepdims=True) + EPS)
                    if is_q:
                        t = t * (GDN_DK ** -0.5)
                    dst[0, rows, off + i * GDN_DK:off + (i + 1) * GDN_DK] = t
            else:
                off = j * COL_BLK - 2 * GDN_KEY
                v_ref[0, rows, off:off + COL_BLK] = u


def _inproj_call(body, x, tail0, spt, tm, n_valid, weights, consts, out_widths, conv_dim, name):
    n_grp, rows, _ = x.shape
    assert rows % tm == 0 and tm % spt == 0
    seg = tm // spt
    n_tiles = rows // tm
    assert (spt == 1 and n_valid == seg) or n_tiles == 1
    assert n_valid % TAIL == 0 and TAIL <= n_valid <= seg
    shared = tail0.shape[0] == 1
    tok_map = lambda g, t: (g, t, 0)
    tail_map = (lambda g, t: (0, 0, 0)) if shared else (lambda g, t: (g, 0, 0))
    in_specs = [pl.BlockSpec((1, tm, D_MODEL), tok_map), pl.BlockSpec((1, D_MODEL), lambda g, t: (0, 0))]
    args = [x, consts[0]]
    for w, layer in weights:
        if layer is None:
            in_specs.append(pl.BlockSpec(w.shape, lambda g, t: (0, 0), pipeline_mode=pl.Buffered(1)))
        else:
            in_specs.append(pl.BlockSpec((None,) + w.shape[1:], lambda g, t, layer=layer: (layer, 0, 0),
                                         pipeline_mode=pl.Buffered(1)))
        args.append(w)
    for cst in consts[1:]:
        in_specs.append(pl.BlockSpec(cst.shape, lambda g, t: (0, 0)))
        args.append(cst)
    in_specs.append(pl.BlockSpec((spt, TAIL, conv_dim), tail_map))
    args.append(tail0)
    out_specs = [pl.BlockSpec((1, tm, wd), tok_map) for wd in out_widths]
    out_shape = [jax.ShapeDtypeStruct((n_grp, rows, wd), F32) for wd in out_widths]
    out_specs.append(pl.BlockSpec((spt, TAIL, conv_dim), lambda g, t: (g, 0, 0)))
    out_shape.append(jax.ShapeDtypeStruct((n_grp * spt, TAIL, conv_dim), F32))
    return pl.pallas_call(
        functools.partial(body, spt, seg, n_valid),
        grid=(n_grp, n_tiles),
        in_specs=in_specs, out_specs=out_specs, out_shape=out_shape,
        scratch_shapes=[pltpu.VMEM((spt, TAIL, conv_dim), F32)],
        compiler_params=pltpu.CompilerParams(
            dimension_semantics=("parallel", "arbitrary"), vmem_limit_bytes=VMEM_LIMIT),
        name=name,
    )(*args)


def _outproj_kernel(final, y_ref, w_ref, h_ref, *rest):
    if final:
        fw_ref, o_ref = rest
    else:
        (o_ref,) = rest
    h = h_ref[...] + _dot(y_ref[...], w_ref[...])
    if final:
        ms = jnp.mean(h * h, axis=-1, keepdims=True)
        h = h * lax.rsqrt(ms + EPS) * fw_ref[...]
    o_ref[...] = h


def _outproj(y2d, w_stack, layer, h2d, tm, final_w=None):
    m, k = y2d.shape
    assert m % tm == 0
    final = final_w is not None
    in_specs = [
        pl.BlockSpec((tm, k), lambda i: (i, 0)),
        pl.BlockSpec((None, k, D_MODEL), lambda i: (layer, 0, 0), pipeline_mode=pl.Buffered(1)),
        pl.BlockSpec((tm, D_MODEL), lambda i: (i, 0)),
    ]
    args = [y2d, w_stack, h2d]
    if final:
        in_specs.append(pl.BlockSpec((1, D_MODEL), lambda i: (0, 0)))
        args.append(final_w.reshape(1, D_MODEL))
    return pl.pallas_call(
        functools.partial(_outproj_kernel, final),
        grid=(m // tm,),
        in_specs=in_specs,
        out_specs=pl.BlockSpec((tm, D_MODEL), lambda i: (i, 0)),
        out_shape=jax.ShapeDtypeStruct((m, D_MODEL), F32),
        compiler_params=pltpu.CompilerParams(
            dimension_semantics=("parallel",), vmem_limit_bytes=VMEM_LIMIT),
        name="outproj_final" if final else "outproj",
    )(*args)


def _ssd_kernel(nb, shared, n_valid, n_chunks,
                zs_ref, u_ref, dt_ref, s0_ref, alog_ref, dsk_ref, gw_ref, e3_ref, tri_ref, sel_ref,
                cmask_ref, bmask_ref,
                y_ref, sfin_ref, st_scr):
    L = CHUNK
    c = pl.program_id(1)
    seqs = range(nb)
    groups = range(SSD_G)
    gn = [slice(g * SSD_N, (g + 1) * SSD_N) for g in groups]
    gw = [slice(g * SSD_GW, (g + 1) * SSD_GW) for g in groups]

    @pl.when(c == 0)
    def _():
        for b in seqs:
            st_scr[b] = s0_ref[0 if shared else b].T

    def tok(ref, b, cols):
        return _pad_rows(ref[b, :, cols], L)

    bm_b = [tok(u_ref, b, slice(SSD_INNER, SSD_INNER + SSD_G * SSD_N)).astype(BF16) for b in seqs]
    cm_b = [tok(u_ref, b, slice(SSD_INNER + SSD_G * SSD_N, SSD_CONV)).astype(BF16) for b in seqs]

    dt4 = [tok(dt_ref, b, slice(None)) for b in seqs]
    neg_a = -jnp.exp(alog_ref[...])
    cs4 = [_cumsum_rows(tri_ref[...], dt4[b] * neg_a) for b in seqs]
    stack3 = [_split3_merge(jnp.concatenate([cs4[b], dt4[b] * jnp.exp(cs4[b][L - 1:L, :] - cs4[b])], axis=0))
              for b in seqs]

    t2 = [_pair_rows(sel_ref[...], [cs4[b], dt4[b]]) for b in seqs]
    cb2 = [[_dot_nt(cm_b[b][:, gn[g]], jnp.concatenate([bm_b[b][:, gn[g]]] * 2, axis=0)) for b in seqs]
           for g in groups]

    slab = 4 * L
    n_slab = SSD_INNER // slab
    per_group = SSD_GW // slab

    def expand(j):
        return [_dot(stack3[b], e3_ref[:, j * slab:(j + 1) * slab]) for b in seqs]

    ex_next = expand(0)
    gated = [[] for _ in seqs]
    for j in range(n_slab):
        g = j // per_group
        lanes = slice(j * slab, (j + 1) * slab)
        ex = ex_next
        if j + 1 < n_slab:
            ex_next = expand(j + 1)
        st_old = [st_scr[b, :, lanes] for b in seqs]
        y_off = [_dot(cm_b[b][:, gn[g]], st_old[b].astype(BF16)) for b in seqs]
        xs, y_in, ecs, dtw = [], [], [], []
        for b in seqs:
            cs_x, dtw_x = ex[b][0:L], ex[b][L:2 * L]
            rows = [jnp.concatenate([jnp.broadcast_to(t2[b][2 * j + i:2 * j + i + 1, off:off + LANE], (L, LANE))
                                     for i in range(slab // LANE)], axis=1) for off in (0, LANE)]
            decay = jnp.exp(cs_x - rows[0] + cmask_ref[...]) * rows[1]
            m = (jnp.concatenate([cb2[g][b]] * (slab // LANE), axis=1) * decay).astype(BF16)
            x = tok(u_ref, b, lanes)
            rhs = jnp.concatenate([x.astype(BF16)] * 4, axis=0) * bmask_ref[...]
            y_in.append(_dot(m, rhs))
            xs.append(x)
            ecs.append(jnp.exp(cs_x))
            dtw.append(dtw_x)
        for b in seqs:
            st_scr[b, :, lanes] = (st_old[b] * ecs[b][L - 1:L, :]
                                   + _dot_tn(bm_b[b][:, gn[g]], (xs[b] * dtw[b]).astype(BF16)))
            y = y_in[b] + y_off[b] * ecs[b] + xs[b] * dsk_ref[:, lanes]
            gated[b].append(y * tok(zs_ref, b, lanes))
        if (j + 1) % per_group == 0:
            for b in seqs:
                blk_g = jnp.concatenate(gated[b], axis=1)
                gated[b] = []
                ms = jnp.mean(blk_g * blk_g, axis=-1, keepdims=True)
                y_ref[b, :, gw[g]] = (blk_g * lax.rsqrt(ms + EPS) * gw_ref[:, gw[g]]).astype(BF16)[:n_valid]

    @pl.when(c == n_chunks - 1)
    def _():
        for b in seqs:
            sfin_ref[b] = st_scr[b].T


def _tri3():
    t = np.arange(CHUNK)
    tri = (t[:, None] >= t[None, :]).astype(BF16)
    return np.concatenate([tri, tri, tri], axis=1)


def _pair_sel(n_rows, n_pairs):
    j = np.arange(n_rows)[:, None]
    kk = np.arange(LANE)[None, :]
    return (((kk % REP) // 2 == j) & (kk < 3 * REP) & (j < n_pairs)).astype(BF16)


def _ssd_consts():
    k = np.arange(LANE)[:, None]
    col = np.arange(SSD_INNER)[None, :]
    e3 = ((k % REP == col // SSD_P) & (k < 3 * REP)).astype(BF16)
    t = np.arange(CHUNK)
    tri, sel = _tri3(), _pair_sel(SSD_H // 2, SSD_H // 2)
    b = np.arange(4 * CHUNK)
    cmask = np.where(t[:, None] >= (b[None, :] % CHUNK), 0.0, NEG_BIG).astype(np.float32)
    bmask = ((b[:, None] // CHUNK) == (b[None, :] // CHUNK)).astype(BF16)
    return tuple(jnp.asarray(x) for x in (e3, tri, sel, cmask, bmask))


def _seqs_per_step(n_seq):
    return 2 if n_seq % 2 == 0 else (3 if n_seq % 3 == 0 else 1)


def _ssd_mixer(zs, u, dt, s0, p, n_valid):
    n_seq, t_len, _ = zs.shape
    assert n_valid == min(t_len, CHUNK) and (t_len % CHUNK == 0 or t_len == n_valid)
    rows = n_valid
    n_chunks = t_len // rows
    nb = _seqs_per_step(n_seq)
    shared = s0.shape[0] == 1
    init_spec = (pl.BlockSpec((1, SSD_INNER, SSD_N), lambda s, c: (0, 0, 0)) if shared
                 else pl.BlockSpec((nb, SSD_INNER, SSD_N), lambda s, c: (s, 0, 0)))
    tok_map = lambda s, c: (s, c, 0)
    const2 = lambda s, c: (0, 0)
    consts = (p["a_log4"], p["d_x"], p["gnorm_w"]) + _ssd_consts()
    return pl.pallas_call(
        functools.partial(_ssd_kernel, nb, shared, n_valid, n_chunks),
        grid=(n_seq // nb, n_chunks),
        in_specs=[
            pl.BlockSpec((nb, rows, SSD_INNER), tok_map),
            pl.BlockSpec((nb, rows, SSD_CONV), tok_map),
            pl.BlockSpec((nb, rows, LANE), tok_map),
            init_spec,
        ] + [pl.BlockSpec(cst.shape, const2) for cst in consts],
        out_specs=[
            pl.BlockSpec((nb, rows, SSD_INNER), tok_map),
            pl.BlockSpec((nb, SSD_INNER, SSD_N), lambda s, c: (s, 0, 0)),
        ],
        out_shape=[
            jax.ShapeDtypeStruct((n_seq, t_len, SSD_INNER), BF16),
            jax.ShapeDtypeStruct((n_seq, SSD_INNER, SSD_N), F32),
        ],
        scratch_shapes=[pltpu.VMEM((nb, SSD_N, SSD_INNER), F32)],
        compiler_params=pltpu.CompilerParams(
            dimension_semantics=("parallel", "arbitrary"), vmem_limit_bytes=VMEM_LIMIT),
        name="ssd_mixer",
    )(zs, u, dt, s0, *consts)


def _pair_blockdiag(y2, bd):
    y16 = y2.astype(BF16)
    return jnp.concatenate([y16, y16], axis=0) * bd


def _unit_lower_inverses(n_list, eye, same16, same32, bd):
    def mm(a_list, b_list):
        return [_dot(a.astype(BF16), _pair_blockdiag(b, bd)) for a, b in zip(a_list, b_list)]

    def axpy(t_list, d_list, sign):
        return [t + sign * d for t, d in zip(t_list, d_list)]

    nd = [n * same16 for n in n_list]
    t = [eye - x for x in nd]
    pw = mm(nd, nd)
    for step in range(3):
        t = axpy(t, mm(t, pw), 1.0)
        if step < 2:
            pw = mm(pw, pw)
    n1 = [n * (same32 - same16) for n in n_list]
    t = axpy(t, mm(t, mm(n1, t)), -1.0)
    n2 = [n * (1.0 - same32) for n in n_list]
    t = axpy(t, mm(t, mm(n2, t)), -1.0)
    return t


def _gdn_kernel(nb, nch, shared, n_valid, n_steps,
                q_ref, k_ref, v_ref, zs_ref, g_ref, beta_ref, s0_ref, ow_ref, e3_ref, tri_ref, sel_ref, bd_ref,
                masks_ref,
                y_ref, sfin_ref, st_scr):
    L = CHUNK
    c = pl.program_id(1)
    seqs = range(nb * nch)

    def tok(ref, b, cols):
        if n_valid < L:
            return _pad_rows(ref[b % nb, :, cols], L)
        return ref[b % nb, (b // nb) * L:(b // nb + 1) * L, cols]

    @pl.when(c == 0)
    def _():
        for s in range(nb):
            st_scr[s] = s0_ref[0 if shared else s]

    g = [tok(g_ref, b, slice(None)) for b in seqs]
    beta = [tok(beta_ref, b, slice(None)) for b in seqs]
    gc = [_cumsum_rows(tri_ref[...], g[b]) for b in seqs]
    stack = []
    for b in seqs:
        stack += [gc[b], beta[b]]
    ex = _dot(_split3_merge(jnp.concatenate(stack, axis=0)), e3_ref[...])
    gc_x = [ex[(2 * b) * L:(2 * b + 1) * L] for b in seqs]
    beta_x = [ex[(2 * b + 1) * L:(2 * b + 2) * L] for b in seqs]
    egc_x = [jnp.exp(x) for x in gc_x]
    egl_x = [jnp.exp(x[L - 1:L, :] - x) for x in gc_x]
    lane = _iota((L, LANE), 1)
    gc_t2 = [_pair_rows(sel_ref[...], [gc[b]]) for b in seqs]

    incl_add = masks_ref[0]
    strict = masks_ref[1]
    eye = masks_ref[2]
    same16 = masks_ref[3]
    same32 = masks_ref[4]
    bd = bd_ref[...]
    first = lane < L

    chains = [(b, h) for h in range(GDN_H) for b in seqs]
    pairs = [(b, p) for p in range(GDN_H // 2) for b in seqs]
    chain_of = lambda b, h: h * len(seqs) + b
    ks = [slice(h * GDN_DK, (h + 1) * GDN_DK) for h in range(GDN_H)]
    vs = [slice(h * GDN_DV, (h + 1) * GDN_DV) for h in range(GDN_H)]
    q = [tok(q_ref, b, ks[h]) for b, h in chains]
    k = [tok(k_ref, b, ks[h]) for b, h in chains]
    egc = [egc_x[b][:, ks[h]] for b, h in chains]
    kb = [k[i] * beta_x[b][:, ks[h]] for i, (b, h) in enumerate(chains)]
    n_ch = range(len(chains))
    zero_k = jnp.zeros((L, GDN_DK), BF16)

    kq2, dec2 = [], []
    for b, p in pairs:
        ia, ib = chain_of(b, 2 * p), chain_of(b, 2 * p + 1)
        lhs = jnp.concatenate([jnp.concatenate([kb[ia], kb[ib]], axis=1),
                               jnp.concatenate([q[ia], q[ib]], axis=1)], axis=0).astype(BF16)
        rhs_nt = jnp.concatenate([jnp.concatenate([k[ia].astype(BF16), zero_k], axis=1),
                                  jnp.concatenate([zero_k, k[ib].astype(BF16)], axis=1)], axis=0)
        kq2.append(_dot_nt(lhs, rhs_nt))
        col2 = jnp.where(first, gc_x[b][:, ks[2 * p]], gc_x[b][:, ks[2 * p + 1]])
        dec2.append(jnp.exp(col2 - jnp.broadcast_to(gc_t2[b][p:p + 1, :], (L, LANE)) + incl_add))
    n_pr = range(len(pairs))
    t_inv2 = _unit_lower_inverses([kq2[j][0:L] * dec2[j] * strict for j in n_pr], eye, same16, same32, bd)

    zero_r = jnp.zeros((L, GDN_DV + GDN_DK), BF16)
    uw = [None] * len(chains)
    for j, (b, p) in enumerate(pairs):
        halves = []
        for h in (2 * p, 2 * p + 1):
            i = chain_of(b, h)
            beta_h = beta_x[b][:, ks[h]]
            halves.append(jnp.concatenate([tok(v_ref, b, vs[h]) * jnp.concatenate([beta_h, beta_h], axis=1),
                                           kb[i] * egc[i]], axis=1).astype(BF16))
        rhs_bd = jnp.concatenate([jnp.concatenate([halves[0], zero_r], axis=1),
                                  jnp.concatenate([zero_r, halves[1]], axis=1)], axis=0)
        uw2 = _dot(t_inv2[j].astype(BF16), rhs_bd)
        width = GDN_DV + GDN_DK
        uw[chain_of(b, 2 * p)] = uw2[:, 0:width]
        uw[chain_of(b, 2 * p + 1)] = uw2[:, width:2 * width]
    zero_v = jnp.zeros((L, GDN_DV), BF16)
    for cc in range(nch):
        mine = [(i, b, h) for i, (b, h) in enumerate(chains) if b // nb == cc]
        s_old = {i: st_scr[b % nb, h] for i, b, h in mine}
        wq = {i: _dot(jnp.concatenate([uw[i][:, GDN_DV:], q[i] * egc[i]], axis=0).astype(BF16),
                      s_old[i].astype(BF16)) for i, b, h in mine}
        v_new = {i: (uw[i][:, :GDN_DV] - wq[i][0:L]).astype(BF16) for i, b, h in mine}
        o = {}
        for j, (b, p) in enumerate(pairs):
            if b // nb != cc:
                continue
            ia, ib = chain_of(b, 2 * p), chain_of(b, 2 * p + 1)
            v_bd = jnp.concatenate([jnp.concatenate([v_new[ia], zero_v], axis=1),
                                    jnp.concatenate([zero_v, v_new[ib]], axis=1)], axis=0)
            o2 = _dot((kq2[j][L:2 * L] * dec2[j]).astype(BF16), v_bd)
            o[ia] = wq[ia][L:2 * L] + o2[:, 0:GDN_DV]
            o[ib] = wq[ib][L:2 * L] + o2[:, GDN_DV:]
        for i, b, h in mine:
            e_last = egc[i][L - 1:L, :]
            st_scr[b % nb, h] = (s_old[i] * jnp.concatenate([e_last, e_last], axis=1)
                                 + _dot_tn((k[i] * egl_x[b][:, ks[h]]).astype(BF16), v_new[i]))
        for i, b, h in mine:
            o_h = o[i] * lax.rsqrt(jnp.mean(o[i] * o[i], axis=-1, keepdims=True) + EPS) * ow_ref[...]
            y_ref[b % nb, cc * L:cc * L + n_valid, vs[h]] = (o_h * tok(zs_ref, b, vs[h])).astype(BF16)[:n_valid]

    @pl.when(c == n_steps - 1)
    def _():
        for s in range(nb):
            sfin_ref[s] = st_scr[s]


def _gdn_consts():
    k = np.arange(LANE)[:, None]
    col = np.arange(GDN_KEY)[None, :]
    e3 = ((k % REP == col // GDN_DK) & (k < 3 * REP)).astype(BF16)
    t = np.arange(CHUNK)
    r, cc = t[:, None], t[None, :]
    tri, sel = _tri3(), _pair_sel(2 * GDN_H, GDN_H // 2)
    masks = np.stack([
        np.where(r >= cc, 0.0, NEG_BIG),
        (r > cc).astype(np.float32),
        (r == cc).astype(np.float32),
        ((r // (CHUNK // 4)) == (cc // (CHUNK // 4))).astype(np.float32),
        ((r // (CHUNK // 2)) == (cc // (CHUNK // 2))).astype(np.float32),
    ]).astype(np.float32)
    masks = np.concatenate([masks, masks], axis=-1)
    b = np.arange(2 * CHUNK)
    bd = ((b[:, None] // CHUNK) == (b[None, :] // CHUNK)).astype(BF16)
    return tuple(jnp.asarray(x) for x in (e3, tri, sel, bd, masks))


def _gdn_mixer(q, k, v, zs, g, beta, s0, p, n_valid):
    n_seq, t_len, _ = zs.shape
    assert n_valid == min(t_len, CHUNK) and (t_len % CHUNK == 0 or t_len == n_valid)
    n_chunks = t_len // n_valid
    nb = _seqs_per_step(n_seq)
    nch = GDN_CHUNKS_PER_STEP if (nb <= 2 and n_chunks % GDN_CHUNKS_PER_STEP == 0) else 1
    rows = nch * n_valid
    shared = s0.shape[0] == 1
    init_spec = (pl.BlockSpec((1, GDN_H, GDN_DK, GDN_DV), lambda s, c: (0, 0, 0, 0)) if shared
                 else pl.BlockSpec((nb, GDN_H, GDN_DK, GDN_DV), lambda s, c: (s, 0, 0, 0)))
    tok_map = lambda s, c: (s, c, 0)
    e3, tri, sel, bd, masks = _gdn_consts()
    consts2 = (p["onorm_w"], e3, tri, sel, bd)
    return pl.pallas_call(
        functools.partial(_gdn_kernel, nb, nch, shared, n_valid, n_chunks // nch),
        grid=(n_seq // nb, n_chunks // nch),
        in_specs=[
            pl.BlockSpec((nb, rows, GDN_KEY), tok_map),
            pl.BlockSpec((nb, rows, GDN_KEY), tok_map),
            pl.BlockSpec((nb, rows, GDN_VAL), tok_map),
            pl.BlockSpec((nb, rows, GDN_VAL), tok_map),
            pl.BlockSpec((nb, rows, LANE), tok_map),
            pl.BlockSpec((nb, rows, LANE), tok_map),
            init_spec,
        ] + [pl.BlockSpec(cst.shape, lambda s, c: (0, 0)) for cst in consts2]
          + [pl.BlockSpec(masks.shape, lambda s, c: (0, 0, 0))],
        out_specs=[
            pl.BlockSpec((nb, rows, GDN_VAL), tok_map),
            pl.BlockSpec((nb, GDN_H, GDN_DK, GDN_DV), lambda s, c: (s, 0, 0, 0)),
        ],
        out_shape=[
            jax.ShapeDtypeStruct((n_seq, t_len, GDN_VAL), BF16),
            jax.ShapeDtypeStruct((n_seq, GDN_H, GDN_DK, GDN_DV), F32),
        ],
        scratch_shapes=[pltpu.VMEM((nb, GDN_H, GDN_DK, GDN_DV), F32)],
        compiler_params=pltpu.CompilerParams(
            dimension_semantics=("parallel", "arbitrary"), vmem_limit_bytes=VMEM_LIMIT),
        name="gdn_mixer",
    )(q, k, v, zs, g, beta, s0, *consts2, masks)


def _rep_lanes(v, n_heads):
    n_rep = LANE // REP if n_heads == REP else 3
    row = jnp.pad(v, (0, REP - n_heads))
    return jnp.pad(jnp.tile(row, n_rep), (0, LANE - n_rep * REP)).reshape(1, LANE)


def _rep_cols(w, n_heads):
    n_rep = LANE // REP if n_heads == REP else 3
    blk = jnp.pad(w, ((0, 0), (0, REP - n_heads)))
    return jnp.pad(jnp.tile(blk, (1, n_rep)), ((0, 0), (0, LANE - n_rep * REP)))


def _ssd_state_in(s):
    return s.reshape(s.shape[0], SSD_INNER, SSD_N)


def _ssd_state_out(s):
    return s.reshape(s.shape[0], SSD_H, SSD_P, SSD_N)


def _tail_from_rows(rows3):
    return jnp.pad(rows3, ((0, 0), (TAIL - (CONV_K - 1), 0), (0, 0)))


def _segment_valid(n_valid, tm_in, spt):
    return tm_in // spt if n_valid == CHUNK else n_valid


def _ssd_layer(h, tail0, s0, p, layer, spt, tm_in, tm_out, n_valid, final_w=None):
    n_grp, rows, _ = h.shape
    weights = [(p["w_in"], layer), (p["w_dt4"][layer], None)]
    consts = [p["norm_w"][layer].reshape(1, D_MODEL), p["conv_w"][layer], p["conv_b"][layer].reshape(1, SSD_CONV),
              p["dt_bias4"][layer]]
    zs, u, dt, tails = _inproj_call(_ssd_inproj_kernel, h, tail0, spt, tm_in, _segment_valid(n_valid, tm_in, spt),
                                    weights, consts, (SSD_INNER, SSD_CONV, LANE), SSD_CONV, "ssd_inproj")
    seq = lambda a: a.reshape(n_grp * spt, rows // spt, a.shape[-1])
    mp = dict(a_log4=p["a_log4"][layer], d_x=p["d_x"][layer], gnorm_w=p["gnorm_w"][layer])
    y, s_fin = _ssd_mixer(seq(zs), seq(u), seq(dt), s0, mp, n_valid)
    h_new = _outproj(y.reshape(n_grp * rows, SSD_INNER), p["w_out"], layer, h.reshape(n_grp * rows, D_MODEL),
                     tm_out, final_w)
    return h_new.reshape(h.shape), tails, s_fin


def _gdn_layer(h, tail0, s0, p, layer, spt, tm_in, tm_out, n_valid, final_w=None):
    n_grp, rows, _ = h.shape
    weights = [(p["w_in"], layer), (p["w_ab"][layer], None)]
    consts = [p["norm_w"][layer].reshape(1, D_MODEL), p["conv_w"][layer], p["dt_bias"][layer], p["a_log"][layer]]
    q, k, v, zs, g, beta, tails = _inproj_call(
        _gdn_inproj_kernel, h, tail0, spt, tm_in, _segment_valid(n_valid, tm_in, spt), weights, consts,
        (GDN_KEY, GDN_KEY, GDN_VAL, GDN_VAL, LANE, LANE), GDN_CONV, "gdn_inproj")
    seq = lambda a: a.reshape(n_grp * spt, rows // spt, a.shape[-1])
    mp = dict(onorm_w=p["onorm_w"][layer].reshape(1, GDN_DV))
    y, s_fin = _gdn_mixer(seq(q), seq(k), seq(v), seq(zs), seq(g), seq(beta), s0, mp, n_valid)
    h_new = _outproj(y.reshape(n_grp * rows, GDN_VAL), p["w_out"], layer, h.reshape(n_grp * rows, D_MODEL),
                     tm_out, final_w)
    return h_new.reshape(h.shape), tails, s_fin


def kernel(x_prompt, x_sample, state_ssd, state_ssd_conv, state_gdn, state_gdn_conv, meta_tokens,
           ssd_norm_w, ssd_w_in, ssd_conv_w, ssd_conv_b, ssd_dt_bias, ssd_a_log, ssd_d, ssd_gnorm_w, ssd_w_out,
           gdn_norm_w, gdn_w_in, gdn_conv_w, gdn_dt_bias, gdn_a_log, gdn_onorm_w, gdn_w_out, final_norm_w):
    n_ssd, n_gdn = ssd_norm_w.shape[0], gdn_norm_w.shape[0]
    depth = n_ssd + n_gdn
    n_dec, dec_t, _ = x_sample.shape
    assert dec_t == N_META and N_META <= CHUNK and x_prompt.shape[1] % TM_PROMPT_OUT == 0
    keep = CONV_K - 1

    ssd_p = dict(
        norm_w=ssd_norm_w, w_in=ssd_w_in.astype(BF16), w_out=ssd_w_out.astype(BF16),
        w_dt4=[_rep_cols(ssd_w_in[j][:, SSD_INNER + SSD_CONV:], SSD_H).astype(BF16) for j in range(n_ssd)],
        conv_w=ssd_conv_w, conv_b=ssd_conv_b,
        dt_bias4=[_rep_lanes(ssd_dt_bias[j], SSD_H) for j in range(n_ssd)],
        a_log4=[_rep_lanes(ssd_a_log[j], SSD_H) for j in range(n_ssd)],
        d_x=[jnp.repeat(ssd_d[j], SSD_P).reshape(1, SSD_INNER) for j in range(n_ssd)],
        gnorm_w=[ssd_gnorm_w[j].reshape(1, SSD_INNER) for j in range(n_ssd)],
    )
    ab0 = GDN_CONV + GDN_VAL
    gdn_p = dict(
        norm_w=gdn_norm_w, w_in=gdn_w_in.astype(BF16), w_out=gdn_w_out.astype(BF16),
        w_ab=[jnp.concatenate([_rep_cols(gdn_w_in[j][:, ab0:ab0 + GDN_H], GDN_H),
                               _rep_cols(gdn_w_in[j][:, ab0 + GDN_H:], GDN_H)], axis=1).astype(BF16)
              for j in range(n_gdn)],
        conv_w=gdn_conv_w,
        dt_bias=[_rep_lanes(gdn_dt_bias[j], GDN_H) for j in range(n_gdn)],
        a_log=[_rep_lanes(gdn_a_log[j], GDN_H) for j in range(n_gdn)],
        onorm_w=gdn_onorm_w,
    )

    n_small = n_dec + 1
    hs = jnp.concatenate([x_sample, meta_tokens.astype(x_sample.dtype)[None]], axis=0)
    hs = hs.reshape(1, n_small * N_META, D_MODEL)
    hp = x_prompt
    tm_small = n_small * N_META
    last_rows = lambda tails: tails[:, TAIL - keep:]
    outs = {k: [] for k in ("p_ssd", "p_ssd_conv", "p_gdn", "p_gdn_conv", "s_ssd", "s_ssd_conv", "s_gdn", "s_gdn_conv")}
    for i in range(depth):
        j = i // 2
        fw = final_norm_w if i == depth - 1 else None
        if i % 2 == 0:
            tail_s = _tail_from_rows(jnp.concatenate([state_ssd_conv[j], jnp.zeros((1, keep, SSD_CONV), F32)], axis=0))
            s0_s = _ssd_state_in(jnp.concatenate([state_ssd[j], jnp.zeros((1,) + state_ssd.shape[2:], F32)], axis=0))
            hs, tails_s, sfin_s = _ssd_layer(hs, tail_s, s0_s, ssd_p, j, n_small, tm_small, tm_small, N_META, fw)
            hp, tails_p, sfin_p = _ssd_layer(hp, tails_s[n_dec:], sfin_s[n_dec:], ssd_p, j, 1, TM_PROMPT_IN,
                                             TM_PROMPT_OUT, CHUNK, fw)
            outs["s_ssd"].append(_ssd_state_out(sfin_s[:n_dec]))
            outs["p_ssd"].append(_ssd_state_out(sfin_p))
            outs["s_ssd_conv"].append(last_rows(tails_s[:n_dec]))
            outs["p_ssd_conv"].append(last_rows(tails_p))
        else:
            tail_s = _tail_from_rows(jnp.concatenate([state_gdn_conv[j], jnp.zeros((1, keep, GDN_CONV), F32)], axis=0))
            s0_s = jnp.concatenate([state_gdn[j], jnp.zeros((1,) + state_gdn.shape[2:], F32)], axis=0)
            hs, tails_s, sfin_s = _gdn_layer(hs, tail_s, s0_s, gdn_p, j, n_small, tm_small, tm_small, N_META, fw)
            hp, tails_p, sfin_p = _gdn_layer(hp, tails_s[n_dec:], sfin_s[n_dec:], gdn_p, j, 1, TM_PROMPT_IN,
                                             TM_PROMPT_OUT, CHUNK, fw)
            outs["s_gdn"].append(sfin_s[:n_dec])
            outs["p_gdn"].append(sfin_p)
            outs["s_gdn_conv"].append(last_rows(tails_s[:n_dec]))
            outs["p_gdn_conv"].append(last_rows(tails_p))
    y_sample = hs.reshape(n_small, N_META, D_MODEL)[:n_dec]
    st = lambda key: jnp.stack(outs[key])
    return (hp, y_sample, st("p_ssd"), st("p_ssd_conv"), st("p_gdn"), st("p_gdn_conv"),
            st("s_ssd"), st("s_ssd_conv"), st("s_gdn"), st("s_gdn_conv"))
```

```python
import functools

import jax
import jax.numpy as jnp
import numpy as np
from jax import lax
from jax.experimental import pallas as pl
from jax.experimental.pallas import tpu as pltpu

F32 = jnp.float32
BF16 = jnp.bfloat16

D_MODEL = 1024
N_META = 16
CONV_K = 4
EPS = 1e-6
CHUNK = 64
TAIL = 8
LANE = 128
COL_BLK = 512
SSD_INNER = 2048
SSD_P = 64
SSD_H = 32
SSD_G = 4
SSD_N = 128
SSD_GW = SSD_INNER // SSD_G
SSD_CONV = SSD_INNER + 2 * SSD_G * SSD_N
GDN_DK = 128
GDN_DV = 256
GDN_H = 8
GDN_KEY = GDN_H * GDN_DK
GDN_VAL = GDN_H * GDN_DV
GDN_CONV = 2 * GDN_KEY + GDN_VAL
REP = 32
NEG_BIG = -1e30

VMEM_LIMIT = 48 * 1024 * 1024
TM_PROMPT_IN = 256
TM_PROMPT_OUT = 1024
GDN_CHUNKS_PER_STEP = 4


NEG_LOG2E = -1.4426950408889634


def _sigmoid(x):
    return 1.0 / (1.0 + jnp.exp2(x * NEG_LOG2E))


def _silu(x):
    return x * _sigmoid(x)


def _softplus(x):
    return jnp.maximum(x, 0.0) + jnp.log1p(jnp.exp(-jnp.abs(x)))


def _dot(a, b, precision=None):
    return jnp.dot(a, b, preferred_element_type=F32, precision=precision)


def _dot_nt(a, b, precision=None):
    return lax.dot_general(a, b, (((1,), (1,)), ((), ())), preferred_element_type=F32, precision=precision)


def _dot_tn(a, b):
    return lax.dot_general(a, b, (((0,), (0,)), ((), ())), preferred_element_type=F32)


def _iota(shape, dim):
    return lax.broadcasted_iota(jnp.int32, shape, dim)


def _pad_rows(x, n_rows):
    if x.shape[0] == n_rows:
        return x
    return jnp.concatenate([x, jnp.zeros((n_rows - x.shape[0], x.shape[1]), x.dtype)], axis=0)


def _split3(x):
    hi = x.astype(BF16)
    r1 = x - hi.astype(F32)
    mid = r1.astype(BF16)
    lo = (r1 - mid.astype(F32)).astype(BF16)
    return hi, mid, lo


def _split3_merge(x):
    hi, mid, lo = _split3(x)
    lane = _iota(x.shape, 1)
    zero = jnp.zeros_like(hi)
    return jnp.where(lane < REP, hi, jnp.where(lane < 2 * REP, mid, jnp.where(lane < 3 * REP, lo, zero)))


def _cumsum_rows(tri3, x):
    return _dot(tri3, jnp.concatenate(_split3(x), axis=0))


def _pair_rows(sel3, values):
    odd = (_iota(values[0].shape, 1) & 1) == 1
    zero = jnp.zeros(values[0].shape, BF16)
    rows = []
    for v in values:
        m3 = _split3_merge(v)
        rows += [jnp.where(odd, zero, m3), jnp.where(odd, m3, zero)]
    return _dot_nt(sel3, jnp.concatenate(rows, axis=0))


def _rmsnorm_bf16(x, w):
    ms = jnp.mean(x * x, axis=-1, keepdims=True)
    return (x * lax.rsqrt(ms + EPS) * w).astype(BF16)


def _conv_block(tail_scr, s, cols, raw, n_valid, cw_ref, bias, tailo_ref):
    seg = raw.shape[0]
    x = [tail_scr[s, :, cols]] + [raw[i * TAIL:(i + 1) * TAIL] for i in range(seg // TAIL)]
    last = raw[n_valid - TAIL:n_valid]
    tail_scr[s, :, cols] = last
    tailo_ref[s, :, cols] = last
    row = _iota((TAIL, raw.shape[1]), 0)

    def shift_down(tiles, d):
        mixed = [tiles[0]] + [jnp.where(row >= TAIL - d, tiles[i - 1], tiles[i]) for i in range(1, len(tiles))]
        return [pltpu.roll(t, d, 0) for t in mixed]

    assert CONV_K == 4
    w = [cw_ref[k:k + 1, cols] for k in range(CONV_K)]
    x1 = shift_down(x, 1)
    near = [t * w[3] + t1 * w[2] for t, t1 in zip(x, x1)]
    far = shift_down([t * w[1] + t1 * w[0] for t, t1 in zip(x, x1)], 2)
    acc = jnp.concatenate([a + b for a, b in zip(near[1:], far[1:])], axis=0)
    if bias is not None:
        acc = acc + bias
    return acc


def _ssd_inproj_kernel(spt, seg, n_valid, x_ref, nw_ref, w_ref, wdt_ref, cw_ref, cb_ref, dtb_ref, tail_ref,
                       zs_ref, u_ref, dt_ref, tailo_ref, tail_scr):
    @pl.when(pl.program_id(1) == 0)
    def _():
        tail_scr[...] = tail_ref[...]

    xn = _rmsnorm_bf16(x_ref[0], nw_ref[...])
    dt_ref[0] = _softplus(_dot(xn, wdt_ref[...]) + dtb_ref[...])

    def conv_dot(j):
        return _dot(xn, w_ref[:, SSD_INNER + j * COL_BLK:SSD_INNER + (j + 1) * COL_BLK])

    n_conv, n_gate = SSD_CONV // COL_BLK, SSD_INNER // COL_BLK
    raw = conv_dot(0)
    for j in range(n_conv):
        cols = slice(j * COL_BLK, (j + 1) * COL_BLK)
        raw_next = conv_dot(j + 1) if j + 1 < n_conv else None
        gate = _dot(xn, w_ref[:, cols]) if j < n_gate else None
        for s in range(spt):
            acc = _conv_block(tail_scr, s, cols, raw[s * seg:(s + 1) * seg], n_valid, cw_ref, cb_ref[:, cols],
                              tailo_ref)
            u_ref[0, s * seg:(s + 1) * seg, cols] = _silu(acc)
        if gate is not None:
            zs_ref[0, :, cols] = _silu(gate)
        raw = raw_next


def _gdn_inproj_kernel(spt, seg, n_valid, x_ref, nw_ref, w_ref, wab_ref, cw_ref, dtb_ref, alog_ref, tail_ref,
                       q_ref, k_ref, v_ref, zs_ref, g_ref, beta_ref, tailo_ref, tail_scr):
    @pl.when(pl.program_id(1) == 0)
    def _():
        tail_scr[...] = tail_ref[...]

    xn = _rmsnorm_bf16(x_ref[0], nw_ref[...])
    ab = _dot(xn, wab_ref[...])
    lane = _iota((1, LANE), 1)
    head_lane = ((lane & (REP - 1)) < GDN_H) & (lane < 3 * REP)
    coef = jnp.where(head_lane, -jnp.exp(alog_ref[...]), 0.0)
    g_ref[0] = coef * _softplus(ab[:, :LANE] + dtb_ref[...])
    beta_ref[0] = _sigmoid(ab[:, LANE:])
    def conv_dot(j):
        return _dot(xn, w_ref[:, j * COL_BLK:(j + 1) * COL_BLK])

    n_conv, n_gate = GDN_CONV // COL_BLK, GDN_VAL // COL_BLK
    raw_next = conv_dot(0)
    for j in range(n_conv):
        cols = slice(j * COL_BLK, (j + 1) * COL_BLK)
        raw = raw_next
        raw_next = conv_dot(j + 1) if j + 1 < n_conv else None
        if j < n_gate:
            gate = _dot(xn, w_ref[:, GDN_CONV + j * COL_BLK:GDN_CONV + (j + 1) * COL_BLK])
            zs_ref[0, :, cols] = _silu(gate)
        for s in range(spt):
            rows = slice(s * seg, (s + 1) * seg)
            u = _silu(_conv_block(tail_scr, s, cols, raw[rows], n_valid, cw_ref, None, tailo_ref))
            if j * COL_BLK < 2 * GDN_KEY:
                is_q = j * COL_BLK < GDN_KEY
                dst = q_ref if is_q else k_ref
                off = j * COL_BLK - (0 if is_q else GDN_KEY)
                for i in range(COL_BLK // GDN_DK):
                    t = u[:, i * GDN_DK:(i + 1) * GDN_DK]
                    t = t * lax.rsqrt(jnp.sum(t * t, axis=-1, keepdims=True) + EPS)
                    if is_q:
                        t = t * (GDN_DK ** -0.5)
                    dst[0, rows, off + i * GDN_DK:off + (i + 1) * GDN_DK] = t
            else:
                off = j * COL_BLK - 2 * GDN_KEY
                v_ref[0, rows, off:off + COL_BLK] = u


def _inproj_call(body, x, tail0, spt, tm, n_valid, weights, consts, out_widths, conv_dim, name):
    n_grp, rows, _ = x.shape
    assert rows % tm == 0 and tm % spt == 0
    seg = tm // spt
    n_tiles = rows // tm
    assert (spt == 1 and n_valid == seg) or n_tiles == 1
    assert n_valid % TAIL == 0 and TAIL <= n_valid <= seg
    shared = tail0.shape[0] == 1
    tok_map = lambda g, t: (g, t, 0)
    tail_map = (lambda g, t: (0, 0, 0)) if shared else (lambda g, t: (g, 0, 0))
    in_specs = [pl.BlockSpec((1, tm, D_MODEL), tok_map), pl.BlockSpec((1, D_MODEL), lambda g, t: (0, 0))]
    args = [x, consts[0]]
    for w, layer in weights:
        if layer is None:
            in_specs.append(pl.BlockSpec(w.shape, lambda g, t: (0, 0), pipeline_mode=pl.Buffered(1)))
        else:
            in_specs.append(pl.BlockSpec((None,) + w.shape[1:], lambda g, t, layer=layer: (layer, 0, 0),
                                         pipeline_mode=pl.Buffered(1)))
        args.append(w)
    for cst in consts[1:]:
        in_specs.append(pl.BlockSpec(cst.shape, lambda g, t: (0, 0)))
        args.append(cst)
    in_specs.append(pl.BlockSpec((spt, TAIL, conv_dim), tail_map))
    args.append(tail0)
    out_specs = [pl.BlockSpec((1, tm, wd), tok_map) for wd in out_widths]
    out_shape = [jax.ShapeDtypeStruct((n_grp, rows, wd), F32) for wd in out_widths]
    out_specs.append(pl.BlockSpec((spt, TAIL, conv_dim), lambda g, t: (g, 0, 0)))
    out_shape.append(jax.ShapeDtypeStruct((n_grp * spt, TAIL, conv_dim), F32))
    return pl.pallas_call(
        functools.partial(body, spt, seg, n_valid),
        grid=(n_grp, n_tiles),
        in_specs=in_specs, out_specs=out_specs, out_shape=out_shape,
        scratch_shapes=[pltpu.VMEM((spt, TAIL, conv_dim), F32)],
        compiler_params=pltpu.CompilerParams(
            dimension_semantics=("parallel", "arbitrary"), vmem_limit_bytes=VMEM_LIMIT),
        name=name,
    )(*args)


def _outproj_kernel(final, y_ref, w_ref, h_ref, *rest):
    if final:
        fw_ref, o_ref = rest
    else:
        (o_ref,) = rest
    h = h_ref[...] + _dot(y_ref[...], w_ref[...])
    if final:
        ms = jnp.mean(h * h, axis=-1, keepdims=True)
        h = h * lax.rsqrt(ms + EPS) * fw_ref[...]
    o_ref[...] = h


def _outproj(y2d, w_stack, layer, h2d, tm, final_w=None):
    m, k = y2d.shape
    assert m % tm == 0
    final = final_w is not None
    in_specs = [
        pl.BlockSpec((tm, k), lambda i: (i, 0)),
        pl.BlockSpec((None, k, D_MODEL), lambda i: (layer, 0, 0), pipeline_mode=pl.Buffered(1)),
        pl.BlockSpec((tm, D_MODEL), lambda i: (i, 0)),
    ]
    args = [y2d, w_stack, h2d]
    if final:
        in_specs.append(pl.BlockSpec((1, D_MODEL), lambda i: (0, 0)))
        args.append(final_w.reshape(1, D_MODEL))
    return pl.pallas_call(
        functools.partial(_outproj_kernel, final),
        grid=(m // tm,),
        in_specs=in_specs,
        out_specs=pl.BlockSpec((tm, D_MODEL), lambda i: (i, 0)),
        out_shape=jax.ShapeDtypeStruct((m, D_MODEL), F32),
        compiler_params=pltpu.CompilerParams(
            dimension_semantics=("parallel",), vmem_limit_bytes=VMEM_LIMIT),
        name="outproj_final" if final else "outproj",
    )(*args)


def _initial_state(s0_refs, b, nb, n_given):
    if len(s0_refs) == 1:
        return s0_refs[0][0]
    val = s0_refs[b][0]
    return jnp.where(pl.program_id(0) * nb + b >= n_given, 0.0, val)


def _initial_state_specs(s0, nb, block):
    zeros = (0,) * len(block)
    if s0.shape[0] == 1:
        return [pl.BlockSpec((1,) + block, lambda s, c: (0,) + zeros)], [s0]
    last = s0.shape[0] - 1
    specs = [pl.BlockSpec((1,) + block, lambda s, c, i=i: (jnp.minimum(s * nb + i, last),) + zeros)
             for i in range(nb)]
    return specs, [s0] * nb


def _ssd_kernel(nb, n_s0, n_given, n_valid, n_chunks, zs_ref, u_ref, dt_ref, *rest):
    s0_refs = rest[:n_s0]
    (alog_ref, dsk_ref, gw_ref, e3_ref, tri_ref, sel_ref, cmask_ref, bmask_ref,
     y_ref, sfin_ref, st_scr) = rest[n_s0:]
    L = CHUNK
    c = pl.program_id(1)
    seqs = range(nb)
    groups = range(SSD_G)
    gn = [slice(g * SSD_N, (g + 1) * SSD_N) for g in groups]
    gw = [slice(g * SSD_GW, (g + 1) * SSD_GW) for g in groups]

    @pl.when(c == 0)
    def _():
        for b in seqs:
            st_scr[b] = _initial_state(s0_refs, b, nb, n_given).T

    def tok(ref, b, cols):
        return _pad_rows(ref[b, :, cols], L)

    bm_b = [tok(u_ref, b, slice(SSD_INNER, SSD_INNER + SSD_G * SSD_N)).astype(BF16) for b in seqs]
    cm_b = [tok(u_ref, b, slice(SSD_INNER + SSD_G * SSD_N, SSD_CONV)).astype(BF16) for b in seqs]

    dt4 = [tok(dt_ref, b, slice(None)) for b in seqs]
    neg_a = -jnp.exp(alog_ref[...])
    cs4 = [_cumsum_rows(tri_ref[...], dt4[b] * neg_a) for b in seqs]
    stack3 = [_split3_merge(jnp.concatenate([cs4[b], dt4[b] * jnp.exp(cs4[b][L - 1:L, :] - cs4[b])], axis=0))
              for b in seqs]

    t2 = [_pair_rows(sel_ref[...], [cs4[b], dt4[b]]) for b in seqs]
    cb2 = [[_dot_nt(cm_b[b][:, gn[g]], jnp.concatenate([bm_b[b][:, gn[g]]] * 2, axis=0)) for b in seqs]
           for g in groups]

    slab = 4 * L
    n_slab = SSD_INNER // slab
    per_group = SSD_GW // slab

    def expand(j):
        return [_dot(stack3[b], e3_ref[:, j * slab:(j + 1) * slab]) for b in seqs]

    ex_next = expand(0)
    gated = [[] for _ in seqs]
    for j in range(n_slab):
        g = j // per_group
        lanes = slice(j * slab, (j + 1) * slab)
        ex = ex_next
        if j + 1 < n_slab:
            ex_next = expand(j + 1)
        st_old = [st_scr[b, :, lanes] for b in seqs]
        y_off = [_dot(cm_b[b][:, gn[g]], st_old[b].astype(BF16)) for b in seqs]
        xs, y_in, ecs, dtw = [], [], [], []
        for b in seqs:
            cs_x, dtw_x = ex[b][0:L], ex[b][L:2 * L]
            rows = [jnp.concatenate([jnp.broadcast_to(t2[b][2 * j + i:2 * j + i + 1, off:off + LANE], (L, LANE))
                                     for i in range(slab // LANE)], axis=1) for off in (0, LANE)]
            decay = jnp.exp(cs_x - rows[0] + cmask_ref[...]) * rows[1]
            m = (jnp.concatenate([cb2[g][b]] * (slab // LANE), axis=1) * decay).astype(BF16)
            x = tok(u_ref, b, lanes)
            rhs = jnp.concatenate([x.astype(BF16)] * 4, axis=0) * bmask_ref[...]
            y_in.append(_dot(m, rhs))
            xs.append(x)
            ecs.append(jnp.exp(cs_x))
            dtw.append(dtw_x)
        for b in seqs:
            st_scr[b, :, lanes] = (st_old[b] * ecs[b][L - 1:L, :]
                                   + _dot_tn(bm_b[b][:, gn[g]], (xs[b] * dtw[b]).astype(BF16)))
            y = y_in[b] + y_off[b] * ecs[b] + xs[b] * dsk_ref[:, lanes]
            gated[b].append(y * tok(zs_ref, b, lanes))
        if (j + 1) % per_group == 0:
            for b in seqs:
                blk_g = jnp.concatenate(gated[b], axis=1)
                gated[b] = []
                ms = jnp.mean(blk_g * blk_g, axis=-1, keepdims=True)
                y_ref[b, :, gw[g]] = (blk_g * lax.rsqrt(ms + EPS) * gw_ref[:, gw[g]]).astype(BF16)[:n_valid]

    @pl.when(c == n_chunks - 1)
    def _():
        for b in seqs:
            sfin_ref[b] = st_scr[b].T


def _tri3():
    t = np.arange(CHUNK)
    tri = (t[:, None] >= t[None, :]).astype(BF16)
    return np.concatenate([tri, tri, tri], axis=1)


def _pair_sel(n_rows, n_pairs):
    j = np.arange(n_rows)[:, None]
    kk = np.arange(LANE)[None, :]
    return (((kk % REP) // 2 == j) & (kk < 3 * REP) & (j < n_pairs)).astype(BF16)


def _ssd_consts():
    k = np.arange(LANE)[:, None]
    col = np.arange(SSD_INNER)[None, :]
    e3 = ((k % REP == col // SSD_P) & (k < 3 * REP)).astype(BF16)
    t = np.arange(CHUNK)
    tri, sel = _tri3(), _pair_sel(SSD_H // 2, SSD_H // 2)
    b = np.arange(4 * CHUNK)
    cmask = np.where(t[:, None] >= (b[None, :] % CHUNK), 0.0, NEG_BIG).astype(np.float32)
    bmask = ((b[:, None] // CHUNK) == (b[None, :] // CHUNK)).astype(BF16)
    return tuple(jnp.asarray(x) for x in (e3, tri, sel, cmask, bmask))


def _seqs_per_step(n_seq):
    return 2 if n_seq % 2 == 0 else (3 if n_seq % 3 == 0 else 1)


def _ssd_mixer(zs, u, dt, s0, p, n_valid):
    n_seq, t_len, _ = zs.shape
    assert n_valid == min(t_len, CHUNK) and (t_len % CHUNK == 0 or t_len == n_valid)
    rows = n_valid
    n_chunks = t_len // rows
    nb = _seqs_per_step(n_seq)
    init_specs, init_args = _initial_state_specs(s0, nb, (SSD_INNER, SSD_N))
    tok_map = lambda s, c: (s, c, 0)
    const2 = lambda s, c: (0, 0)
    consts = (p["a_log4"], p["d_x"], p["gnorm_w"]) + _ssd_consts()
    return pl.pallas_call(
        functools.partial(_ssd_kernel, nb, len(init_args), s0.shape[0], n_valid, n_chunks),
        grid=(n_seq // nb, n_chunks),
        in_specs=[
            pl.BlockSpec((nb, rows, SSD_INNER), tok_map),
            pl.BlockSpec((nb, rows, SSD_CONV), tok_map),
            pl.BlockSpec((nb, rows, LANE), tok_map),
        ] + init_specs + [pl.BlockSpec(cst.shape, const2) for cst in consts],
        out_specs=[
            pl.BlockSpec((nb, rows, SSD_INNER), tok_map),
            pl.BlockSpec((nb, SSD_INNER, SSD_N), lambda s, c: (s, 0, 0)),
        ],
        out_shape=[
            jax.ShapeDtypeStruct((n_seq, t_len, SSD_INNER), BF16),
            jax.ShapeDtypeStruct((n_seq, SSD_INNER, SSD_N), F32),
        ],
        scratch_shapes=[pltpu.VMEM((nb, SSD_N, SSD_INNER), F32)],
        compiler_params=pltpu.CompilerParams(
            dimension_semantics=("parallel", "arbitrary"), vmem_limit_bytes=VMEM_LIMIT),
        name="ssd_mixer",
    )(zs, u, dt, *init_args, *consts)


def _pair_blockdiag(y2, bd):
    y16 = y2.astype(BF16)
    return jnp.concatenate([y16, y16], axis=0) * bd


def _unit_lower_inverses(n_list, eye, same16, same32, bd):
    def mm(a_list, b_list):
        return [_dot(a.astype(BF16), _pair_blockdiag(b, bd)) for a, b in zip(a_list, b_list)]

    def axpy(t_list, d_list, sign):
        return [t + sign * d for t, d in zip(t_list, d_list)]

    nd = [n * same16 for n in n_list]
    t = [eye - x for x in nd]
    pw = mm(nd, nd)
    for step in range(3):
        t = axpy(t, mm(t, pw), 1.0)
        if step < 2:
            pw = mm(pw, pw)
    n1 = [n * (same32 - same16) for n in n_list]
    t = axpy(t, mm(t, mm(n1, t)), -1.0)
    n2 = [n * (1.0 - same32) for n in n_list]
    t = axpy(t, mm(t, mm(n2, t)), -1.0)
    return t


def _gdn_kernel(nb, nch, n_s0, n_given, n_valid, n_steps,
                q_ref, k_ref, v_ref, zs_ref, g_ref, beta_ref, *rest):
    s0_refs = rest[:n_s0]
    ow_ref, e3_ref, tri_ref, sel_ref, bd_ref, masks_ref, y_ref, sfin_ref, st_scr = rest[n_s0:]
    L = CHUNK
    c = pl.program_id(1)
    seqs = range(nb * nch)

    def tok(ref, b, cols):
        if n_valid < L:
            return _pad_rows(ref[b % nb, :, cols], L)
        return ref[b % nb, (b // nb) * L:(b // nb + 1) * L, cols]

    @pl.when(c == 0)
    def _():
        for s in range(nb):
            st_scr[s] = _initial_state(s0_refs, s, nb, n_given)

    g = [tok(g_ref, b, slice(None)) for b in seqs]
    beta = [tok(beta_ref, b, slice(None)) for b in seqs]
    gc = [_cumsum_rows(tri_ref[...], g[b]) for b in seqs]
    stack = []
    for b in seqs:
        stack += [gc[b], beta[b]]
    ex = _dot(_split3_merge(jnp.concatenate(stack, axis=0)), e3_ref[...])
    gc_x = [ex[(2 * b) * L:(2 * b + 1) * L] for b in seqs]
    beta_x = [ex[(2 * b + 1) * L:(2 * b + 2) * L] for b in seqs]
    egc_x = [jnp.exp(x) for x in gc_x]
    egl_x = [jnp.exp(x[L - 1:L, :] - x) for x in gc_x]
    lane = _iota((L, LANE), 1)
    gc_t2 = [_pair_rows(sel_ref[...], [gc[b]]) for b in seqs]

    incl_add = masks_ref[0]
    strict = masks_ref[1]
    eye = masks_ref[2]
    same16 = masks_ref[3]
    same32 = masks_ref[4]
    bd = bd_ref[...]
    first = lane < L

    chains = [(b, h) for h in range(GDN_H) for b in seqs]
    pairs = [(b, p) for p in range(GDN_H // 2) for b in seqs]
    chain_of = lambda b, h: h * len(seqs) + b
    ks = [slice(h * GDN_DK, (h + 1) * GDN_DK) for h in range(GDN_H)]
    vs = [slice(h * GDN_DV, (h + 1) * GDN_DV) for h in range(GDN_H)]
    q = [tok(q_ref, b, ks[h]) for b, h in chains]
    k = [tok(k_ref, b, ks[h]) for b, h in chains]
    egc = [egc_x[b][:, ks[h]] for b, h in chains]
    kb = [k[i] * beta_x[b][:, ks[h]] for i, (b, h) in enumerate(chains)]
    n_ch = range(len(chains))
    zero_k = jnp.zeros((L, GDN_DK), BF16)

    kq2, dec2 = [], []
    for b, p in pairs:
        ia, ib = chain_of(b, 2 * p), chain_of(b, 2 * p + 1)
        lhs = jnp.concatenate([jnp.concatenate([kb[ia], kb[ib]], axis=1),
                               jnp.concatenate([q[ia], q[ib]], axis=1)], axis=0).astype(BF16)
        rhs_nt = jnp.concatenate([jnp.concatenate([k[ia].astype(BF16), zero_k], axis=1),
                                  jnp.concatenate([zero_k, k[ib].astype(BF16)], axis=1)], axis=0)
        kq2.append(_dot_nt(lhs, rhs_nt))
        col2 = jnp.where(first, gc_x[b][:, ks[2 * p]], gc_x[b][:, ks[2 * p + 1]])
        dec2.append(jnp.exp(col2 - jnp.broadcast_to(gc_t2[b][p:p + 1, :], (L, LANE)) + incl_add))
    n_pr = range(len(pairs))
    t_inv2 = _unit_lower_inverses([kq2[j][0:L] * dec2[j] * strict for j in n_pr], eye, same16, same32, bd)

    zero_r = jnp.zeros((L, GDN_DV + GDN_DK), BF16)
    uw = [None] * len(chains)
    for j, (b, p) in enumerate(pairs):
        halves = []
        for h in (2 * p, 2 * p + 1):
            i = chain_of(b, h)
            beta_h = beta_x[b][:, ks[h]]
            halves.append(jnp.concatenate([tok(v_ref, b, vs[h]) * jnp.concatenate([beta_h, beta_h], axis=1),
                                           kb[i] * egc[i]], axis=1).astype(BF16))
        rhs_bd = jnp.concatenate([jnp.concatenate([halves[0], zero_r], axis=1),
                                  jnp.concatenate([zero_r, halves[1]], axis=1)], axis=0)
        uw2 = _dot(t_inv2[j].astype(BF16), rhs_bd)
        width = GDN_DV + GDN_DK
        uw[chain_of(b, 2 * p)] = uw2[:, 0:width]
        uw[chain_of(b, 2 * p + 1)] = uw2[:, width:2 * width]
    zero_v = jnp.zeros((L, GDN_DV), BF16)
    for cc in range(nch):
        mine = [(i, b, h) for i, (b, h) in enumerate(chains) if b // nb == cc]
        s_old = {i: st_scr[b % nb, h] for i, b, h in mine}
        wq = {i: _dot(jnp.concatenate([uw[i][:, GDN_DV:], q[i] * egc[i]], axis=0).astype(BF16),
                      s_old[i].astype(BF16)) for i, b, h in mine}
        v_new = {i: (uw[i][:, :GDN_DV] - wq[i][0:L]).astype(BF16) for i, b, h in mine}
        o = {}
        for j, (b, p) in enumerate(pairs):
            if b // nb != cc:
                continue
            ia, ib = chain_of(b, 2 * p), chain_of(b, 2 * p + 1)
            v_bd = jnp.concatenate([jnp.concatenate([v_new[ia], zero_v], axis=1),
                                    jnp.concatenate([zero_v, v_new[ib]], axis=1)], axis=0)
            o2 = _dot((kq2[j][L:2 * L] * dec2[j]).astype(BF16), v_bd)
            o[ia] = wq[ia][L:2 * L] + o2[:, 0:GDN_DV]
            o[ib] = wq[ib][L:2 * L] + o2[:, GDN_DV:]
        for i, b, h in mine:
            e_last = egc[i][L - 1:L, :]
            st_scr[b % nb, h] = (s_old[i] * jnp.concatenate([e_last, e_last], axis=1)
                                 + _dot_tn((k[i] * egl_x[b][:, ks[h]]).astype(BF16), v_new[i]))
        for i, b, h in mine:
            o_h = o[i] * lax.rsqrt(jnp.mean(o[i] * o[i], axis=-1, keepdims=True) + EPS) * ow_ref[...]
            y_ref[b % nb, cc * L:cc * L + n_valid, vs[h]] = (o_h * tok(zs_ref, b, vs[h])).astype(BF16)[:n_valid]

    @pl.when(c == n_steps - 1)
    def _():
        for s in range(nb):
            sfin_ref[s] = st_scr[s]


def _gdn_consts():
    k = np.arange(LANE)[:, None]
    col = np.arange(GDN_KEY)[None, :]
    e3 = ((k % REP == col // GDN_DK) & (k < 3 * REP)).astype(BF16)
    t = np.arange(CHUNK)
    r, cc = t[:, None], t[None, :]
    tri, sel = _tri3(), _pair_sel(2 * GDN_H, GDN_H // 2)
    masks = np.stack([
        np.where(r >= cc, 0.0, NEG_BIG),
        (r > cc).astype(np.float32),
        (r == cc).astype(np.float32),
        ((r // (CHUNK // 4)) == (cc // (CHUNK // 4))).astype(np.float32),
        ((r // (CHUNK // 2)) == (cc // (CHUNK // 2))).astype(np.float32),
    ]).astype(np.float32)
    masks = np.concatenate([masks, masks], axis=-1)
    b = np.arange(2 * CHUNK)
    bd = ((b[:, None] // CHUNK) == (b[None, :] // CHUNK)).astype(BF16)
    return tuple(jnp.asarray(x) for x in (e3, tri, sel, bd, masks))


def _gdn_mixer(q, k, v, zs, g, beta, s0, p, n_valid):
    n_seq, t_len, _ = zs.shape
    assert n_valid == min(t_len, CHUNK) and (t_len % CHUNK == 0 or t_len == n_valid)
    n_chunks = t_len // n_valid
    nb = _seqs_per_step(n_seq)
    nch = GDN_CHUNKS_PER_STEP if (nb <= 2 and n_chunks % GDN_CHUNKS_PER_STEP == 0) else 1
    rows = nch * n_valid
    init_specs, init_args = _initial_state_specs(s0, nb, (GDN_H, GDN_DK, GDN_DV))
    tok_map = lambda s, c: (s, c, 0)
    e3, tri, sel, bd, masks = _gdn_consts()
    consts2 = (p["onorm_w"], e3, tri, sel, bd)
    return pl.pallas_call(
        functools.partial(_gdn_kernel, nb, nch, len(init_args), s0.shape[0], n_valid, n_chunks // nch),
        grid=(n_seq // nb, n_chunks // nch),
        in_specs=[
            pl.BlockSpec((nb, rows, GDN_KEY), tok_map),
            pl.BlockSpec((nb, rows, GDN_KEY), tok_map),
            pl.BlockSpec((nb, rows, GDN_VAL), tok_map),
            pl.BlockSpec((nb, rows, GDN_VAL), tok_map),
            pl.BlockSpec((nb, rows, LANE), tok_map),
            pl.BlockSpec((nb, rows, LANE), tok_map),
        ] + init_specs + [pl.BlockSpec(cst.shape, lambda s, c: (0, 0)) for cst in consts2]
          + [pl.BlockSpec(masks.shape, lambda s, c: (0, 0, 0))],
        out_specs=[
            pl.BlockSpec((nb, rows, GDN_VAL), tok_map),
            pl.BlockSpec((nb, GDN_H, GDN_DK, GDN_DV), lambda s, c: (s, 0, 0, 0)),
        ],
        out_shape=[
            jax.ShapeDtypeStruct((n_seq, t_len, GDN_VAL), BF16),
            jax.ShapeDtypeStruct((n_seq, GDN_H, GDN_DK, GDN_DV), F32),
        ],
        scratch_shapes=[pltpu.VMEM((nb, GDN_H, GDN_DK, GDN_DV), F32)],
        compiler_params=pltpu.CompilerParams(
            dimension_semantics=("parallel", "arbitrary"), vmem_limit_bytes=VMEM_LIMIT),
        name="gdn_mixer",
    )(q, k, v, zs, g, beta, *init_args, *consts2, masks)


def _rep_lanes(v, n_heads):
    n_rep = LANE // REP if n_heads == REP else 3
    row = jnp.pad(v, (0, REP - n_heads))
    return jnp.pad(jnp.tile(row, n_rep), (0, LANE - n_rep * REP)).reshape(1, LANE)


def _rep_cols(w, n_heads):
    n_rep = LANE // REP if n_heads == REP else 3
    blk = jnp.pad(w, ((0, 0), (0, REP - n_heads)))
    return jnp.pad(jnp.tile(blk, (1, n_rep)), ((0, 0), (0, LANE - n_rep * REP)))


def _ssd_state_in(s):
    return s.reshape(s.shape[0], SSD_INNER, SSD_N)


def _ssd_state_out(s):
    return s.reshape(s.shape[0], SSD_H, SSD_P, SSD_N)


def _tail_from_rows(rows3):
    return jnp.pad(rows3, ((0, 0), (TAIL - (CONV_K - 1), 0), (0, 0)))


def _segment_valid(n_valid, tm_in, spt):
    return tm_in // spt if n_valid == CHUNK else n_valid


def _ssd_layer(h, tail0, s0, p, layer, spt, tm_in, tm_out, n_valid, final_w=None):
    n_grp, rows, _ = h.shape
    weights = [(p["w_in"], layer), (p["w_dt4"][layer], None)]
    consts = [p["norm_w"][layer].reshape(1, D_MODEL), p["conv_w"][layer], p["conv_b"][layer].reshape(1, SSD_CONV),
              p["dt_bias4"][layer]]
    zs, u, dt, tails = _inproj_call(_ssd_inproj_kernel, h, tail0, spt, tm_in, _segment_valid(n_valid, tm_in, spt),
                                    weights, consts, (SSD_INNER, SSD_CONV, LANE), SSD_CONV, "ssd_inproj")
    seq = lambda a: a.reshape(n_grp * spt, rows // spt, a.shape[-1])
    mp = dict(a_log4=p["a_log4"][layer], d_x=p["d_x"][layer], gnorm_w=p["gnorm_w"][layer])
    y, s_fin = _ssd_mixer(seq(zs), seq(u), seq(dt), s0, mp, n_valid)
    h_new = _outproj(y.reshape(n_grp * rows, SSD_INNER), p["w_out"], layer, h.reshape(n_grp * rows, D_MODEL),
                     tm_out, final_w)
    return h_new.reshape(h.shape), tails, s_fin


def _gdn_layer(h, tail0, s0, p, layer, spt, tm_in, tm_out, n_valid, final_w=None):
    n_grp, rows, _ = h.shape
    weights = [(p["w_in"], layer), (p["w_ab"][layer], None)]
    consts = [p["norm_w"][layer].reshape(1, D_MODEL), p["conv_w"][layer], p["dt_bias"][layer], p["a_log"][layer]]
    q, k, v, zs, g, beta, tails = _inproj_call(
        _gdn_inproj_kernel, h, tail0, spt, tm_in, _segment_valid(n_valid, tm_in, spt), weights, consts,
        (GDN_KEY, GDN_KEY, GDN_VAL, GDN_VAL, LANE, LANE), GDN_CONV, "gdn_inproj")
    seq = lambda a: a.reshape(n_grp * spt, rows // spt, a.shape[-1])
    mp = dict(onorm_w=p["onorm_w"][layer].reshape(1, GDN_DV))
    y, s_fin = _gdn_mixer(seq(q), seq(k), seq(v), seq(zs), seq(g), seq(beta), s0, mp, n_valid)
    h_new = _outproj(y.reshape(n_grp * rows, GDN_VAL), p["w_out"], layer, h.reshape(n_grp * rows, D_MODEL),
                     tm_out, final_w)
    return h_new.reshape(h.shape), tails, s_fin


def kernel(x_prompt, x_sample, state_ssd, state_ssd_conv, state_gdn, state_gdn_conv, meta_tokens,
           ssd_norm_w, ssd_w_in, ssd_conv_w, ssd_conv_b, ssd_dt_bias, ssd_a_log, ssd_d, ssd_gnorm_w, ssd_w_out,
           gdn_norm_w, gdn_w_in, gdn_conv_w, gdn_dt_bias, gdn_a_log, gdn_onorm_w, gdn_w_out, final_norm_w):
    n_ssd, n_gdn = ssd_norm_w.shape[0], gdn_norm_w.shape[0]
    depth = n_ssd + n_gdn
    n_dec, dec_t, _ = x_sample.shape
    assert dec_t == N_META and N_META <= CHUNK and x_prompt.shape[1] % TM_PROMPT_OUT == 0
    keep = CONV_K - 1

    ssd_p = dict(
        norm_w=ssd_norm_w, w_in=ssd_w_in.astype(BF16), w_out=ssd_w_out.astype(BF16),
        w_dt4=[_rep_cols(ssd_w_in[j][:, SSD_INNER + SSD_CONV:], SSD_H).astype(BF16) for j in range(n_ssd)],
        conv_w=ssd_conv_w, conv_b=ssd_conv_b,
        dt_bias4=[_rep_lanes(ssd_dt_bias[j], SSD_H) for j in range(n_ssd)],
        a_log4=[_rep_lanes(ssd_a_log[j], SSD_H) for j in range(n_ssd)],
        d_x=[jnp.repeat(ssd_d[j], SSD_P).reshape(1, SSD_INNER) for j in range(n_ssd)],
        gnorm_w=[ssd_gnorm_w[j].reshape(1, SSD_INNER) for j in range(n_ssd)],
    )
    ab0 = GDN_CONV + GDN_VAL
    gdn_p = dict(
        norm_w=gdn_norm_w, w_in=gdn_w_in.astype(BF16), w_out=gdn_w_out.astype(BF16),
        w_ab=[jnp.concatenate([_rep_cols(gdn_w_in[j][:, ab0:ab0 + GDN_H], GDN_H),
                               _rep_cols(gdn_w_in[j][:, ab0 + GDN_H:], GDN_H)], axis=1).astype(BF16)
              for j in range(n_gdn)],
        conv_w=gdn_conv_w,
        dt_bias=[_rep_lanes(gdn_dt_bias[j], GDN_H) for j in range(n_gdn)],
        a_log=[_rep_lanes(gdn_a_log[j], GDN_H) for j in range(n_gdn)],
        onorm_w=gdn_onorm_w,
    )

    n_small = n_dec + 1
    hs = jnp.concatenate([x_sample, meta_tokens.astype(x_sample.dtype)[None]], axis=0)
    hs = hs.reshape(1, n_small * N_META, D_MODEL)
    hp = x_prompt
    tm_small = n_small * N_META
    last_rows = lambda tails: tails[:, TAIL - keep:]
    outs = {k: [] for k in ("p_ssd", "p_ssd_conv", "p_gdn", "p_gdn_conv", "s_ssd", "s_ssd_conv", "s_gdn", "s_gdn_conv")}
    for i in range(depth):
        j = i // 2
        fw = final_norm_w if i == depth - 1 else None
        if i % 2 == 0:
            tail_s = _tail_from_rows(jnp.concatenate([state_ssd_conv[j], jnp.zeros((1, keep, SSD_CONV), F32)], axis=0))
            s0_s = _ssd_state_in(state_ssd[j])
            hs, tails_s, sfin_s = _ssd_layer(hs, tail_s, s0_s, ssd_p, j, n_small, tm_small, tm_small, N_META, fw)
            hp, tails_p, sfin_p = _ssd_layer(hp, tails_s[n_dec:], sfin_s[n_dec:], ssd_p, j, 1, TM_PROMPT_IN,
                                             TM_PROMPT_OUT, CHUNK, fw)
            outs["s_ssd"].append(_ssd_state_out(sfin_s[:n_dec]))
            outs["p_ssd"].append(_ssd_state_out(sfin_p))
            outs["s_ssd_conv"].append(last_rows(tails_s[:n_dec]))
            outs["p_ssd_conv"].append(last_rows(tails_p))
        else:
            tail_s = _tail_from_rows(jnp.concatenate([state_gdn_conv[j], jnp.zeros((1, keep, GDN_CONV), F32)], axis=0))
            s0_s = state_gdn[j]
            hs, tails_s, sfin_s = _gdn_layer(hs, tail_s, s0_s, gdn_p, j, n_small, tm_small, tm_small, N_META, fw)
            hp, tails_p, sfin_p = _gdn_layer(hp, tails_s[n_dec:], sfin_s[n_dec:], gdn_p, j, 1, TM_PROMPT_IN,
                                             TM_PROMPT_OUT, CHUNK, fw)
            outs["s_gdn"].append(sfin_s[:n_dec])
            outs["p_gdn"].append(sfin_p)
            outs["s_gdn_conv"].append(last_rows(tails_s[:n_dec]))
            outs["p_gdn_conv"].append(last_rows(tails_p))
    y_sample = hs.reshape(n_small, N_META, D_MODEL)[:n_dec]
    st = lambda key: jnp.stack(outs[key])
    return (hp, y_sample, st("p_ssd"), st("p_ssd_conv"), st("p_gdn"), st("p_gdn_conv"),
            st("s_ssd"), st("s_ssd_conv"), st("s_gdn"), st("s_gdn_conv"))
```

```python
import functools

import jax
import jax.numpy as jnp
import numpy as np
from jax import lax
from jax.experimental import pallas as pl
from jax.experimental.pallas import tpu as pltpu

F32 = jnp.float32
BF16 = jnp.bfloat16

D_MODEL = 1024
N_META = 16
CONV_K = 4
EPS = 1e-6
CHUNK = 64
TAIL = 8
LANE = 128
COL_BLK = 512
SSD_INNER = 2048
SSD_P = 64
SSD_H = 32
SSD_G = 4
SSD_N = 128
SSD_GW = SSD_INNER // SSD_G
SSD_CONV = SSD_INNER + 2 * SSD_G * SSD_N
GDN_DK = 128
GDN_DV = 256
GDN_H = 8
GDN_KEY = GDN_H * GDN_DK
GDN_VAL = GDN_H * GDN_DV
GDN_CONV = 2 * GDN_KEY + GDN_VAL
REP = 32
NEG_BIG = -1e30

VMEM_LIMIT = 48 * 1024 * 1024
TM_PROMPT_IN = 256
TM_PROMPT_OUT = 1024
GDN_CHUNKS_PER_STEP = 2


NEG_LOG2E = -1.4426950408889634


def _sigmoid(x):
    return 1.0 / (1.0 + jnp.exp2(x * NEG_LOG2E))


def _silu(x):
    return x * _sigmoid(x)


def _softplus(x):
    return jnp.maximum(x, 0.0) + jnp.log1p(jnp.exp(-jnp.abs(x)))


def _dot(a, b, precision=None):
    return jnp.dot(a, b, preferred_element_type=F32, precision=precision)


def _dot_nt(a, b, precision=None):
    return lax.dot_general(a, b, (((1,), (1,)), ((), ())), preferred_element_type=F32, precision=precision)


def _dot_tn(a, b):
    return lax.dot_general(a, b, (((0,), (0,)), ((), ())), preferred_element_type=F32)


def _iota(shape, dim):
    return lax.broadcasted_iota(jnp.int32, shape, dim)


def _pad_rows(x, n_rows):
    if x.shape[0] == n_rows:
        return x
    return jnp.concatenate([x, jnp.zeros((n_rows - x.shape[0], x.shape[1]), x.dtype)], axis=0)


def _split3(x):
    hi = x.astype(BF16)
    r1 = x - hi.astype(F32)
    mid = r1.astype(BF16)
    lo = (r1 - mid.astype(F32)).astype(BF16)
    return hi, mid, lo


def _split3_merge(x):
    hi, mid, lo = _split3(x)
    lane = _iota(x.shape, 1)
    zero = jnp.zeros_like(hi)
    return jnp.where(lane < REP, hi, jnp.where(lane < 2 * REP, mid, jnp.where(lane < 3 * REP, lo, zero)))


def _cumsum_rows(tri3, x):
    return _dot(tri3, jnp.concatenate(_split3(x), axis=0))


def _pair_rows(sel3, values):
    odd = (_iota(values[0].shape, 1) & 1) == 1
    zero = jnp.zeros(values[0].shape, BF16)
    rows = []
    for v in values:
        m3 = _split3_merge(v)
        rows += [jnp.where(odd, zero, m3), jnp.where(odd, m3, zero)]
    return _dot_nt(sel3, jnp.concatenate(rows, axis=0))


def _rmsnorm_bf16(x, w):
    ms = jnp.mean(x * x, axis=-1, keepdims=True)
    return (x * lax.rsqrt(ms + EPS) * w).astype(BF16)


def _conv_block(tail_scr, s, cols, raw, n_valid, cw_ref, bias, tailo_ref):
    seg = raw.shape[0]
    x = [tail_scr[s, :, cols]] + [raw[i * TAIL:(i + 1) * TAIL] for i in range(seg // TAIL)]
    last = raw[n_valid - TAIL:n_valid]
    tail_scr[s, :, cols] = last
    tailo_ref[s, :, cols] = last
    row = _iota((TAIL, raw.shape[1]), 0)

    def shift_down(tiles, d):
        mixed = [tiles[0]] + [jnp.where(row >= TAIL - d, tiles[i - 1], tiles[i]) for i in range(1, len(tiles))]
        return [pltpu.roll(t, d, 0) for t in mixed]

    assert CONV_K == 4
    w = [cw_ref[k:k + 1, cols] for k in range(CONV_K)]
    x1 = shift_down(x, 1)
    near = [t * w[3] + t1 * w[2] for t, t1 in zip(x, x1)]
    far = shift_down([t * w[1] + t1 * w[0] for t, t1 in zip(x, x1)], 2)
    acc = jnp.concatenate([a + b for a, b in zip(near[1:], far[1:])], axis=0)
    if bias is not None:
        acc = acc + bias
    return acc


def _ssd_inproj_kernel(spt, seg, n_valid, x_ref, nw_ref, w_ref, wdt_ref, cw_ref, cb_ref, dtb_ref, tail_ref,
                       zs_ref, u_ref, dt_ref, tailo_ref, tail_scr):
    @pl.when(pl.program_id(1) == 0)
    def _():
        tail_scr[...] = tail_ref[...]

    xn = _rmsnorm_bf16(x_ref[0], nw_ref[...])
    dt_ref[0] = _softplus(_dot(xn, wdt_ref[...]) + dtb_ref[...])

    def conv_dot(j):
        return _dot(xn, w_ref[:, SSD_INNER + j * COL_BLK:SSD_INNER + (j + 1) * COL_BLK])

    n_conv, n_gate = SSD_CONV // COL_BLK, SSD_INNER // COL_BLK
    raw = conv_dot(0)
    for j in range(n_conv):
        cols = slice(j * COL_BLK, (j + 1) * COL_BLK)
        raw_next = conv_dot(j + 1) if j + 1 < n_conv else None
        gate = _dot(xn, w_ref[:, cols]) if j < n_gate else None
        for s in range(spt):
            acc = _conv_block(tail_scr, s, cols, raw[s * seg:(s + 1) * seg], n_valid, cw_ref, cb_ref[:, cols],
                              tailo_ref)
            u_ref[0, s * seg:(s + 1) * seg, cols] = _silu(acc)
        if gate is not None:
            zs_ref[0, :, cols] = _silu(gate)
        raw = raw_next


def _gdn_inproj_kernel(spt, seg, n_valid, x_ref, nw_ref, w_ref, wab_ref, cw_ref, dtb_ref, alog_ref, tail_ref,
                       q_ref, k_ref, v_ref, zs_ref, g_ref, beta_ref, tailo_ref, tail_scr):
    @pl.when(pl.program_id(1) == 0)
    def _():
        tail_scr[...] = tail_ref[...]

    xn = _rmsnorm_bf16(x_ref[0], nw_ref[...])
    ab = _dot(xn, wab_ref[...])
    lane = _iota((1, LANE), 1)
    head_lane = ((lane & (REP - 1)) < GDN_H) & (lane < 3 * REP)
    coef = jnp.where(head_lane, -jnp.exp(alog_ref[...]), 0.0)
    g_ref[0] = coef * _softplus(ab[:, :LANE] + dtb_ref[...])
    beta_ref[0] = _sigmoid(ab[:, LANE:])
    def conv_dot(j):
        return _dot(xn, w_ref[:, j * COL_BLK:(j + 1) * COL_BLK])

    n_conv, n_gate = GDN_CONV // COL_BLK, GDN_VAL // COL_BLK
    raw_next = conv_dot(0)
    for j in range(n_conv):
        cols = slice(j * COL_BLK, (j + 1) * COL_BLK)
        raw = raw_next
        raw_next = conv_dot(j + 1) if j + 1 < n_conv else None
        if j < n_gate:
            gate = _dot(xn, w_ref[:, GDN_CONV + j * COL_BLK:GDN_CONV + (j + 1) * COL_BLK])
            zs_ref[0, :, cols] = _silu(gate)
        for s in range(spt):
            rows = slice(s * seg, (s + 1) * seg)
            u = _silu(_conv_block(tail_scr, s, cols, raw[rows], n_valid, cw_ref, None, tailo_ref))
            if j * COL_BLK < 2 * GDN_KEY:
                is_q = j * COL_BLK < GDN_KEY
                dst = q_ref if is_q else k_ref
                off = j * COL_BLK - (0 if is_q else GDN_KEY)
                for i in range(COL_BLK // GDN_DK):
                    t = u[:, i * GDN_DK:(i + 1) * GDN_DK]
                    t = t * lax.rsqrt(jnp.sum(t * t, axis=-1, keepdims=True) + EPS)
                    if is_q:
                        t = t * (GDN_DK ** -0.5)
                    dst[0, rows, off + i * GDN_DK:off + (i + 1) * GDN_DK] = t
            else:
                off = j * COL_BLK - 2 * GDN_KEY
                v_ref[0, rows, off:off + COL_BLK] = u


def _inproj_call(body, x, tail0, spt, tm, n_valid, weights, consts, out_widths, conv_dim, name):
    n_grp, rows, _ = x.shape
    assert rows % tm == 0 and tm % spt == 0
    seg = tm // spt
    n_tiles = rows // tm
    assert (spt == 1 and n_valid == seg) or n_tiles == 1
    assert n_valid % TAIL == 0 and TAIL <= n_valid <= seg
    shared = tail0.shape[0] == 1
    tok_map = lambda g, t: (g, t, 0)
    tail_map = (lambda g, t: (0, 0, 0)) if shared else (lambda g, t: (g, 0, 0))
    in_specs = [pl.BlockSpec((1, tm, D_MODEL), tok_map), pl.BlockSpec((1, D_MODEL), lambda g, t: (0, 0))]
    args = [x, consts[0]]
    for w, layer in weights:
        if layer is None:
            in_specs.append(pl.BlockSpec(w.shape, lambda g, t: (0, 0), pipeline_mode=pl.Buffered(1)))
        else:
            in_specs.append(pl.BlockSpec((None,) + w.shape[1:], lambda g, t, layer=layer: (layer, 0, 0),
                                         pipeline_mode=pl.Buffered(1)))
        args.append(w)
    for cst in consts[1:]:
        in_specs.append(pl.BlockSpec(cst.shape, lambda g, t: (0, 0)))
        args.append(cst)
    in_specs.append(pl.BlockSpec((spt, TAIL, conv_dim), tail_map))
    args.append(tail0)
    out_specs = [pl.BlockSpec((1, tm, wd), tok_map) for wd in out_widths]
    out_shape = [jax.ShapeDtypeStruct((n_grp, rows, wd), F32) for wd in out_widths]
    out_specs.append(pl.BlockSpec((spt, TAIL, conv_dim), lambda g, t: (g, 0, 0)))
    out_shape.append(jax.ShapeDtypeStruct((n_grp * spt, TAIL, conv_dim), F32))
    return pl.pallas_call(
        functools.partial(body, spt, seg, n_valid),
        grid=(n_grp, n_tiles),
        in_specs=in_specs, out_specs=out_specs, out_shape=out_shape,
        scratch_shapes=[pltpu.VMEM((spt, TAIL, conv_dim), F32)],
        compiler_params=pltpu.CompilerParams(
            dimension_semantics=("parallel", "arbitrary"), vmem_limit_bytes=VMEM_LIMIT),
        name=name,
    )(*args)


def _outproj_kernel(final, y_ref, w_ref, h_ref, *rest):
    if final:
        fw_ref, o_ref = rest
    else:
        (o_ref,) = rest
    h = h_ref[...] + _dot(y_ref[...], w_ref[...])
    if final:
        ms = jnp.mean(h * h, axis=-1, keepdims=True)
        h = h * lax.rsqrt(ms + EPS) * fw_ref[...]
    o_ref[...] = h


def _outproj(y2d, w_stack, layer, h2d, tm, final_w=None):
    m, k = y2d.shape
    assert m % tm == 0
    final = final_w is not None
    in_specs = [
        pl.BlockSpec((tm, k), lambda i: (i, 0)),
        pl.BlockSpec((None, k, D_MODEL), lambda i: (layer, 0, 0), pipeline_mode=pl.Buffered(1)),
        pl.BlockSpec((tm, D_MODEL), lambda i: (i, 0)),
    ]
    args = [y2d, w_stack, h2d]
    if final:
        in_specs.append(pl.BlockSpec((1, D_MODEL), lambda i: (0, 0)))
        args.append(final_w.reshape(1, D_MODEL))
    return pl.pallas_call(
        functools.partial(_outproj_kernel, final),
        grid=(m // tm,),
        in_specs=in_specs,
        out_specs=pl.BlockSpec((tm, D_MODEL), lambda i: (i, 0)),
        out_shape=jax.ShapeDtypeStruct((m, D_MODEL), F32),
        compiler_params=pltpu.CompilerParams(
            dimension_semantics=("parallel",), vmem_limit_bytes=VMEM_LIMIT),
        name="outproj_final" if final else "outproj",
    )(*args)


def _initial_state(s0_refs, b, nb, n_given):
    if len(s0_refs) == 1:
        return s0_refs[0][0]
    val = s0_refs[b][0]
    return jnp.where(pl.program_id(0) * nb + b >= n_given, 0.0, val)


def _initial_state_specs(s0, nb, block):
    zeros = (0,) * len(block)
    if s0.shape[0] == 1:
        return [pl.BlockSpec((1,) + block, lambda s, c: (0,) + zeros)], [s0]
    last = s0.shape[0] - 1
    specs = [pl.BlockSpec((1,) + block, lambda s, c, i=i: (jnp.minimum(s * nb + i, last),) + zeros)
             for i in range(nb)]
    return specs, [s0] * nb


def _project_out(final, nb, y_scr, h_ref, wo_ref, fw_ref, o_ref):
    rows = y_scr.shape[1]
    proj = _dot(jnp.concatenate([y_scr[b] for b in range(nb)], axis=0), wo_ref[...])
    for b in range(nb):
        h = h_ref[b] + proj[b * rows:(b + 1) * rows]
        if final:
            ms = jnp.mean(h * h, axis=-1, keepdims=True)
            h = h * lax.rsqrt(ms + EPS) * fw_ref[...]
        o_ref[b] = h


def _ssd_kernel(nb, n_s0, n_given, n_valid, n_chunks, final, zs_ref, u_ref, dt_ref, *rest):
    s0_refs = rest[:n_s0]
    (alog_ref, dsk_ref, gw_ref, e3_ref, tri_ref, sel_ref, cmask_ref, bmask_ref,
     h_ref, wo_ref, fw_ref, o_ref, sfin_ref, st_scr, y_ref) = rest[n_s0:]
    L = CHUNK
    c = pl.program_id(1)
    seqs = range(nb)
    groups = range(SSD_G)
    gn = [slice(g * SSD_N, (g + 1) * SSD_N) for g in groups]
    gw = [slice(g * SSD_GW, (g + 1) * SSD_GW) for g in groups]

    @pl.when(c == 0)
    def _():
        for b in seqs:
            st_scr[b] = _initial_state(s0_refs, b, nb, n_given).T

    def tok(ref, b, cols):
        return _pad_rows(ref[b, :, cols], L)

    bm_b = [tok(u_ref, b, slice(SSD_INNER, SSD_INNER + SSD_G * SSD_N)).astype(BF16) for b in seqs]
    cm_b = [tok(u_ref, b, slice(SSD_INNER + SSD_G * SSD_N, SSD_CONV)).astype(BF16) for b in seqs]

    dt4 = [tok(dt_ref, b, slice(None)) for b in seqs]
    neg_a = -jnp.exp(alog_ref[...])
    cs4 = [_cumsum_rows(tri_ref[...], dt4[b] * neg_a) for b in seqs]
    stack3 = [_split3_merge(jnp.concatenate([cs4[b], dt4[b] * jnp.exp(cs4[b][L - 1:L, :] - cs4[b])], axis=0))
              for b in seqs]

    t2 = [_pair_rows(sel_ref[...], [cs4[b], dt4[b]]) for b in seqs]
    cb2 = [[_dot_nt(cm_b[b][:, gn[g]], jnp.concatenate([bm_b[b][:, gn[g]]] * 2, axis=0)) for b in seqs]
           for g in groups]

    slab = 4 * L
    n_slab = SSD_INNER // slab
    per_group = SSD_GW // slab

    def expand(j):
        return [_dot(stack3[b], e3_ref[:, j * slab:(j + 1) * slab]) for b in seqs]

    ex_next = expand(0)
    gated = [[] for _ in seqs]
    for j in range(n_slab):
        g = j // per_group
        lanes = slice(j * slab, (j + 1) * slab)
        ex = ex_next
        if j + 1 < n_slab:
            ex_next = expand(j + 1)
        st_old = [st_scr[b, :, lanes] for b in seqs]
        y_off = [_dot(cm_b[b][:, gn[g]], st_old[b].astype(BF16)) for b in seqs]
        xs, y_in, ecs, dtw = [], [], [], []
        for b in seqs:
            cs_x, dtw_x = ex[b][0:L], ex[b][L:2 * L]
            rows = [jnp.concatenate([jnp.broadcast_to(t2[b][2 * j + i:2 * j + i + 1, off:off + LANE], (L, LANE))
                                     for i in range(slab // LANE)], axis=1) for off in (0, LANE)]
            decay = jnp.exp(cs_x - rows[0] + cmask_ref[...]) * rows[1]
            m = (jnp.concatenate([cb2[g][b]] * (slab // LANE), axis=1) * decay).astype(BF16)
            x = tok(u_ref, b, lanes)
            rhs = jnp.concatenate([x.astype(BF16)] * 4, axis=0) * bmask_ref[...]
            y_in.append(_dot(m, rhs))
            xs.append(x)
            ecs.append(jnp.exp(cs_x))
            dtw.append(dtw_x)
        for b in seqs:
            st_scr[b, :, lanes] = (st_old[b] * ecs[b][L - 1:L, :]
                                   + _dot_tn(bm_b[b][:, gn[g]], (xs[b] * dtw[b]).astype(BF16)))
            y = y_in[b] + y_off[b] * ecs[b] + xs[b] * dsk_ref[:, lanes]
            gated[b].append(y * tok(zs_ref, b, lanes))
        if (j + 1) % per_group == 0:
            for b in seqs:
                blk_g = jnp.concatenate(gated[b], axis=1)
                gated[b] = []
                ms = jnp.mean(blk_g * blk_g, axis=-1, keepdims=True)
                y_ref[b, :, gw[g]] = (blk_g * lax.rsqrt(ms + EPS) * gw_ref[:, gw[g]]).astype(BF16)[:n_valid]
    _project_out(final, nb, y_ref, h_ref, wo_ref, fw_ref, o_ref)

    @pl.when(c == n_chunks - 1)
    def _():
        for b in seqs:
            sfin_ref[b] = st_scr[b].T


def _tri3():
    t = np.arange(CHUNK)
    tri = (t[:, None] >= t[None, :]).astype(BF16)
    return np.concatenate([tri, tri, tri], axis=1)


def _pair_sel(n_rows, n_pairs):
    j = np.arange(n_rows)[:, None]
    kk = np.arange(LANE)[None, :]
    return (((kk % REP) // 2 == j) & (kk < 3 * REP) & (j < n_pairs)).astype(BF16)


def _ssd_consts():
    k = np.arange(LANE)[:, None]
    col = np.arange(SSD_INNER)[None, :]
    e3 = ((k % REP == col // SSD_P) & (k < 3 * REP)).astype(BF16)
    t = np.arange(CHUNK)
    tri, sel = _tri3(), _pair_sel(SSD_H // 2, SSD_H // 2)
    b = np.arange(4 * CHUNK)
    cmask = np.where(t[:, None] >= (b[None, :] % CHUNK), 0.0, NEG_BIG).astype(np.float32)
    bmask = ((b[:, None] // CHUNK) == (b[None, :] // CHUNK)).astype(BF16)
    return tuple(jnp.asarray(x) for x in (e3, tri, sel, cmask, bmask))


def _seqs_per_step(n_seq):
    return 2 if n_seq % 2 == 0 else (3 if n_seq % 3 == 0 else 1)


def _out_proj_specs(h, w_out, layer, final_w, nb, rows):
    specs = [pl.BlockSpec((nb, rows, D_MODEL), lambda s, c: (s, c, 0)),
             pl.BlockSpec((None,) + w_out.shape[1:], lambda s, c: (layer, 0, 0), pipeline_mode=pl.Buffered(1)),
             pl.BlockSpec((1, D_MODEL), lambda s, c: (0, 0))]
    return specs, [h, w_out, final_w.reshape(1, D_MODEL)]


def _ssd_mixer(zs, u, dt, s0, p, n_valid, h, w_out, layer, final_w, final):
    n_seq, t_len, _ = zs.shape
    assert n_valid == min(t_len, CHUNK) and (t_len % CHUNK == 0 or t_len == n_valid)
    rows = n_valid
    n_chunks = t_len // rows
    nb = _seqs_per_step(n_seq)
    init_specs, init_args = _initial_state_specs(s0, nb, (SSD_INNER, SSD_N))
    tok_map = lambda s, c: (s, c, 0)
    const2 = lambda s, c: (0, 0)
    consts = (p["a_log4"], p["d_x"], p["gnorm_w"]) + _ssd_consts()
    proj_specs, proj_args = _out_proj_specs(h, w_out, layer, final_w, nb, rows)
    return pl.pallas_call(
        functools.partial(_ssd_kernel, nb, len(init_args), s0.shape[0], n_valid, n_chunks, final),
        grid=(n_seq // nb, n_chunks),
        in_specs=[
            pl.BlockSpec((nb, rows, SSD_INNER), tok_map),
            pl.BlockSpec((nb, rows, SSD_CONV), tok_map),
            pl.BlockSpec((nb, rows, LANE), tok_map),
        ] + init_specs + [pl.BlockSpec(cst.shape, const2) for cst in consts] + proj_specs,
        out_specs=[
            pl.BlockSpec((nb, rows, D_MODEL), tok_map),
            pl.BlockSpec((nb, SSD_INNER, SSD_N), lambda s, c: (s, 0, 0)),
        ],
        out_shape=[
            jax.ShapeDtypeStruct((n_seq, t_len, D_MODEL), F32),
            jax.ShapeDtypeStruct((n_seq, SSD_INNER, SSD_N), F32),
        ],
        scratch_shapes=[pltpu.VMEM((nb, SSD_N, SSD_INNER), F32), pltpu.VMEM((nb, rows, SSD_INNER), BF16)],
        compiler_params=pltpu.CompilerParams(
            dimension_semantics=("parallel", "arbitrary"), vmem_limit_bytes=VMEM_LIMIT),
        name="ssd_mixer",
    )(zs, u, dt, *init_args, *consts, *proj_args)


def _pair_blockdiag(y2, bd):
    y16 = y2.astype(BF16)
    return jnp.concatenate([y16, y16], axis=0) * bd


def _unit_lower_inverses(n_list, eye, same16, same32, bd):
    def mm(a_list, b_list):
        return [_dot(a.astype(BF16), _pair_blockdiag(b, bd)) for a, b in zip(a_list, b_list)]

    def axpy(t_list, d_list, sign):
        return [t + sign * d for t, d in zip(t_list, d_list)]

    nd = [n * same16 for n in n_list]
    t = [eye - x for x in nd]
    pw = mm(nd, nd)
    for step in range(3):
        t = axpy(t, mm(t, pw), 1.0)
        if step < 2:
            pw = mm(pw, pw)
    n1 = [n * (same32 - same16) for n in n_list]
    t = axpy(t, mm(t, mm(n1, t)), -1.0)
    n2 = [n * (1.0 - same32) for n in n_list]
    t = axpy(t, mm(t, mm(n2, t)), -1.0)
    return t


def _gdn_kernel(nb, nch, n_s0, n_given, n_valid, n_steps, final,
                q_ref, k_ref, v_ref, zs_ref, g_ref, beta_ref, *rest):
    s0_refs = rest[:n_s0]
    (ow_ref, e3_ref, tri_ref, sel_ref, bd_ref, masks_ref,
     h_ref, wo_ref, fw_ref, o_ref, sfin_ref, st_scr, y_ref) = rest[n_s0:]
    L = CHUNK
    c = pl.program_id(1)
    seqs = range(nb * nch)

    def tok(ref, b, cols):
        if n_valid < L:
            return _pad_rows(ref[b % nb, :, cols], L)
        return ref[b % nb, (b // nb) * L:(b // nb + 1) * L, cols]

    @pl.when(c == 0)
    def _():
        for s in range(nb):
            st_scr[s] = _initial_state(s0_refs, s, nb, n_given)

    g = [tok(g_ref, b, slice(None)) for b in seqs]
    beta = [tok(beta_ref, b, slice(None)) for b in seqs]
    gc = [_cumsum_rows(tri_ref[...], g[b]) for b in seqs]
    stack = []
    for b in seqs:
        stack += [gc[b], beta[b]]
    ex = _dot(_split3_merge(jnp.concatenate(stack, axis=0)), e3_ref[...])
    gc_x = [ex[(2 * b) * L:(2 * b + 1) * L] for b in seqs]
    beta_x = [ex[(2 * b + 1) * L:(2 * b + 2) * L] for b in seqs]
    egc_x = [jnp.exp(x) for x in gc_x]
    egl_x = [jnp.exp(x[L - 1:L, :] - x) for x in gc_x]
    lane = _iota((L, LANE), 1)
    gc_t2 = [_pair_rows(sel_ref[...], [gc[b]]) for b in seqs]

    incl_add = masks_ref[0]
    strict = masks_ref[1]
    eye = masks_ref[2]
    same16 = masks_ref[3]
    same32 = masks_ref[4]
    bd = bd_ref[...]
    first = lane < L

    chains = [(b, h) for h in range(GDN_H) for b in seqs]
    pairs = [(b, p) for p in range(GDN_H // 2) for b in seqs]
    chain_of = lambda b, h: h * len(seqs) + b
    ks = [slice(h * GDN_DK, (h + 1) * GDN_DK) for h in range(GDN_H)]
    vs = [slice(h * GDN_DV, (h + 1) * GDN_DV) for h in range(GDN_H)]
    q = [tok(q_ref, b, ks[h]) for b, h in chains]
    k = [tok(k_ref, b, ks[h]) for b, h in chains]
    egc = [egc_x[b][:, ks[h]] for b, h in chains]
    kb = [k[i] * beta_x[b][:, ks[h]] for i, (b, h) in enumerate(chains)]
    n_ch = range(len(chains))
    zero_k = jnp.zeros((L, GDN_DK), BF16)

    kq2, dec2 = [], []
    for b, p in pairs:
        ia, ib = chain_of(b, 2 * p), chain_of(b, 2 * p + 1)
        lhs = jnp.concatenate([jnp.concatenate([kb[ia], kb[ib]], axis=1),
                               jnp.concatenate([q[ia], q[ib]], axis=1)], axis=0).astype(BF16)
        rhs_nt = jnp.concatenate([jnp.concatenate([k[ia].astype(BF16), zero_k], axis=1),
                                  jnp.concatenate([zero_k, k[ib].astype(BF16)], axis=1)], axis=0)
        kq2.append(_dot_nt(lhs, rhs_nt))
        col2 = jnp.where(first, gc_x[b][:, ks[2 * p]], gc_x[b][:, ks[2 * p + 1]])
        dec2.append(jnp.exp(col2 - jnp.broadcast_to(gc_t2[b][p:p + 1, :], (L, LANE)) + incl_add))
    n_pr = range(len(pairs))
    t_inv2 = _unit_lower_inverses([kq2[j][0:L] * dec2[j] * strict for j in n_pr], eye, same16, same32, bd)

    zero_r = jnp.zeros((L, GDN_DV + GDN_DK), BF16)
    uw = [None] * len(chains)
    for j, (b, p) in enumerate(pairs):
        halves = []
        for h in (2 * p, 2 * p + 1):
            i = chain_of(b, h)
            beta_h = beta_x[b][:, ks[h]]
            halves.append(jnp.concatenate([tok(v_ref, b, vs[h]) * jnp.concatenate([beta_h, beta_h], axis=1),
                                           kb[i] * egc[i]], axis=1).astype(BF16))
        rhs_bd = jnp.concatenate([jnp.concatenate([halves[0], zero_r], axis=1),
                                  jnp.concatenate([zero_r, halves[1]], axis=1)], axis=0)
        uw2 = _dot(t_inv2[j].astype(BF16), rhs_bd)
        width = GDN_DV + GDN_DK
        uw[chain_of(b, 2 * p)] = uw2[:, 0:width]
        uw[chain_of(b, 2 * p + 1)] = uw2[:, width:2 * width]
    zero_v = jnp.zeros((L, GDN_DV), BF16)
    for cc in range(nch):
        mine = [(i, b, h) for i, (b, h) in enumerate(chains) if b // nb == cc]
        s_old = {i: st_scr[b % nb, h] for i, b, h in mine}
        wq = {i: _dot(jnp.concatenate([uw[i][:, GDN_DV:], q[i] * egc[i]], axis=0).astype(BF16),
                      s_old[i].astype(BF16)) for i, b, h in mine}
        v_new = {i: (uw[i][:, :GDN_DV] - wq[i][0:L]).astype(BF16) for i, b, h in mine}
        o = {}
        for j, (b, p) in enumerate(pairs):
            if b // nb != cc:
                continue
            ia, ib = chain_of(b, 2 * p), chain_of(b, 2 * p + 1)
            v_bd = jnp.concatenate([jnp.concatenate([v_new[ia], zero_v], axis=1),
                                    jnp.concatenate([zero_v, v_new[ib]], axis=1)], axis=0)
            o2 = _dot((kq2[j][L:2 * L] * dec2[j]).astype(BF16), v_bd)
            o[ia] = wq[ia][L:2 * L] + o2[:, 0:GDN_DV]
            o[ib] = wq[ib][L:2 * L] + o2[:, GDN_DV:]
        for i, b, h in mine:
            e_last = egc[i][L - 1:L, :]
            st_scr[b % nb, h] = (s_old[i] * jnp.concatenate([e_last, e_last], axis=1)
                                 + _dot_tn((k[i] * egl_x[b][:, ks[h]]).astype(BF16), v_new[i]))
        for i, b, h in mine:
            o_h = o[i] * lax.rsqrt(jnp.mean(o[i] * o[i], axis=-1, keepdims=True) + EPS) * ow_ref[...]
            y_ref[b % nb, cc * L:cc * L + n_valid, vs[h]] = (o_h * tok(zs_ref, b, vs[h])).astype(BF16)[:n_valid]
    _project_out(final, nb, y_ref, h_ref, wo_ref, fw_ref, o_ref)

    @pl.when(c == n_steps - 1)
    def _():
        for s in range(nb):
            sfin_ref[s] = st_scr[s]


def _gdn_consts():
    k = np.arange(LANE)[:, None]
    col = np.arange(GDN_KEY)[None, :]
    e3 = ((k % REP == col // GDN_DK) & (k < 3 * REP)).astype(BF16)
    t = np.arange(CHUNK)
    r, cc = t[:, None], t[None, :]
    tri, sel = _tri3(), _pair_sel(2 * GDN_H, GDN_H // 2)
    masks = np.stack([
        np.where(r >= cc, 0.0, NEG_BIG),
        (r > cc).astype(np.float32),
        (r == cc).astype(np.float32),
        ((r // (CHUNK // 4)) == (cc // (CHUNK // 4))).astype(np.float32),
        ((r // (CHUNK // 2)) == (cc // (CHUNK // 2))).astype(np.float32),
    ]).astype(np.float32)
    masks = np.concatenate([masks, masks], axis=-1)
    b = np.arange(2 * CHUNK)
    bd = ((b[:, None] // CHUNK) == (b[None, :] // CHUNK)).astype(BF16)
    return tuple(jnp.asarray(x) for x in (e3, tri, sel, bd, masks))


def _gdn_mixer(q, k, v, zs, g, beta, s0, p, n_valid, h, w_out, layer, final_w, final):
    n_seq, t_len, _ = zs.shape
    assert n_valid == min(t_len, CHUNK) and (t_len % CHUNK == 0 or t_len == n_valid)
    n_chunks = t_len // n_valid
    nb = _seqs_per_step(n_seq)
    nch = GDN_CHUNKS_PER_STEP if (nb <= 2 and n_chunks % GDN_CHUNKS_PER_STEP == 0) else 1
    rows = nch * n_valid
    init_specs, init_args = _initial_state_specs(s0, nb, (GDN_H, GDN_DK, GDN_DV))
    tok_map = lambda s, c: (s, c, 0)
    e3, tri, sel, bd, masks = _gdn_consts()
    consts2 = (p["onorm_w"], e3, tri, sel, bd)
    proj_specs, proj_args = _out_proj_specs(h, w_out, layer, final_w, nb, rows)
    return pl.pallas_call(
        functools.partial(_gdn_kernel, nb, nch, len(init_args), s0.shape[0], n_valid, n_chunks // nch, final),
        grid=(n_seq // nb, n_chunks // nch),
        in_specs=[
            pl.BlockSpec((nb, rows, GDN_KEY), tok_map),
            pl.BlockSpec((nb, rows, GDN_KEY), tok_map),
            pl.BlockSpec((nb, rows, GDN_VAL), tok_map),
            pl.BlockSpec((nb, rows, GDN_VAL), tok_map),
            pl.BlockSpec((nb, rows, LANE), tok_map),
            pl.BlockSpec((nb, rows, LANE), tok_map),
        ] + init_specs + [pl.BlockSpec(cst.shape, lambda s, c: (0, 0)) for cst in consts2]
          + [pl.BlockSpec(masks.shape, lambda s, c: (0, 0, 0))] + proj_specs,
        out_specs=[
            pl.BlockSpec((nb, rows, D_MODEL), tok_map),
            pl.BlockSpec((nb, GDN_H, GDN_DK, GDN_DV), lambda s, c: (s, 0, 0, 0)),
        ],
        out_shape=[
            jax.ShapeDtypeStruct((n_seq, t_len, D_MODEL), F32),
            jax.ShapeDtypeStruct((n_seq, GDN_H, GDN_DK, GDN_DV), F32),
        ],
        scratch_shapes=[pltpu.VMEM((nb, GDN_H, GDN_DK, GDN_DV), F32), pltpu.VMEM((nb, rows, GDN_VAL), BF16)],
        compiler_params=pltpu.CompilerParams(
            dimension_semantics=("parallel", "arbitrary"), vmem_limit_bytes=VMEM_LIMIT),
        name="gdn_mixer",
    )(q, k, v, zs, g, beta, *init_args, *consts2, masks, *proj_args)


def _rep_lanes(v, n_heads):
    n_rep = LANE // REP if n_heads == REP else 3
    row = jnp.pad(v, (0, REP - n_heads))
    return jnp.pad(jnp.tile(row, n_rep), (0, LANE - n_rep * REP)).reshape(1, LANE)


def _rep_cols(w, n_heads):
    n_rep = LANE // REP if n_heads == REP else 3
    blk = jnp.pad(w, ((0, 0), (0, REP - n_heads)))
    return jnp.pad(jnp.tile(blk, (1, n_rep)), ((0, 0), (0, LANE - n_rep * REP)))


def _ssd_state_in(s):
    return s.reshape(s.shape[0], SSD_INNER, SSD_N)


def _ssd_state_out(s):
    return s.reshape(s.shape[0], SSD_H, SSD_P, SSD_N)


def _tail_from_rows(rows3):
    return jnp.pad(rows3, ((0, 0), (TAIL - (CONV_K - 1), 0), (0, 0)))


def _segment_valid(n_valid, tm_in, spt):
    return tm_in // spt if n_valid == CHUNK else n_valid


def _ssd_layer(h, tail0, s0, p, layer, spt, tm_in, n_valid, final_w, final):
    n_grp, rows, _ = h.shape
    weights = [(p["w_in"], layer), (p["w_dt4"][layer], None)]
    consts = [p["norm_w"][layer].reshape(1, D_MODEL), p["conv_w"][layer], p["conv_b"][layer].reshape(1, SSD_CONV),
              p["dt_bias4"][layer]]
    zs, u, dt, tails = _inproj_call(_ssd_inproj_kernel, h, tail0, spt, tm_in, _segment_valid(n_valid, tm_in, spt),
                                    weights, consts, (SSD_INNER, SSD_CONV, LANE), SSD_CONV, "ssd_inproj")
    seq = lambda a: a.reshape(n_grp * spt, rows // spt, a.shape[-1])
    mp = dict(a_log4=p["a_log4"][layer], d_x=p["d_x"][layer], gnorm_w=p["gnorm_w"][layer])
    h_new, s_fin = _ssd_mixer(seq(zs), seq(u), seq(dt), s0, mp, n_valid, seq(h), p["w_out"], layer, final_w, final)
    return h_new.reshape(h.shape), tails, s_fin


def _gdn_layer(h, tail0, s0, p, layer, spt, tm_in, n_valid, final_w, final):
    n_grp, rows, _ = h.shape
    weights = [(p["w_in"], layer), (p["w_ab"][layer], None)]
    consts = [p["norm_w"][layer].reshape(1, D_MODEL), p["conv_w"][layer], p["dt_bias"][layer], p["a_log"][layer]]
    q, k, v, zs, g, beta, tails = _inproj_call(
        _gdn_inproj_kernel, h, tail0, spt, tm_in, _segment_valid(n_valid, tm_in, spt), weights, consts,
        (GDN_KEY, GDN_KEY, GDN_VAL, GDN_VAL, LANE, LANE), GDN_CONV, "gdn_inproj")
    seq = lambda a: a.reshape(n_grp * spt, rows // spt, a.shape[-1])
    mp = dict(onorm_w=p["onorm_w"][layer].reshape(1, GDN_DV))
    h_new, s_fin = _gdn_mixer(seq(q), seq(k), seq(v), seq(zs), seq(g), seq(beta), s0, mp, n_valid,
                              seq(h), p["w_out"], layer, final_w, final)
    return h_new.reshape(h.shape), tails, s_fin


def kernel(x_prompt, x_sample, state_ssd, state_ssd_conv, state_gdn, state_gdn_conv, meta_tokens,
           ssd_norm_w, ssd_w_in, ssd_conv_w, ssd_conv_b, ssd_dt_bias, ssd_a_log, ssd_d, ssd_gnorm_w, ssd_w_out,
           gdn_norm_w, gdn_w_in, gdn_conv_w, gdn_dt_bias, gdn_a_log, gdn_onorm_w, gdn_w_out, final_norm_w):
    n_ssd, n_gdn = ssd_norm_w.shape[0], gdn_norm_w.shape[0]
    depth = n_ssd + n_gdn
    n_dec, dec_t, _ = x_sample.shape
    assert dec_t == N_META and N_META <= CHUNK and x_prompt.shape[1] % TM_PROMPT_OUT == 0
    keep = CONV_K - 1

    ssd_p = dict(
        norm_w=ssd_norm_w, w_in=ssd_w_in.astype(BF16), w_out=ssd_w_out.astype(BF16),
        w_dt4=[_rep_cols(ssd_w_in[j][:, SSD_INNER + SSD_CONV:], SSD_H).astype(BF16) for j in range(n_ssd)],
        conv_w=ssd_conv_w, conv_b=ssd_conv_b,
        dt_bias4=[_rep_lanes(ssd_dt_bias[j], SSD_H) for j in range(n_ssd)],
        a_log4=[_rep_lanes(ssd_a_log[j], SSD_H) for j in range(n_ssd)],
        d_x=[jnp.repeat(ssd_d[j], SSD_P).reshape(1, SSD_INNER) for j in range(n_ssd)],
        gnorm_w=[ssd_gnorm_w[j].reshape(1, SSD_INNER) for j in range(n_ssd)],
    )
    ab0 = GDN_CONV + GDN_VAL
    gdn_p = dict(
        norm_w=gdn_norm_w, w_in=gdn_w_in.astype(BF16), w_out=gdn_w_out.astype(BF16),
        w_ab=[jnp.concatenate([_rep_cols(gdn_w_in[j][:, ab0:ab0 + GDN_H], GDN_H),
                               _rep_cols(gdn_w_in[j][:, ab0 + GDN_H:], GDN_H)], axis=1).astype(BF16)
              for j in range(n_gdn)],
        conv_w=gdn_conv_w,
        dt_bias=[_rep_lanes(gdn_dt_bias[j], GDN_H) for j in range(n_gdn)],
        a_log=[_rep_lanes(gdn_a_log[j], GDN_H) for j in range(n_gdn)],
        onorm_w=gdn_onorm_w,
    )

    n_small = n_dec + 1
    hs = jnp.concatenate([x_sample, meta_tokens.astype(x_sample.dtype)[None]], axis=0)
    hs = hs.reshape(1, n_small * N_META, D_MODEL)
    hp = x_prompt
    tm_small = n_small * N_META
    last_rows = lambda tails: tails[:, TAIL - keep:]
    outs = {k: [] for k in ("p_ssd", "p_ssd_conv", "p_gdn", "p_gdn_conv", "s_ssd", "s_ssd_conv", "s_gdn", "s_gdn_conv")}
    for i in range(depth):
        j = i // 2
        fw, last = final_norm_w, i == depth - 1
        if i % 2 == 0:
            tail_s = _tail_from_rows(jnp.concatenate([state_ssd_conv[j], jnp.zeros((1, keep, SSD_CONV), F32)], axis=0))
            s0_s = _ssd_state_in(state_ssd[j])
            hs, tails_s, sfin_s = _ssd_layer(hs, tail_s, s0_s, ssd_p, j, n_small, tm_small, N_META, fw, last)
            hp, tails_p, sfin_p = _ssd_layer(hp, tails_s[n_dec:], sfin_s[n_dec:], ssd_p, j, 1, TM_PROMPT_IN,
                                             CHUNK, fw, last)
            outs["s_ssd"].append(_ssd_state_out(sfin_s[:n_dec]))
            outs["p_ssd"].append(_ssd_state_out(sfin_p))
            outs["s_ssd_conv"].append(last_rows(tails_s[:n_dec]))
            outs["p_ssd_conv"].append(last_rows(tails_p))
        else:
            tail_s = _tail_from_rows(jnp.concatenate([state_gdn_conv[j], jnp.zeros((1, keep, GDN_CONV), F32)], axis=0))
            s0_s = state_gdn[j]
            hs, tails_s, sfin_s = _gdn_layer(hs, tail_s, s0_s, gdn_p, j, n_small, tm_small, N_META, fw, last)
            hp, tails_p, sfin_p = _gdn_layer(hp, tails_s[n_dec:], sfin_s[n_dec:], gdn_p, j, 1, TM_PROMPT_IN,
                                             CHUNK, fw, last)
            outs["s_gdn"].append(sfin_s[:n_dec])
            outs["p_gdn"].append(sfin_p)
            outs["s_gdn_conv"].append(last_rows(tails_s[:n_dec]))
            outs["p_gdn_conv"].append(last_rows(tails_p))
    y_sample = hs.reshape(n_small, N_META, D_MODEL)[:n_dec]
    st = lambda key: jnp.stack(outs[key])
    return (hp, y_sample, st("p_ssd"), st("p_ssd_conv"), st("p_gdn"), st("p_gdn_conv"),
            st("s_ssd"), st("s_ssd_conv"), st("s_gdn"), st("s_gdn_conv"))
```
